```python
import jax, jax.numpy as jnp
from jax import lax
import numpy as np

D_MODEL = 1024
BATCH = 4
SEQ = 4096
DEPTH = 4
DEC_BATCH = 128
DEC_SEQ = 1
PAST_LEN = 2048
PAGE_SIZE = 128

HEAD_DIM = 64
NSA_HEADS = 8
NSA_KV_HEADS = 2
NSA_REP = NSA_HEADS // NSA_KV_HEADS
NSA_WIDTH = NSA_HEADS * HEAD_DIM
RWKV_HEADS = 8
RWKV_WIDTH = RWKV_HEADS * HEAD_DIM
CMP_BLOCK = 32
CMP_STRIDE = 16
CMP_HIDDEN = 256
SEL_BLOCK = 64
SEL_TOPN = 16
WINDOW = 512
Q_BLOCK = 128
DECAY_RANK = 64
A_RANK = 64
GATE_RANK = 128
FFN_HIDDEN = -(-8 * D_MODEL // (3 * 256)) * 256
PLE_DIM = 256
Q_COLS = NSA_WIDTH
KV_COLS = 4 * NSA_KV_HEADS * HEAD_DIM
WIN_COLS = 2 * NSA_KV_HEADS * HEAD_DIM
GATE_COLS = 3 * NSA_HEADS
RWKV_PROJ = 3 * RWKV_WIDTH + DECAY_RANK + A_RANK + GATE_RANK
IN_COLS = Q_COLS + KV_COLS + WIN_COLS + GATE_COLS + RWKV_PROJ
NORM_EPS = 1e-6
GN_EPS = 64e-5
BIG = 1e9

kernel_name = 'hybrid_nsa_rwkv7_decoder_step'


def rmsnorm(x, g):
    x32 = x.astype(jnp.float32)
    y = x32 * lax.rsqrt(jnp.mean(x32 * x32, axis=-1, keepdims=True) + NORM_EPS)
    return (y * g.astype(jnp.float32)).astype(x.dtype)


def masked_softmax(s, mask):
    s = jnp.where(mask, s.astype(jnp.float32), -jnp.inf)
    m = jnp.max(s, axis=-1, keepdims=True)
    m = jnp.where(jnp.isfinite(m), m, 0.0)
    e = jnp.where(mask, jnp.exp(s - m), 0.0)
    return e / jnp.maximum(jnp.sum(e, axis=-1, keepdims=True), 1e-30)


def compress(rows, pe, w1, b1, w2, b2):
    B, L, G, hd = rows.shape
    xw = jnp.transpose(rows, (0, 2, 1, 3)).reshape(B * G, L, hd)
    k1 = w1.reshape(CMP_BLOCK, hd, CMP_HIDDEN)
    hcv = lax.conv_general_dilated(xw, k1, (CMP_STRIDE,), 'VALID', dimension_numbers=('NWC', 'WIO', 'NWC'))
    hcv = jax.nn.gelu(hcv + (pe.reshape(-1) @ w1 + b1))
    out = hcv @ w2 + b2
    nb = out.shape[1]
    return jnp.transpose(out.reshape(B, G, nb, hd), (0, 2, 1, 3))


def nsa_mixer(q, gate_logits, kv_full, kw_pad, q_pos0, lp):
    B, Tq = q.shape[0], q.shape[1]
    L = kv_full.shape[1]
    G, R = NSA_KV_HEADS, NSA_REP
    qn = (rmsnorm(q, lp['q_norm']) * (HEAD_DIM ** -0.5)).reshape(B, Tq, G, R, HEAD_DIM)
    kc = rmsnorm(compress(kv_full[:, :, 0], lp['cmp_pe'][0], lp['cmp_w1'][0], lp['cmp_b1'][0],
                          lp['cmp_w2'][0], lp['cmp_b2'][0]), lp['k_norm'][0])
    vc = compress(kv_full[:, :, 1], lp['cmp_pe'][1], lp['cmp_w1'][1], lp['cmp_b1'][1],
                  lp['cmp_w2'][1], lp['cmp_b2'][1])
    nb = kc.shape[1]
    c_start = jnp.arange(nb) * CMP_STRIDE
    c_end = c_start + CMP_BLOCK - 1
    ns = -(-L // SEL_BLOCK)
    n_sel = min(SEL_TOPN, ns)
    s_start = jnp.arange(ns) * SEL_BLOCK
    overlap = ((c_start[:, None] < s_start[None] + SEL_BLOCK)
               & (c_start[:, None] + CMP_BLOCK > s_start[None])).astype(jnp.float32)
    pad = ns * SEL_BLOCK - L
    ks = jnp.pad(rmsnorm(kv_full[:, :, 2], lp['k_norm'][1]), ((0, 0), (0, pad), (0, 0), (0, 0)))
    vs = jnp.pad(kv_full[:, :, 3], ((0, 0), (0, pad), (0, 0), (0, 0)))
    kw = rmsnorm(kw_pad[:, :, 0], lp['k_norm'][2])
    vw = kw_pad[:, :, 1]
    gates = jax.nn.sigmoid(gate_logits.astype(jnp.float32)).astype(q.dtype).reshape(B, Tq, G, R, 3)
    chunk = Q_BLOCK if Tq % Q_BLOCK == 0 else Tq
    bi = jnp.arange(B)[:, None, None, None]
    gi = jnp.arange(G)[None, None, :, None]
    jsel = jnp.arange(ns)

    def attend_chunk(c):
        qs = c * chunk
        q_c = lax.dynamic_slice_in_dim(qn, qs, chunk, 1)
        pos = q_pos0 + qs + jnp.arange(chunk)
        s_c = jnp.einsum('bcgrd,bngd->bcgrn', q_c, kc)
        m_c = (c_end[None] <= pos[:, None])[None, :, None, None, :]
        p_c = masked_softmax(s_c, m_c)
        o_c = jnp.einsum('bcgrn,bngd->bcgrd', p_c.astype(vc.dtype), vc)
        imp = jnp.einsum('bcgn,ns->bcgs', jnp.sum(p_c, axis=3), overlap)
        cur = pos // SEL_BLOCK
        forced = (jsel[None] == 0) | (jsel[None] == cur[:, None]) | (jsel[None] == cur[:, None] - 1)
        causal = s_start[None] <= pos[:, None]
        score = jnp.where(forced[None, :, None], BIG, jnp.where(causal[None, :, None], imp, -BIG))
        _, idx = lax.top_k(score, n_sel)
        kpos = (idx[..., None] * SEL_BLOCK + jnp.arange(SEL_BLOCK)).reshape(B, chunk, G, n_sel * SEL_BLOCK)
        k_g = ks[bi, kpos, gi]
        v_g = vs[bi, kpos, gi]
        s_s = jnp.einsum('bcgrd,bcgkd->bcgrk', q_c, k_g)
        m_s = (kpos <= pos[None, :, None, None])[:, :, :, None, :]
        p_s = masked_softmax(s_s, m_s)
        o_s = jnp.einsum('bcgrk,bcgkd->bcgrd', p_s.astype(v_g.dtype), v_g)
        k_w = lax.dynamic_slice_in_dim(kw, qs, chunk + WINDOW, 1)
        v_w = lax.dynamic_slice_in_dim(vw, qs, chunk + WINDOW, 1)
        wpos = q_pos0 - WINDOW + qs + jnp.arange(chunk + WINDOW)
        dist = pos[:, None] - wpos[None]
        m_w = ((dist >= 0) & (dist <= WINDOW) & (wpos[None] >= 0))[None, :, None, None, :]
        s_w = jnp.einsum('bcgrd,bkgd->bcgrk', q_c, k_w)
        p_w = masked_softmax(s_w, m_w)
        o_w = jnp.einsum('bcgrk,bkgd->bcgrd', p_w.astype(v_w.dtype), v_w)
        g_c = lax.dynamic_slice_in_dim(gates, qs, chunk, 1)
        return g_c[..., 0:1] * o_c + g_c[..., 1:2] * o_s + g_c[..., 2:3] * o_w

    out = lax.map(attend_chunk, jnp.arange(Tq // chunk))
    return jnp.moveaxis(out, 0, 1).reshape(B, Tq, NSA_WIDTH)


def rwkv_mixer(xr, shift0, wkv0, lp):
    B, T = xr.shape[0], xr.shape[1]
    H, N, RW = RWKV_HEADS, HEAD_DIM, RWKV_WIDTH
    prev = jnp.concatenate([shift0[:, None].astype(xr.dtype), xr[:, :-1]], axis=1)
    xs = xr + (prev - xr) * lp['shift_mu']
    r, k, v, wd, ad, gd = jnp.split(xs, [RW, 2 * RW, 3 * RW, 3 * RW + DECAY_RANK, 3 * RW + DECAY_RANK + A_RANK], axis=-1)
    w_log = -jax.nn.softplus(-(lp['w0'] + jnp.tanh(wd) @ lp['w2'])) - 0.5
    decay = jnp.exp(-jnp.exp(w_log.astype(jnp.float32)))
    a = jax.nn.sigmoid(lp['a0'] + ad @ lp['a2'])
    gate = jax.nn.sigmoid(gd) @ lp['g2']
    heads = lambda t: t.reshape(B, T, H, N)
    kk = heads(k * lp['k_k']).astype(jnp.float32)
    kk = kk / jnp.maximum(jnp.sqrt(jnp.sum(kk * kk, axis=-1, keepdims=True)), 1e-12)
    k = k * (1 + (a - 1) * lp['k_a'])
    r_h, k_h, v_h, a_h = heads(r), heads(k), heads(v), heads(a)
    tmaj = lambda t: jnp.moveaxis(t.astype(jnp.float32), 1, 0)

    def step(S, inp):
        r_t, w_t, k_t, v_t, a_t, b_t = inp
        sa = jnp.einsum('bhij,bhj->bhi', S, a_t)
        S = S * w_t[:, :, None, :] + sa[..., None] * b_t[:, :, None, :] + v_t[..., None] * k_t[:, :, None, :]
        return S, jnp.einsum('bhij,bhj->bhi', S, r_t)

    S, y = lax.scan(step, wkv0.astype(jnp.float32),
                    (tmaj(r_h), tmaj(heads(decay)), tmaj(k_h), tmaj(v_h), tmaj(-kk), tmaj(kk * a_h)))
    y = jnp.moveaxis(y, 0, 1)
    mu = jnp.mean(y, axis=-1, keepdims=True)
    var = jnp.mean(jnp.square(y - mu), axis=-1, keepdims=True)
    y = ((y - mu) * lax.rsqrt(var + GN_EPS)).reshape(B, T, RW) * lp['ln_w'] + lp['ln_b']
    bonus = jnp.sum(r_h * k_h * lp['r_k'], axis=-1, keepdims=True) * v_h
    y = (y + bonus.reshape(B, T, RW).astype(jnp.float32)) * gate
    return y.astype(xr.dtype), S, xr[:, -1]


def trunk_layer(x, pe, lp, past_kv, win_buf, wkv0, shift0, n_keep_win):
    B, T = x.shape[0], x.shape[1]
    G = NSA_KV_HEADS
    q_pos0 = past_kv.shape[1]
    h = rmsnorm(x, lp['norm_mix'])
    z = h @ lp['w_in']
    c1 = Q_COLS
    c2 = c1 + KV_COLS
    c3 = c2 + WIN_COLS
    c4 = c3 + GATE_COLS
    q, kv_new, kw_new, gl, xr = jnp.split(z, [c1, c2, c3, c4], axis=-1)
    q = q.reshape(B, T, NSA_HEADS, HEAD_DIM)
    kv_new = kv_new.reshape(B, T, 4, G, HEAD_DIM)
    kw_new = kw_new.reshape(B, T, 2, G, HEAD_DIM)
    gl = gl.reshape(B, T, NSA_HEADS, 3)
    kv_full = jnp.concatenate([past_kv.astype(kv_new.dtype), kv_new], axis=1)
    w_rows = jnp.concatenate([win_buf.astype(kw_new.dtype), kw_new], axis=1)
    kw_pad = jnp.pad(w_rows, ((0, 0), (WINDOW - win_buf.shape[1], 0), (0, 0), (0, 0), (0, 0)))
    o_nsa = nsa_mixer(q, gl, kv_full, kw_pad, q_pos0, lp)
    o_rwkv, wkv, shift = rwkv_mixer(xr, shift0, wkv0, lp)
    x = x + jnp.concatenate([o_nsa, o_rwkv], axis=-1) @ lp['w_out']
    hf = rmsnorm(x, lp['norm_ffn'])
    x = x + (jax.nn.silu(hf @ lp['w_gate']) * (hf @ lp['w_up'])) @ lp['w_down']
    pg = jax.nn.sigmoid(rmsnorm(x, lp['norm_ple']) @ lp['w_ple_gate'])
    x = x + pg * (pe @ lp['w_ple'])
    new_win = w_rows[:, w_rows.shape[1] - n_keep_win:]
    return x, kv_new, new_win, wkv, shift


def setup_inputs(seed: int = 0) -> dict:
    key = jax.random.key(seed)
    keys = iter(jax.random.split(key, 64))
    nrm = lambda shape, scale: jax.random.normal(next(keys), shape, jnp.float32) * scale
    gain = lambda shape: 1.0 + nrm(shape, 0.1)
    n_pages = PAST_LEN // PAGE_SIZE
    n_phys = (DEC_BATCH * n_pages * 5) // 4
    w_buf = min(WINDOW, PAST_LEN)
    G, RW = NSA_KV_HEADS, RWKV_WIDTH
    page_table = jax.random.permutation(next(keys), n_phys)[:DEC_BATCH * n_pages]
    page_table = page_table.reshape(DEC_BATCH, n_pages).astype(jnp.int32)
    return {
        'x_prompt': nrm((BATCH, SEQ, D_MODEL), 1.0),
        'x_sample': nrm((DEC_BATCH, DEC_SEQ, D_MODEL), 1.0),
        'p_prompt': nrm((DEPTH, BATCH, SEQ, PLE_DIM), 1.0),
        'p_sample': nrm((DEPTH, DEC_BATCH, DEC_SEQ, PLE_DIM), 1.0),
        'cache_kv': nrm((DEPTH, n_phys, PAGE_SIZE, 4, G, HEAD_DIM), 1.0),
        'cache_win': nrm((DEPTH, DEC_BATCH, w_buf, 2, G, HEAD_DIM), 1.0),
        'state_wkv': nrm((DEPTH, DEC_BATCH, RWKV_HEADS, HEAD_DIM, HEAD_DIM), 0.3),
        'state_shift': nrm((DEPTH, DEC_BATCH, RWKV_PROJ), 1.0),
        'page_table': page_table,
        'norm_mix': gain((DEPTH, D_MODEL)),
        'w_in': nrm((DEPTH, D_MODEL, IN_COLS), D_MODEL ** -0.5),
        'q_norm': gain((DEPTH, HEAD_DIM)),
        'k_norm': gain((DEPTH, 3, HEAD_DIM)),
        'cmp_pe': nrm((DEPTH, 2, CMP_BLOCK, HEAD_DIM), 0.1),
        'cmp_w1': nrm((DEPTH, 2, CMP_BLOCK * HEAD_DIM, CMP_HIDDEN), (CMP_BLOCK * HEAD_DIM) ** -0.5),
        'cmp_b1': nrm((DEPTH, 2, CMP_HIDDEN), 0.01),
        'cmp_w2': nrm((DEPTH, 2, CMP_HIDDEN, HEAD_DIM), CMP_HIDDEN ** -0.5),
        'cmp_b2': nrm((DEPTH, 2, HEAD_DIM), 0.01),
        'shift_mu': jax.random.uniform(next(keys), (DEPTH, RWKV_PROJ), jnp.float32),
        'w0': jax.random.uniform(next(keys), (DEPTH, RW), jnp.float32, -6.0, 0.0),
        'w2': nrm((DEPTH, DECAY_RANK, RW), 0.1),
        'a0': nrm((DEPTH, RW), 0.1),
        'a2': nrm((DEPTH, A_RANK, RW), 0.1),
        'g2': nrm((DEPTH, GATE_RANK, RW), GATE_RANK ** -0.5),
        'k_k': 0.85 + nrm((DEPTH, RW), 0.05),
        'k_a': 1.0 + nrm((DEPTH, RW), 0.05),
        'r_k': nrm((DEPTH, RWKV_HEADS, HEAD_DIM), 0.1),
        'ln_w': gain((DEPTH, RW)),
        'ln_b': nrm((DEPTH, RW), 0.01),
        'w_out': nrm((DEPTH, D_MODEL, D_MODEL), D_MODEL ** -0.5),
        'norm_ffn': gain((DEPTH, D_MODEL)),
        'w_gate': nrm((DEPTH, D_MODEL, FFN_HIDDEN), D_MODEL ** -0.5),
        'w_up': nrm((DEPTH, D_MODEL, FFN_HIDDEN), D_MODEL ** -0.5),
        'w_down': nrm((DEPTH, FFN_HIDDEN, D_MODEL), FFN_HIDDEN ** -0.5),
        'norm_ple': gain((DEPTH, D_MODEL)),
        'w_ple_gate': nrm((DEPTH, D_MODEL, D_MODEL), D_MODEL ** -0.5),
        'w_ple': nrm((DEPTH, PLE_DIM, D_MODEL), PLE_DIM ** -0.5),
    }


def reference(x_prompt, x_sample, p_prompt, p_sample, cache_kv, cache_win, state_wkv, state_shift,
              page_table, norm_mix, w_in, q_norm, k_norm, cmp_pe, cmp_w1, cmp_b1, cmp_w2, cmp_b2,
              shift_mu, w0, w2, a0, a2, g2, k_k, k_a, r_k, ln_w, ln_b, w_out, norm_ffn, w_gate,
              w_up, w_down, norm_ple, w_ple_gate, w_ple):
    B, T = x_prompt.shape[0], x_prompt.shape[1]
    DB = x_sample.shape[0]
    G = NSA_KV_HEADS
    past_len = page_table.shape[1] * cache_kv.shape[2]
    dt = x_prompt.dtype
    empty_kv = jnp.zeros((B, 0, 4, G, HEAD_DIM), dt)
    empty_win = jnp.zeros((B, 0, 2, G, HEAD_DIM), dt)
    zero_wkv = jnp.zeros((B, RWKV_HEADS, HEAD_DIM, HEAD_DIM), jnp.float32)
    zero_shift = jnp.zeros((B, RWKV_PROJ), dt)
    xp, xs = x_prompt, x_sample
    kv_p, kv_s, win_p, win_s, wkv_p, wkv_s, sh_p, sh_s = [], [], [], [], [], [], [], []
    for i in range(DEPTH):
        lp = dict(norm_mix=norm_mix[i], w_in=w_in[i], q_norm=q_norm[i], k_norm=k_norm[i],
                  cmp_pe=cmp_pe[i], cmp_w1=cmp_w1[i], cmp_b1=cmp_b1[i], cmp_w2=cmp_w2[i], cmp_b2=cmp_b2[i],
                  shift_mu=shift_mu[i], w0=w0[i], w2=w2[i], a0=a0[i], a2=a2[i], g2=g2[i],
                  k_k=k_k[i], k_a=k_a[i], r_k=r_k[i], ln_w=ln_w[i], ln_b=ln_b[i], w_out=w_out[i],
                  norm_ffn=norm_ffn[i], w_gate=w_gate[i], w_up=w_up[i], w_down=w_down[i],
                  norm_ple=norm_ple[i], w_ple_gate=w_ple_gate[i], w_ple=w_ple[i])
        xp, kv1, win1, wkv1, sh1 = trunk_layer(xp, p_prompt[i], lp, empty_kv, empty_win, zero_wkv,
                                              zero_shift, min(WINDOW, T))
        past = cache_kv[i][page_table].reshape(DB, past_len, 4, G, HEAD_DIM)
        xs, kv2, win2, wkv2, sh2 = trunk_layer(xs, p_sample[i], lp, past, cache_win[i], state_wkv[i],
                                              state_shift[i], cache_win.shape[2])
        kv_p.append(kv1); kv_s.append(kv2); win_p.append(win1); win_s.append(win2)
        wkv_p.append(wkv1); wkv_s.append(wkv2); sh_p.append(sh1); sh_s.append(sh2)
    return (xp, xs, jnp.stack(kv_p), jnp.stack(kv_s), jnp.stack(win_p), jnp.stack(win_s),
            jnp.stack(wkv_p), jnp.stack(wkv_s), jnp.stack(sh_p), jnp.stack(sh_s))
```

```python
import functools

import jax
import jax.numpy as jnp
from jax import lax
from jax.experimental import pallas as pl
from jax.experimental.pallas import tpu as pltpu

F32 = jnp.float32
BF16 = jnp.bfloat16
HI = lax.Precision.HIGHEST

D_MODEL = 1024
HEAD_DIM = 64
LANES = 128
NSA_HEADS = 8
NSA_KV_HEADS = 2
RWKV_HEADS = 8
RWKV_WIDTH = RWKV_HEADS * HEAD_DIM
CMP_BLOCK = 32
CMP_STRIDE = 16
CMP_HIDDEN = 256
SEL_BLOCK = 64
SEL_TOPN = 16
WINDOW = 512
Q_BLOCK = 128
DECAY_RANK = 64
A_RANK = 64
GATE_RANK = 128
Q_COLS = NSA_HEADS * HEAD_DIM
KV_COLS = 4 * NSA_KV_HEADS * HEAD_DIM
WIN_COLS = 2 * NSA_KV_HEADS * HEAD_DIM
GATE_COLS = 3 * NSA_HEADS
RWKV_PROJ = 3 * RWKV_WIDTH + DECAY_RANK + A_RANK + GATE_RANK
XR_PAD = 3 * RWKV_WIDTH + 3 * LANES
IN_PAD = Q_COLS + KV_COLS + WIN_COLS + LANES + XR_PAD
NORM_EPS = 1e-6
GN_EPS = 64e-5
BIG = 1e9
NEG = -1e30
RWKV_CHUNK = 64
VMEM_LIMIT = 56 * 1024 * 1024


def _dot(a, b, prec=None):
    return lax.dot_general(a, b, (((1,), (0,)), ((), ())), precision=prec, preferred_element_type=F32)


def _dot_nt(a, b, prec=None):
    return lax.dot_general(a, b, (((1,), (1,)), ((), ())), precision=prec, preferred_element_type=F32)


def _dot_tn(a, b, prec=None):
    return lax.dot_general(a, b, (((0,), (0,)), ((), ())), precision=prec, preferred_element_type=F32)


def _iota(shape, dim):
    return lax.broadcasted_iota(jnp.int32, shape, dim)


def _lo_mask(shape):
    return (_iota(shape, len(shape) - 1) % LANES) < HEAD_DIM


def _seg_sum(x):
    lo = _lo_mask(x.shape)
    s_lo = jnp.sum(jnp.where(lo, x, 0.0), axis=-1, keepdims=True)
    s_hi = jnp.sum(jnp.where(lo, 0.0, x), axis=-1, keepdims=True)
    return jnp.where(lo, s_lo, s_hi)


def _seg_rmsnorm(x, gain):
    ms = _seg_sum(x * x) * (1.0 / HEAD_DIM)
    return x * lax.rsqrt(ms + NORM_EPS) * gain


def _rmsnorm_rows(x, gain):
    return x * lax.rsqrt(jnp.mean(x * x, axis=-1, keepdims=True) + NORM_EPS) * gain


def _sigmoid(x):
    return 1.0 / (1.0 + jnp.exp(-x))


def _masked_softmax(s, mask):
    sm = jnp.where(mask, s, NEG)
    m = jnp.max(sm, axis=-1, keepdims=True)
    m = jnp.where(m > 0.5 * NEG, m, 0.0)
    e = jnp.where(mask, jnp.exp(sm - m), 0.0)
    return e / jnp.maximum(jnp.sum(e, axis=-1, keepdims=True), 1e-30)


def _params(sem):
    return pltpu.CompilerParams(dimension_semantics=sem, vmem_limit_bytes=VMEM_LIMIT)


def _const_spec(shape):
    nd = len(shape)
    return pl.BlockSpec(shape, lambda *_: (0,) * nd)


_IN_SPLITS = (("q", 0, Q_COLS), ("kv", Q_COLS, KV_COLS), ("kw", Q_COLS + KV_COLS, WIN_COLS),
              ("gl", Q_COLS + KV_COLS + WIN_COLS, LANES),
              ("xr", Q_COLS + KV_COLS + WIN_COLS + LANES, XR_PAD))


def _in_proj_kernel(x_ref, g_ref, w_ref, *out_refs):
    h = _rmsnorm_rows(x_ref[...], g_ref[...]).astype(BF16)
    for o_ref, (_, lo, n) in zip(out_refs, _IN_SPLITS):
        o_ref[...] = _dot(h, w_ref[:, lo:lo + n])


def _in_proj(x, gain, w_pad):
    m = x.shape[0]
    tm = min(m, 512)
    assert m % tm == 0
    return pl.pallas_call(
        _in_proj_kernel,
        grid=(m // tm,),
        in_specs=[pl.BlockSpec((tm, D_MODEL), lambda i: (i, 0)),
                  _const_spec((1, D_MODEL)),
                  _const_spec((D_MODEL, IN_PAD))],
        out_specs=[pl.BlockSpec((tm, n), lambda i: (i, 0)) for _, _, n in _IN_SPLITS],
        out_shape=[jax.ShapeDtypeStruct((m, n), F32) for _, _, n in _IN_SPLITS],
        compiler_params=_params(("parallel",)),
        name="in_proj",
    )(x, gain, w_pad)


def _pad_xr(v):
    z = jnp.zeros(v.shape[:-1] + (LANES - DECAY_RANK,), v.dtype)
    c = 3 * RWKV_WIDTH
    return jnp.concatenate([v[..., :c + DECAY_RANK], z, v[..., c + DECAY_RANK:c + DECAY_RANK + A_RANK], z,
                            v[..., c + DECAY_RANK + A_RANK:]], axis=-1)


def _unpad_xr(v):
    c = 3 * RWKV_WIDTH
    return jnp.concatenate([v[..., :c + DECAY_RANK], v[..., c + LANES:c + LANES + A_RANK], v[..., c + 2 * LANES:]],
                           axis=-1)


def _pad_w_in(w_in):
    c = Q_COLS + KV_COLS + WIN_COLS
    gl = jnp.pad(w_in[:, c:c + GATE_COLS], ((0, 0), (0, LANES - GATE_COLS)))
    return jnp.concatenate([w_in[:, :c], gl, _pad_xr(w_in[:, c + GATE_COLS:])], axis=-1).astype(BF16)


def _ffn_kernel(x_ref, on_ref, or_ref, pe_ref, wout_ref, nf_ref, wg_ref, wu_ref, wd_ref, np_ref, wpg_ref,
                wple_ref, y_ref, x1_s, hf_s, acc_s):
    j = pl.program_id(1)

    @pl.when(j == 0)
    def _():
        half = D_MODEL // 2
        o = _dot(on_ref[...].astype(BF16), wout_ref[0:half, :]) + _dot(or_ref[...].astype(BF16), wout_ref[half:, :])
        x1 = x_ref[...] + o
        x1_s[...] = x1
        hf_s[...] = _rmsnorm_rows(x1, nf_ref[...]).astype(BF16)
        acc_s[...] = jnp.zeros_like(acc_s)

    hf = hf_s[...]
    g = _dot(hf, wg_ref[...])
    u = _dot(hf, wu_ref[...])
    act = (g * _sigmoid(g)) * u
    acc_s[...] += _dot(act.astype(BF16), wd_ref[...])

    @pl.when(j == pl.num_programs(1) - 1)
    def _():
        x2 = x1_s[...] + acc_s[...]
        hp = _rmsnorm_rows(x2, np_ref[...]).astype(BF16)
        pg = _sigmoid(_dot(hp, wpg_ref[...]))
        y_ref[...] = x2 + pg * _dot(pe_ref[...].astype(BF16), wple_ref[...])


def _out_ffn(x, o_nsa, o_rwkv, pe, lw):
    m = x.shape[0]
    tm = min(m, 512)
    hid = lw["w_gate"].shape[1]
    th = 256
    assert m % tm == 0 and hid % th == 0
    ple = pe.shape[1]
    row = lambda i, j: (i, 0)
    return pl.pallas_call(
        _ffn_kernel,
        grid=(m // tm, hid // th),
        in_specs=[pl.BlockSpec((tm, D_MODEL), row),
                  pl.BlockSpec((tm, D_MODEL // 2), row),
                  pl.BlockSpec((tm, D_MODEL // 2), row),
                  pl.BlockSpec((tm, ple), row),
                  _const_spec((D_MODEL, D_MODEL)),
                  _const_spec((1, D_MODEL)),
                  pl.BlockSpec((D_MODEL, th), lambda i, j: (0, j)),
                  pl.BlockSpec((D_MODEL, th), lambda i, j: (0, j)),
                  pl.BlockSpec((th, D_MODEL), lambda i, j: (j, 0)),
                  _const_spec((1, D_MODEL)),
                  _const_spec((D_MODEL, D_MODEL)),
                  _const_spec((ple, D_MODEL))],
        out_specs=pl.BlockSpec((tm, D_MODEL), row),
        out_shape=jax.ShapeDtypeStruct((m, D_MODEL), F32),
        scratch_shapes=[pltpu.VMEM((tm, D_MODEL), F32), pltpu.VMEM((tm, D_MODEL), BF16),
                        pltpu.VMEM((tm, D_MODEL), F32)],
        compiler_params=_params(("parallel", "arbitrary")),
        name="out_ffn_ple",
    )(x, o_nsa, o_rwkv, pe, lw["w_out"], lw["norm_ffn"], lw["w_gate"], lw["w_up"], lw["w_down"],
      lw["norm_ple"], lw["w_ple_gate"], lw["w_ple"])


def _gelu_tanh(x):
    return 0.5 * x * (1.0 + jnp.tanh(0.7978845608028654 * (x + 0.044715 * x * x * x)))


def _compress_body(load_rows, nbp, first_step, pe_ref, wbd_ref, b1_ref, w2lo_ref, w2hi_ref, b2_ref, kn_ref,
                   kcn_ref, kcr_ref, vcn_ref, vcr_ref, bias_s):
    hid = CMP_HIDDEN

    @pl.when(first_step)
    def _():
        for ty in range(2):
            acc = jnp.zeros((1, hid), F32)
            for r in range(CMP_STRIDE):
                t = _dot(pe_ref[ty, r:r + 1, :].astype(BF16), wbd_ref[ty, r])
                acc = acc + t[:, 0:hid] + t[:, 3 * hid:4 * hid]
            bias_s[ty:ty + 1, :] = acc + b1_ref[ty:ty + 1, :]

    acc = [jnp.zeros((nbp, 4 * hid), F32) for _ in range(2)]
    for r in range(CMP_STRIDE):
        for ty in range(2):
            acc[ty] = acc[ty] + _dot(load_rows(r, ty).astype(BF16), wbd_ref[ty, r])
    outs = ((kcn_ref, kcr_ref), (vcn_ref, vcr_ref))
    for ty in range(2):
        hs = []
        for g in range(2):
            a = acc[ty][:, 2 * hid * g:2 * hid * g + hid]
            b = acc[ty][:, 2 * hid * g + hid:2 * hid * (g + 1)]
            b_next = pltpu.roll(b, nbp - 1, 0)
            hs.append(_gelu_tanh(a + b_next + bias_s[ty:ty + 1, :]).astype(BF16))
        o_n = _dot(hs[0], w2lo_ref[ty]) + _dot(hs[1], w2hi_ref[ty]) + b2_ref[ty:ty + 1, :]
        o_r = _dot(hs[0], w2hi_ref[ty]) + _dot(hs[1], w2lo_ref[ty]) + b2_ref[ty:ty + 1, :]
        if ty == 0:
            o_n = _seg_rmsnorm(o_n, kn_ref[...])
            o_r = _seg_rmsnorm(o_r, kn_ref[...])
        outs[ty][0][0] = o_n.astype(outs[ty][0].dtype)
        outs[ty][1][0] = o_r.astype(outs[ty][1].dtype)


def _compress_prompt_kernel(k_ref, v_ref, *refs, nbp):
    load = lambda r, ty: (k_ref, v_ref)[ty][0, pl.ds(r, nbp, stride=CMP_STRIDE), :]
    _compress_body(load, nbp, pl.program_id(0) == 0, *refs)


def _compress_paged_kernel(pt_ref, *refs, nbp, n_pages, page):
    del pt_ref
    page_refs, rest, rows_s = refs[:n_pages], refs[n_pages:-2], refs[-2:]
    for j in range(n_pages):
        for ty in range(2):
            rows_s[ty][j * page:(j + 1) * page, :] = page_refs[j][0, :, LANES * ty:LANES * (ty + 1)]
    load = lambda r, ty: rows_s[ty][pl.ds(r, nbp, stride=CMP_STRIDE), :]
    _compress_body(load, nbp, pl.program_id(0) == 0, *rest)


def _compress_weight_specs():
    hid = CMP_HIDDEN
    return [_const_spec((2, CMP_STRIDE, LANES)), _const_spec((2, CMP_STRIDE, LANES, 4 * hid)),
            _const_spec((2, hid)), _const_spec((2, hid, LANES)), _const_spec((2, hid, LANES)),
            _const_spec((2, LANES)), _const_spec((1, LANES))]


def _compress_outs(b, nbp):
    spec = pl.BlockSpec((1, nbp, LANES), lambda i, *_: (i, 0, 0))
    shapes = [jax.ShapeDtypeStruct((b, nbp, LANES), dt) for dt in (F32, F32, BF16, BF16)]
    return [spec] * 4, shapes


def _compress_prompt(kv, cw):
    b, t, _ = kv.shape
    nbp = t // CMP_STRIDE
    out_specs, out_shape = _compress_outs(b, nbp)
    return pl.pallas_call(
        functools.partial(_compress_prompt_kernel, nbp=nbp),
        grid=(b,),
        in_specs=[pl.BlockSpec((1, t, LANES), lambda i: (i, 0, 0)),
                  pl.BlockSpec((1, t, LANES), lambda i: (i, 0, 1))] + _compress_weight_specs(),
        out_specs=out_specs, out_shape=out_shape,
        scratch_shapes=[pltpu.VMEM((2, CMP_HIDDEN), F32)],
        compiler_params=_params(("arbitrary",)),
        name="compress_prompt",
    )(kv, kv, *cw)


def _compress_paged(cache, pt_flat, n_seq, n_pages, cw):
    page = cache.shape[1]
    nbp = n_pages * page // CMP_STRIDE
    out_specs, out_shape = _compress_outs(n_seq, nbp)
    page_specs = [pl.BlockSpec((1, page, 2 * LANES), functools.partial(
        lambda i, pt, j: (pt[i * n_pages + j], 0, 0), j=j)) for j in range(n_pages)]
    return pl.pallas_call(
        functools.partial(_compress_paged_kernel, nbp=nbp, n_pages=n_pages, page=page),
        grid_spec=pltpu.PrefetchScalarGridSpec(
            num_scalar_prefetch=1, grid=(n_seq,),
            in_specs=page_specs + _compress_weight_specs(),
            out_specs=out_specs,
            scratch_shapes=[pltpu.VMEM((2, CMP_HIDDEN), F32), pltpu.VMEM((n_pages * page, LANES), F32),
                            pltpu.VMEM((n_pages * page, LANES), F32)]),
        out_shape=out_shape,
        compiler_params=_params(("arbitrary",)),
        name="compress_paged",
    )(pt_flat, *([cache] * n_pages), *cw)


def _compress_weights(cmp_pe, cmp_w1, cmp_b1, cmp_w2, cmp_b2, k_norm0):
    hid = CMP_HIDDEN
    w1 = cmp_w1.reshape(2, CMP_BLOCK, HEAD_DIM, hid)
    wcat = jnp.concatenate([w1[:, :CMP_STRIDE], w1[:, CMP_STRIDE:]], axis=-1)
    z = jnp.zeros_like(wcat)
    wbd = jnp.concatenate([jnp.concatenate([wcat, z], -1), jnp.concatenate([z, wcat], -1)], axis=2).astype(BF16)
    pe_pair = jnp.concatenate([cmp_pe[:, :CMP_STRIDE], cmp_pe[:, CMP_STRIDE:]], axis=-1)
    z2 = jnp.zeros_like(cmp_w2)
    w2lo = jnp.concatenate([cmp_w2, z2], -1).astype(BF16)
    w2hi = jnp.concatenate([z2, cmp_w2], -1).astype(BF16)
    b2 = jnp.concatenate([cmp_b2, cmp_b2], -1)
    kn = jnp.concatenate([k_norm0, k_norm0])[None]
    return (pe_pair, wbd, cmp_b1, w2lo, w2hi, b2, kn)


def _kv_prep_kernel(kv_ref, kw_ref, kns_ref, knw_ref, ksn_ref, ksr_ref, vsn_ref, vsr_ref, kwn_ref, kwr_ref,
                    vwn_ref, vwr_ref):
    def put(x, n_ref, r_ref):
        n_ref[0] = x.astype(BF16)
        r_ref[0] = pltpu.roll(x, HEAD_DIM, 1).astype(BF16)

    put(_seg_rmsnorm(kv_ref[0, :, 2 * LANES:3 * LANES], kns_ref[...]), ksn_ref, ksr_ref)
    put(kv_ref[0, :, 3 * LANES:4 * LANES], vsn_ref, vsr_ref)
    put(_seg_rmsnorm(kw_ref[0, :, 0:LANES], knw_ref[...]), kwn_ref, kwr_ref)
    put(kw_ref[0, :, LANES:2 * LANES], vwn_ref, vwr_ref)


def _kv_prep(kv, kw, kn_sel, kn_win):
    b, t, _ = kv.shape
    tm = min(t, 512)
    spec = lambda n: pl.BlockSpec((1, tm, n), lambda i, j: (i, j, 0))
    return pl.pallas_call(
        _kv_prep_kernel,
        grid=(b, t // tm),
        in_specs=[spec(4 * LANES), spec(2 * LANES), _const_spec((1, LANES)), _const_spec((1, LANES))],
        out_specs=[spec(LANES)] * 8,
        out_shape=[jax.ShapeDtypeStruct((b, t, LANES), BF16)] * 8,
        compiler_params=_params(("parallel", "parallel")),
        name="kv_prep",
    )(kv, kw, kn_sel, kn_win)


def _overlap_t(ns, nbp):
    j = _iota((ns, nbp), 0) * SEL_BLOCK
    n = _iota((ns, nbp), 1) * CMP_STRIDE
    return jnp.where((n < j + SEL_BLOCK) & (n + CMP_BLOCK > j), 1.0, 0.0).astype(F32)


def _nsa_prompt_kernel(q_ref, gl_ref, qn_ref, kcn_ref, kcr_ref, vcn_ref, vcr_ref, ksn_ref, ksr_ref, vsn_ref,
                       vsr_ref, kwn_ref, kwr_ref, vwn_ref, vwr_ref, o_ref, *, seq, tk, wk):
    i = pl.program_id(1)
    qb = Q_BLOCK
    nbp = kcn_ref.shape[1]
    nb = nbp - 1
    ns = seq // SEL_BLOCK
    n_sel = min(SEL_TOPN, ns)
    lo = _lo_mask((qb, LANES))
    pos = i * qb + _iota((qb, 1), 0)
    sig = _sigmoid(gl_ref[0])

    qp = [_seg_rmsnorm(q_ref[0, :, LANES * p:LANES * (p + 1)], qn_ref[...]) * (HEAD_DIM ** -0.5) for p in range(4)]
    q_lo = [jnp.where(lo, x, 0.0) for x in qp]
    q_hi = [jnp.where(lo, 0.0, x) for x in qp]

    n_c = _iota((qb, nbp), 1)
    mask_c = (n_c * CMP_STRIDE + (CMP_BLOCK - 1) <= pos) & (n_c < nb)
    ov_t = _overlap_t(ns, nbp)
    j_t = _iota((ns, qb), 0)
    pos_t = i * qb + _iota((ns, qb), 1)
    cur_t = pos_t >> 6
    forced_t = (j_t == 0) | (j_t == cur_t) | (j_t == cur_t - 1)
    causal_t = j_t * SEL_BLOCK <= pos_t
    eye = jnp.where(_iota((qb, qb), 0) == _iota((qb, qb), 1), 1.0, 0.0).astype(BF16)

    w_start = pl.multiple_of(jnp.maximum(i * qb - (wk - qb), 0), qb)
    w_pos = w_start + _iota((qb, wk), 1)
    w_dist = pos - w_pos
    mask_w = (w_dist >= 0) & (w_dist <= WINDOW)
    n_tiles = (i * qb + qb - 1) // tk + 1

    for g in range(2):
        order = lambda n, r: (n, r) if g == 0 else (r, n)
        kc_a, kc_b = order(kcn_ref, kcr_ref)
        vc_a, vc_b = order(vcn_ref, vcr_ref)
        ks_a, ks_b = order(ksn_ref, ksr_ref)
        vs_a, vs_b = order(vsn_ref, vsr_ref)
        kw_a, kw_b = order(kwn_ref, kwr_ref)
        vw_a, vw_b = order(vwn_ref, vwr_ref)

        o_cmp = []
        psum = jnp.zeros((qb, nbp), F32)
        for p in (2 * g, 2 * g + 1):
            p1 = _masked_softmax(_dot_nt(q_lo[p], kc_a[0], HI), mask_c)
            p2 = _masked_softmax(_dot_nt(q_hi[p], kc_b[0], HI), mask_c)
            o_cmp.append(jnp.where(lo, _dot(p1.astype(BF16), vc_a[0]), _dot(p2.astype(BF16), vc_b[0])))
            psum = psum + p1 + p2
        imp_t = _dot_nt(ov_t, psum, HI)
        score = jnp.where(forced_t, BIG, jnp.where(causal_t, imp_t, -BIG))
        rank = jnp.zeros((ns, qb), jnp.int32)
        for jp in range(ns):
            row = score[jp:jp + 1, :]
            rank = rank + jnp.where((row > score) | ((row == score) & (j_t > jp)), 1, 0)
        sel_t = jnp.where(rank < n_sel, 1.0, 0.0).astype(BF16)
        sel = _dot_nt(eye, sel_t).astype(BF16)

        for idx, p in enumerate((2 * g, 2 * g + 1)):
            qa = q_lo[p].astype(BF16)
            qh = q_hi[p].astype(BF16)

            def tile(kt, carry, qa=qa, qh=qh):
                k0 = pl.multiple_of(kt * tk, tk)
                k_pos = k0 + _iota((qb, tk), 1)
                blk = (k0 + _iota((ns, tk), 1)) >> 6
                expand = jnp.where(_iota((ns, tk), 0) == blk, 1.0, 0.0).astype(BF16)
                valid = (_dot(sel, expand) > 0.5) & (k_pos <= pos)
                out = []
                for (m, l, acc), q_, k_ref, v_ref in ((carry[0], qa, ks_a, vs_a), (carry[1], qh, ks_b, vs_b)):
                    s = jnp.where(valid, _dot_nt(q_, k_ref[0, pl.ds(k0, tk), :]), NEG)
                    m_new = jnp.maximum(m, jnp.max(s, axis=-1, keepdims=True))
                    alpha = jnp.exp(m - m_new)
                    e = jnp.where(valid, jnp.exp(s - m_new), 0.0)
                    l_new = alpha * l + jnp.sum(e, axis=-1, keepdims=True)
                    acc_new = alpha * acc + _dot(e.astype(BF16), v_ref[0, pl.ds(k0, tk), :])
                    out.append((m_new, l_new, acc_new))
                return tuple(out)

            init = (jnp.full((qb, 1), NEG, F32), jnp.zeros((qb, 1), F32), jnp.zeros((qb, LANES), F32))
            (_, l1, a1), (_, l2, a2) = lax.fori_loop(0, n_tiles, tile, (init, init))
            o_sel = jnp.where(lo, a1 / l1, a2 / l2)

            pw1 = _masked_softmax(_dot_nt(qa, kw_a[0, pl.ds(w_start, wk), :]), mask_w)
            pw2 = _masked_softmax(_dot_nt(qh, kw_b[0, pl.ds(w_start, wk), :]), mask_w)
            o_win = jnp.where(lo, _dot(pw1.astype(BF16), vw_a[0, pl.ds(w_start, wk), :]),
                              _dot(pw2.astype(BF16), vw_b[0, pl.ds(w_start, wk), :]))

            gate = lambda c: jnp.where(lo, sig[:, 6 * p + c:6 * p + c + 1], sig[:, 6 * p + 3 + c:6 * p + 4 + c])
            o_ref[0, :, LANES * p:LANES * (p + 1)] = gate(0) * o_cmp[idx] + gate(1) * o_sel + gate(2) * o_win


def _nsa_prompt(q, gl, qn, kc, ksv, seq):
    b = q.shape[0]
    nbp = kc[0].shape[1]
    tk = min(seq, 512)
    wk = min(seq, WINDOW + Q_BLOCK)
    assert seq % Q_BLOCK == 0 and seq % tk == 0
    blk = lambda n: pl.BlockSpec((1, Q_BLOCK, n), lambda bi, i: (bi, i, 0))
    full = lambda rows: pl.BlockSpec((1, rows, LANES), lambda bi, i: (bi, 0, 0))
    return pl.pallas_call(
        functools.partial(_nsa_prompt_kernel, seq=seq, tk=tk, wk=wk),
        grid=(b, seq // Q_BLOCK),
        in_specs=[blk(4 * LANES), blk(LANES), _const_spec((1, LANES))] + [full(nbp)] * 4 + [full(seq)] * 8,
        out_specs=blk(4 * LANES),
        out_shape=jax.ShapeDtypeStruct((b, seq, 4 * LANES), F32),
        compiler_params=_params(("parallel", "arbitrary")),
        name="nsa_prompt",
    )(q, gl, qn, *kc, *ksv)


def _nsa_step_kernel(pt_ref, *refs, n_pages, page, past):
    del pt_ref
    page_refs = refs[:n_pages]
    (qm_ref, gl_ref, kcn_ref, vcn_ref, kvn_ref, cw_ref, kwn_ref, qn_ref, kns_ref, knw_ref,
     o_ref, win_ref) = refs[n_pages:]
    nh = NSA_HEADS
    nbp = kcn_ref.shape[1]
    nb = nbp - 1
    ns = past // SEL_BLOCK + 1
    n_sel = min(SEL_TOPN, ns)
    wbuf = cw_ref.shape[1]

    qn = _seg_rmsnorm(qm_ref[0], qn_ref[...]) * (HEAD_DIM ** -0.5)
    qb16 = qn.astype(BF16)
    sg = _sigmoid(gl_ref[0])

    n_c = _iota((nh, nbp), 1)
    mask_c = (n_c * CMP_STRIDE + (CMP_BLOCK - 1) <= past) & (n_c < nb)
    p_c = _masked_softmax(_dot_nt(qn, kcn_ref[0], HI), mask_c)
    o_cmp = _dot(p_c.astype(BF16), vcn_ref[0])
    rep = NSA_HEADS // NSA_KV_HEADS
    same_group = jnp.where(_iota((nh, nh), 0) // rep == _iota((nh, nh), 1) // rep, 1.0, 0.0).astype(F32)
    psum = _dot(same_group, p_c, HI)
    imp = _dot_nt(psum, _overlap_t(LANES, nbp), HI)
    j = _iota((nh, LANES), 1)
    cur = past // SEL_BLOCK
    forced = (j == 0) | (j == cur) | (j == cur - 1)
    score = jnp.where(forced, BIG, jnp.where(j * SEL_BLOCK <= past, imp, -BIG))
    score = jnp.where(j < ns, score, -4.0 * BIG)
    rank = jnp.zeros((nh, LANES), jnp.int32)
    for jp in range(ns):
        col = score[:, jp:jp + 1]
        rank = rank + jnp.where((col > score) | ((col == score) & (j > jp)), 1, 0)
    sel = jnp.where((rank < n_sel) & (j < ns), 1.0, 0.0).astype(BF16)

    kv_new = kvn_ref[0]
    k_new = _seg_rmsnorm(kv_new[:, 0:LANES], kns_ref[...])
    s_new = jnp.sum(qn * k_new, axis=-1, keepdims=True)
    s_parts, m_parts, v_parts = [], [], []
    for pj in range(n_pages):
        pg = page_refs[pj][0]
        kp = _seg_rmsnorm(pg[:, 0:LANES], kns_ref[...]).astype(BF16)
        v_parts.append(pg[:, LANES:2 * LANES].astype(BF16))
        blk = (pj * page + _iota((LANES, page), 1)) >> 6
        expand = jnp.where(_iota((LANES, page), 0) == blk, 1.0, 0.0).astype(BF16)
        m_parts.append(_dot(sel, expand) > 0.5)
        s_parts.append(_dot_nt(qb16, kp))
    s_all = jnp.concatenate(s_parts, axis=-1)
    valid = jnp.concatenate(m_parts, axis=-1)
    s_all = jnp.where(valid, s_all, NEG)
    m = jnp.maximum(jnp.max(s_all, axis=-1, keepdims=True), s_new)
    e_all = jnp.where(valid, jnp.exp(s_all - m), 0.0)
    e_new = jnp.exp(s_new - m)
    acc = e_new * kv_new[:, LANES:2 * LANES]
    for pj in range(n_pages):
        acc = acc + _dot(e_all[:, pj * page:(pj + 1) * page].astype(BF16), v_parts[pj])
    o_sel = acc / (jnp.sum(e_all, axis=-1, keepdims=True) + e_new)

    cw = cw_ref[0]
    kw_new = kwn_ref[0]
    kwn = _seg_rmsnorm(cw[:, 0:LANES], knw_ref[...]).astype(BF16)
    s_w = _dot_nt(qb16, kwn)
    s_wn = jnp.sum(qn * _seg_rmsnorm(kw_new[:, 0:LANES], knw_ref[...]), axis=-1, keepdims=True)
    m_w = jnp.maximum(jnp.max(s_w, axis=-1, keepdims=True), s_wn)
    e_w = jnp.exp(s_w - m_w)
    e_wn = jnp.exp(s_wn - m_w)
    o_win = (_dot(e_w.astype(BF16), cw[:, LANES:2 * LANES].astype(BF16)) + e_wn * kw_new[:, LANES:2 * LANES]) / (
        jnp.sum(e_w, axis=-1, keepdims=True) + e_wn)

    o = sg[:, 0:1] * o_cmp + sg[:, 1:2] * o_sel + sg[:, 2:3] * o_win
    first_group = _iota((nh, LANES), 0) < rep
    o_ref[0] = jnp.where(first_group, o, pltpu.roll(o, HEAD_DIM, 1))[:, 0:HEAD_DIM]
    last = _iota((wbuf, 2 * LANES), 0) == wbuf - 1
    win_ref[0] = jnp.where(last, kw_new, pltpu.roll(cw, wbuf - 1, 0))


def _nsa_step(cache, pt_flat, qm, glm, kcn, vcn, kv_new, cache_win, kw_new, gains, n_pages):
    db = qm.shape[0]
    page = cache.shape[1]
    past = n_pages * page
    nbp = kcn.shape[1]
    wbuf = cache_win.shape[1]
    assert wbuf == min(WINDOW, past) and page == LANES and past // SEL_BLOCK + 1 <= LANES
    per = lambda shape, last=0: pl.BlockSpec((1,) + shape, lambda i, pt: (i, 0, last))
    page_specs = [pl.BlockSpec((1, page, 2 * LANES), functools.partial(
        lambda i, pt, j: (pt[i * n_pages + j], 0, 1), j=j)) for j in range(n_pages)]
    return pl.pallas_call(
        functools.partial(_nsa_step_kernel, n_pages=n_pages, page=page, past=past),
        grid_spec=pltpu.PrefetchScalarGridSpec(
            num_scalar_prefetch=1, grid=(db,),
            in_specs=page_specs + [per((NSA_HEADS, LANES)), per((NSA_HEADS, LANES)), per((nbp, LANES)),
                                   per((nbp, LANES)), per((1, 2 * LANES), 1), per((wbuf, 2 * LANES)),
                                   per((1, 2 * LANES))] + [_const_spec((1, LANES))] * 3,
            out_specs=[per((NSA_HEADS, HEAD_DIM)), per((wbuf, 2 * LANES))]),
        out_shape=[jax.ShapeDtypeStruct((db, NSA_HEADS, HEAD_DIM), F32),
                   jax.ShapeDtypeStruct((db, wbuf, 2 * LANES), F32)],
        compiler_params=_params(("parallel",)),
        name="nsa_step",
    )(pt_flat, *([cache] * n_pages), qm, glm, kcn, vcn, kv_new, cache_win, kw_new, *gains)


def _softplus(z):
    return jnp.maximum(z, 0.0) + jnp.log(1.0 + jnp.exp(-jnp.abs(z)))


def _rwkv_features(x, prev, mu_ref, w0_ref, a0_ref, w2_ref, a2_ref, g2_ref, kk_ref, ka_ref, outs):
    r_ref, lw_ref, k_ref, v_ref, a_ref, b_ref, g_ref = outs
    rw = RWKV_WIDTH
    xs = x + (prev - x) * mu_ref[...]
    r, k, v = xs[:, 0:rw], xs[:, rw:2 * rw], xs[:, 2 * rw:3 * rw]
    wd = xs[:, 3 * rw:3 * rw + LANES]
    ad = xs[:, 3 * rw + LANES:3 * rw + 2 * LANES]
    gd = xs[:, 3 * rw + 2 * LANES:3 * rw + 3 * LANES]
    w_log = -_softplus(-(w0_ref[...] + _dot(jnp.tanh(wd).astype(BF16), w2_ref[...]))) - 0.5
    a_sig = _sigmoid(a0_ref[...] + _dot(ad.astype(BF16), a2_ref[...]))
    kk = k * kk_ref[...]
    kk2 = kk * kk
    norm2 = jnp.concatenate([_seg_sum(kk2[:, LANES * p:LANES * (p + 1)]) for p in range(rw // LANES)], axis=-1)
    kk = kk / jnp.maximum(jnp.sqrt(norm2), 1e-12)
    r_ref[...] = r.reshape(r_ref.shape)
    lw_ref[...] = (-jnp.exp(w_log)).reshape(lw_ref.shape)
    k_ref[...] = (k * (1.0 + (a_sig - 1.0) * ka_ref[...])).reshape(k_ref.shape)
    v_ref[...] = v.reshape(v_ref.shape)
    a_ref[...] = (-kk).reshape(a_ref.shape)
    b_ref[...] = (kk * a_sig).reshape(b_ref.shape)
    g_ref[...] = _dot(_sigmoid(gd).astype(BF16), g2_ref[...]).reshape(g_ref.shape)


def _rwkv_prep_seq_kernel(xr_ref, shift_ref, *refs):
    params, outs, last_s = refs[:8], refs[8:15], refs[15]
    t = pl.program_id(1)

    @pl.when(t == 0)
    def _():
        last_s[...] = shift_ref[0]

    x = xr_ref[0]
    tm = x.shape[0]
    prev = jnp.where(_iota(x.shape, 0) == 0, last_s[...], pltpu.roll(x, 1, 0))
    last_s[...] = x[tm - 1:tm, :]
    _rwkv_features(x, prev, *params, outs)


def _rwkv_prep_step_kernel(xr_ref, shift_ref, *refs):
    _rwkv_features(xr_ref[...], shift_ref[...], *refs[:8], refs[8:15])


def _rwkv_param_specs():
    one = _const_spec((1, RWKV_WIDTH))
    low = _const_spec((LANES, RWKV_WIDTH))
    return [_const_spec((1, XR_PAD)), one, one, low, low, low, one, one]


def _rwkv_prep_seq(xr, shift0, rp):
    b, t, _ = xr.shape
    tm = min(t, 512)
    out = pl.BlockSpec((1, tm, RWKV_WIDTH), lambda i, j: (i, j, 0))
    return pl.pallas_call(
        _rwkv_prep_seq_kernel,
        grid=(b, t // tm),
        in_specs=[pl.BlockSpec((1, tm, XR_PAD), lambda i, j: (i, j, 0)),
                  pl.BlockSpec((1, 1, XR_PAD), lambda i, j: (i, 0, 0))] + _rwkv_param_specs(),
        out_specs=[out] * 7,
        out_shape=[jax.ShapeDtypeStruct((b, t, RWKV_WIDTH), F32)] * 7,
        scratch_shapes=[pltpu.VMEM((1, XR_PAD), F32)],
        compiler_params=_params(("parallel", "arbitrary")),
        name="rwkv_prep_seq",
    )(xr, shift0, *rp)


def _rwkv_prep_step(xr, shift0, rp):
    m = xr.shape[0]
    return pl.pallas_call(
        _rwkv_prep_step_kernel,
        grid=(1,),
        in_specs=[_const_spec((m, XR_PAD)), _const_spec((m, XR_PAD))] + _rwkv_param_specs(),
        out_specs=[_const_spec((m, RWKV_WIDTH))] * 7,
        out_shape=[jax.ShapeDtypeStruct((m, RWKV_WIDTH), F32)] * 7,
        compiler_params=_params(("arbitrary",)),
        name="rwkv_prep_step",
    )(xr, shift0, *rp)


def _rwkv_params(shift_mu, w0, w2, a0, a2, g2, k_k, k_a):
    pad = lambda w: jnp.pad(w, ((0, LANES - w.shape[0]), (0, 0))).astype(BF16)
    return (_pad_xr(shift_mu)[None], w0[None], a0[None], pad(w2), pad(a2), g2.astype(BF16), k_k[None], k_a[None])


def _rwkv_finish(y, r, k, v, gate, rk, lnw, lnb):
    mu = _seg_sum(y) * (1.0 / HEAD_DIM)
    d = y - mu
    var = _seg_sum(d * d) * (1.0 / HEAD_DIM)
    yn = d * lax.rsqrt(var + GN_EPS) * lnw + lnb
    return (yn + _seg_sum(r * k * rk) * v) * gate


def _rwkv_scan_kernel(r_ref, lw_ref, k_ref, v_ref, a_ref, b_ref, g_ref, rk_ref, lnw_ref, lnb_ref,
                      o_ref, s_out_ref, s_s):
    c = pl.program_id(1)
    ch = r_ref.shape[1]

    @pl.when(c == 0)
    def _():
        s_s[...] = jnp.zeros_like(s_s)

    row = _iota((ch, ch), 0)
    col = _iota((ch, ch), 1)
    tril = jnp.where(row >= col, 1.0, 0.0).astype(F32)
    strict = row > col
    incl = row >= col
    eye = jnp.where(row == col, 1.0, 0.0).astype(F32)
    lo = _lo_mask((ch, LANES))
    blockdiag = (_iota((LANES, LANES), 0) < HEAD_DIM) == (_iota((LANES, LANES), 1) < HEAD_DIM)

    for p in range(RWKV_WIDTH // LANES):
        sl = slice(LANES * p, LANES * (p + 1))
        r, lw, k, v, a, b = (ref[0, :, sl] for ref in (r_ref, lw_ref, k_ref, v_ref, a_ref, b_ref))
        s0 = s_s[p]
        cl = _dot(tril, lw, HI)
        g_in = jnp.exp(cl)
        g_inv = jnp.exp(-cl)
        at = a * jnp.exp(cl - lw)
        rt = r * g_in
        bt = b * g_inv
        kt = k * g_inv
        x0 = _dot_nt(at, s0, HI)
        y0 = _dot_nt(rt, s0, HI)
        u_h, y_h = [], []
        for m in (lo, jnp.logical_not(lo)):
            at_m = jnp.where(m, at, 0.0)
            rt_m = jnp.where(m, rt, 0.0)
            n = jnp.where(strict, _dot_nt(at_m, bt, HI), 0.0)
            l_ak = jnp.where(strict, _dot_nt(at_m, kt, HI), 0.0)
            m_rb = jnp.where(incl, _dot_nt(rt_m, bt, HI), 0.0)
            m_rk = jnp.where(incl, _dot_nt(rt_m, kt, HI), 0.0)
            t_inv = eye + n
            pw = n
            for _ in range(max(ch.bit_length() - 2, 0)):
                pw = _dot(pw, pw, HI)
                t_inv = t_inv + _dot(t_inv, pw, HI)
            u = _dot(t_inv, x0 + _dot(l_ak, v, HI), HI)
            u_h.append(u)
            y_h.append((m_rb, m_rk))
        u = jnp.where(lo, u_h[0], u_h[1])
        y = y0 + jnp.where(lo, _dot(y_h[0][0], u, HI) + _dot(y_h[0][1], v, HI),
                           _dot(y_h[1][0], u, HI) + _dot(y_h[1][1], v, HI))
        upd = _dot_tn(jnp.concatenate([u, v], axis=0), jnp.concatenate([bt, kt], axis=0), HI)
        s_new = (s0 + jnp.where(blockdiag, upd, 0.0)) * g_in[ch - 1:ch, :]
        s_s[p] = s_new
        o_ref[0, :, sl] = _rwkv_finish(y, r, k, v, g_ref[0, :, sl], rk_ref[:, sl], lnw_ref[:, sl], lnb_ref[:, sl])

    @pl.when(c == pl.num_programs(1) - 1)
    def _():
        s_out_ref[0] = s_s[...]


def _rwkv_scan(feats, rk, lnw, lnb):
    b, t, _ = feats[0].shape
    ch = min(t, RWKV_CHUNK)
    assert t % ch == 0 and ch & (ch - 1) == 0
    blk = pl.BlockSpec((1, ch, RWKV_WIDTH), lambda i, c: (i, c, 0))
    one = _const_spec((1, RWKV_WIDTH))
    n_pairs = RWKV_WIDTH // LANES
    return pl.pallas_call(
        _rwkv_scan_kernel,
        grid=(b, t // ch),
        in_specs=[blk] * 7 + [one] * 3,
        out_specs=[blk, pl.BlockSpec((1, n_pairs, LANES, LANES), lambda i, c: (i, 0, 0, 0))],
        out_shape=[jax.ShapeDtypeStruct((b, t, RWKV_WIDTH), F32),
                   jax.ShapeDtypeStruct((b, n_pairs, LANES, LANES), F32)],
        scratch_shapes=[pltpu.VMEM((n_pairs, LANES, LANES), F32)],
        compiler_params=_params(("parallel", "arbitrary")),
        name="rwkv_scan",
    )(*feats, rk, lnw, lnb)


def _unblock_states(s_bd):
    hd = HEAD_DIM
    return jnp.stack([s_bd[:, :, :hd, :hd], s_bd[:, :, hd:, hd:]], axis=2).reshape(s_bd.shape[0], -1, hd, hd)


def _rwkv_step_kernel(s_ref, r_ref, lw_ref, k_ref, v_ref, a_ref, b_ref, g_ref, rk_ref, lnw_ref, lnb_ref,
                      o_ref, s_out_ref):
    hd = HEAD_DIM
    eye = _iota((hd, hd), 0) == _iota((hd, hd), 1)
    for h in range(RWKV_HEADS):
        s0 = s_ref[0, h]
        r, lw, k, v, a, b = (ref[0, h] for ref in (r_ref, lw_ref, k_ref, v_ref, a_ref, b_ref))
        v_col = jnp.sum(jnp.where(eye, v, 0.0), axis=-1, keepdims=True)
        sa = jnp.sum(s0 * a, axis=-1, keepdims=True)
        s1 = s0 * jnp.exp(lw) + sa * b + v_col * k
        s_out_ref[0, h] = s1
        y_col = jnp.sum(s1 * r, axis=-1, keepdims=True)
        y = jnp.sum(jnp.where(eye, y_col, 0.0), axis=0, keepdims=True)
        mu = jnp.mean(y, axis=-1, keepdims=True)
        d = y - mu
        var = jnp.mean(d * d, axis=-1, keepdims=True)
        yn = d * lax.rsqrt(var + GN_EPS) * lnw_ref[h] + lnb_ref[h]
        bonus = jnp.sum(r * k * rk_ref[h], axis=-1, keepdims=True) * v
        o_ref[0, h] = (yn + bonus) * g_ref[0, h]


def _rwkv_step(state, feats, rk, lnw, lnb):
    m = state.shape[0]
    hd, nh = HEAD_DIM, RWKV_HEADS
    heads = lambda x: x.reshape(x.shape[:-1] + (nh, 1, hd))
    vec = pl.BlockSpec((1, nh, 1, hd), lambda i: (i, 0, 0, 0))
    st = pl.BlockSpec((1, nh, hd, hd), lambda i: (i, 0, 0, 0))
    par = _const_spec((nh, 1, hd))
    o, s1 = pl.pallas_call(
        _rwkv_step_kernel,
        grid=(m,),
        in_specs=[st] + [vec] * 7 + [par] * 3,
        out_specs=[vec, st],
        out_shape=[jax.ShapeDtypeStruct((m, nh, 1, hd), F32), jax.ShapeDtypeStruct((m, nh, hd, hd), F32)],
        compiler_params=_params(("parallel",)),
        name="rwkv_step",
    )(state, *[heads(f) for f in feats], heads(rk)[0], heads(lnw)[0], heads(lnb)[0])
    return o.reshape(m, nh * hd), s1


def _layer_params(i, norm_mix, w_in, q_norm, k_norm, cmp_pe, cmp_w1, cmp_b1, cmp_w2, cmp_b2, shift_mu, w0, w2,
                  a0, a2, g2, k_k, k_a, r_k, ln_w, ln_b, w_out, norm_ffn, w_gate, w_up, w_down, norm_ple,
                  w_ple_gate, w_ple):
    pair = lambda g: jnp.concatenate([g, g])[None]
    return dict(
        norm_mix=norm_mix[i][None], w_in=_pad_w_in(w_in[i]),
        qn=pair(q_norm[i]), kn_sel=pair(k_norm[i, 1]), kn_win=pair(k_norm[i, 2]),
        cw=_compress_weights(cmp_pe[i], cmp_w1[i], cmp_b1[i], cmp_w2[i], cmp_b2[i], k_norm[i, 0]),
        rp=_rwkv_params(shift_mu[i], w0[i], w2[i], a0[i], a2[i], g2[i], k_k[i], k_a[i]),
        rk=r_k[i].reshape(1, RWKV_WIDTH), lnw=ln_w[i][None], lnb=ln_b[i][None],
        w_out=w_out[i].astype(BF16), norm_ffn=norm_ffn[i][None], w_gate=w_gate[i].astype(BF16),
        w_up=w_up[i].astype(BF16), w_down=w_down[i].astype(BF16), norm_ple=norm_ple[i][None],
        w_ple_gate=w_ple_gate[i].astype(BF16), w_ple=w_ple[i].astype(BF16))


def _prompt_layer(x, pe, lp, b, t):
    g, hd = NSA_KV_HEADS, HEAD_DIM
    q, kv, kw, gl, xr = _in_proj(x, lp["norm_mix"], lp["w_in"])
    kv = kv.reshape(b, t, KV_COLS)
    kw = kw.reshape(b, t, WIN_COLS)
    xr = xr.reshape(b, t, XR_PAD)
    kc = _compress_prompt(kv, lp["cw"])
    ksv = _kv_prep(kv, kw, lp["kn_sel"], lp["kn_win"])
    o_nsa = _nsa_prompt(q.reshape(b, t, Q_COLS), gl.reshape(b, t, LANES), lp["qn"], kc, ksv, t)
    feats = _rwkv_prep_seq(xr, jnp.zeros((b, 1, XR_PAD), F32), lp["rp"])
    o_rwkv, s_bd = _rwkv_scan(feats, lp["rk"], lp["lnw"], lp["lnb"])
    x = _out_ffn(x, o_nsa.reshape(b * t, Q_COLS), o_rwkv.reshape(b * t, RWKV_WIDTH), pe, lp)
    keep = min(WINDOW, t)
    return (x, kv.reshape(b, t, 4, g, hd), kw[:, t - keep:].reshape(b, keep, 2, g, hd), _unblock_states(s_bd),
            _unpad_xr(xr[:, -1]))


def _sample_layer(x, pe, lp, cache, pt_flat, n_pages, cache_win, state_wkv, state_shift):
    db = x.shape[0]
    g, hd = NSA_KV_HEADS, HEAD_DIM
    q, kv, kw, gl, xr = _in_proj(x, lp["norm_mix"], lp["w_in"])
    kcn, _, vcn, _ = _compress_paged(cache, pt_flat, db, n_pages, lp["cw"])
    q3 = q.reshape(db, NSA_HEADS, hd)
    z = jnp.zeros((db, NSA_HEADS // 2, hd), F32)
    qm = jnp.concatenate([jnp.concatenate([q3[:, :4], z], -1), jnp.concatenate([z, q3[:, 4:]], -1)], axis=1)
    glm = jnp.pad(gl[:, :GATE_COLS].reshape(db, NSA_HEADS, 3), ((0, 0), (0, 0), (0, LANES - 3)))
    wbuf = cache_win.shape[1]
    o8, win = _nsa_step(cache, pt_flat, qm, glm, kcn, vcn, kv[:, None, :], cache_win.reshape(db, wbuf, WIN_COLS),
                        kw[:, None, :], (lp["qn"], lp["kn_sel"], lp["kn_win"]), n_pages)
    o_nsa = o8.reshape(db, Q_COLS)
    feats = _rwkv_prep_step(xr, _pad_xr(state_shift), lp["rp"])
    o_rwkv, s1 = _rwkv_step(state_wkv, feats, lp["rk"], lp["lnw"], lp["lnb"])
    x = _out_ffn(x, o_nsa, o_rwkv, pe, lp)
    return (x, kv.reshape(db, 1, 4, g, hd), win.reshape(db, wbuf, 2, g, hd), s1, _unpad_xr(xr))


def kernel(x_prompt, x_sample, p_prompt, p_sample, cache_kv, cache_win, state_wkv, state_shift, page_table, norm_mix, w_in, q_norm, k_norm, cmp_pe, cmp_w1, cmp_b1, cmp_w2, cmp_b2, shift_mu, w0, w2, a0, a2, g2, k_k, k_a, r_k, ln_w, ln_b, w_out, norm_ffn, w_gate, w_up, w_down, norm_ple, w_ple_gate, w_ple):
    b, t, d = x_prompt.shape
    db = x_sample.shape[0]
    depth, n_phys, page = cache_kv.shape[:3]
    n_pages = page_table.shape[1]
    assert x_sample.shape[1] == 1 and d == D_MODEL
    cache = cache_kv.reshape(depth * n_phys, page, KV_COLS)
    pt_flat = page_table.reshape(-1).astype(jnp.int32)
    xp = x_prompt.reshape(b * t, d)
    xs = x_sample.reshape(db, d)
    outs_p, outs_s = [], []
    for i in range(depth):
        lp = _layer_params(i, norm_mix, w_in, q_norm, k_norm, cmp_pe, cmp_w1, cmp_b1, cmp_w2, cmp_b2, shift_mu,
                           w0, w2, a0, a2, g2, k_k, k_a, r_k, ln_w, ln_b, w_out, norm_ffn, w_gate, w_up, w_down,
                           norm_ple, w_ple_gate, w_ple)
        xp, *rest_p = _prompt_layer(xp, p_prompt[i].reshape(b * t, -1), lp, b, t)
        xs, *rest_s = _sample_layer(xs, p_sample[i].reshape(db, -1), lp, cache, pt_flat + i * n_phys, n_pages,
                                    cache_win[i], state_wkv[i], state_shift[i])
        outs_p.append(rest_p)
        outs_s.append(rest_s)
    stack = lambda outs, k: jnp.stack([o[k] for o in outs])
    return (xp.reshape(b, t, d), xs.reshape(db, 1, d),
            stack(outs_p, 0), stack(outs_s, 0), stack(outs_p, 1), stack(outs_s, 1),
            stack(outs_p, 2), stack(outs_s, 2), stack(outs_p, 3), stack(outs_s, 3))
```

```python
import functools

import jax
import jax.numpy as jnp
from jax import lax
from jax.experimental import pallas as pl
from jax.experimental.pallas import tpu as pltpu

F32 = jnp.float32
BF16 = jnp.bfloat16
HI = lax.Precision.HIGHEST

D_MODEL = 1024
HEAD_DIM = 64
LANES = 128
NSA_HEADS = 8
NSA_KV_HEADS = 2
RWKV_HEADS = 8
RWKV_WIDTH = RWKV_HEADS * HEAD_DIM
CMP_BLOCK = 32
CMP_STRIDE = 16
CMP_HIDDEN = 256
SEL_BLOCK = 64
SEL_TOPN = 16
WINDOW = 512
Q_BLOCK = 128
DECAY_RANK = 64
A_RANK = 64
GATE_RANK = 128
Q_COLS = NSA_HEADS * HEAD_DIM
KV_COLS = 4 * NSA_KV_HEADS * HEAD_DIM
WIN_COLS = 2 * NSA_KV_HEADS * HEAD_DIM
GATE_COLS = 3 * NSA_HEADS
RWKV_PROJ = 3 * RWKV_WIDTH + DECAY_RANK + A_RANK + GATE_RANK
XR_PAD = 3 * RWKV_WIDTH + 3 * LANES
IN_PAD = Q_COLS + KV_COLS + WIN_COLS + LANES + XR_PAD
NORM_EPS = 1e-6
GN_EPS = 64e-5
BIG = 1e9
NEG = -1e30
RWKV_CHUNK = 64
RWKV_SEQS_PER_STEP = 2
VMEM_LIMIT = 56 * 1024 * 1024


def _dot(a, b, prec=None):
    return lax.dot_general(a, b, (((1,), (0,)), ((), ())), precision=prec, preferred_element_type=F32)


def _dot_nt(a, b, prec=None):
    return lax.dot_general(a, b, (((1,), (1,)), ((), ())), precision=prec, preferred_element_type=F32)


def _dot_tn(a, b, prec=None):
    return lax.dot_general(a, b, (((0,), (0,)), ((), ())), precision=prec, preferred_element_type=F32)


def _iota(shape, dim):
    return lax.broadcasted_iota(jnp.int32, shape, dim)


def _lo_mask(shape):
    return (_iota(shape, len(shape) - 1) % LANES) < HEAD_DIM


def _seg_sum(x):
    lo = _lo_mask(x.shape)
    s_lo = jnp.sum(jnp.where(lo, x, 0.0), axis=-1, keepdims=True)
    s_hi = jnp.sum(jnp.where(lo, 0.0, x), axis=-1, keepdims=True)
    return jnp.where(lo, s_lo, s_hi)


def _seg_rmsnorm(x, gain):
    ms = _seg_sum(x * x) * (1.0 / HEAD_DIM)
    return x * lax.rsqrt(ms + NORM_EPS) * gain


def _rmsnorm_rows(x, gain):
    return x * lax.rsqrt(jnp.mean(x * x, axis=-1, keepdims=True) + NORM_EPS) * gain


def _sigmoid(x):
    return 1.0 / (1.0 + jnp.exp(-x))


def _masked_softmax(s, mask):
    sm = jnp.where(mask, s, NEG)
    m = jnp.max(sm, axis=-1, keepdims=True)
    m = jnp.where(m > 0.5 * NEG, m, 0.0)
    e = jnp.where(mask, jnp.exp(sm - m), 0.0)
    return e / jnp.maximum(jnp.sum(e, axis=-1, keepdims=True), 1e-30)


def _params(sem):
    return pltpu.CompilerParams(dimension_semantics=sem, vmem_limit_bytes=VMEM_LIMIT)


def _const_spec(shape):
    nd = len(shape)
    return pl.BlockSpec(shape, lambda *_: (0,) * nd)


_IN_SPLITS = (("q", 0, Q_COLS), ("kv", Q_COLS, KV_COLS), ("kw", Q_COLS + KV_COLS, WIN_COLS),
              ("gl", Q_COLS + KV_COLS + WIN_COLS, LANES),
              ("xr", Q_COLS + KV_COLS + WIN_COLS + LANES, XR_PAD))


def _in_proj_kernel(x_ref, g_ref, w_ref, *out_refs):
    h = _rmsnorm_rows(x_ref[...], g_ref[...]).astype(BF16)
    for o_ref, (_, lo, n) in zip(out_refs, _IN_SPLITS):
        o_ref[...] = _dot(h, w_ref[:, lo:lo + n])


def _in_proj(x, gain, w_pad):
    m = x.shape[0]
    tm = min(m, 512)
    assert m % tm == 0
    return pl.pallas_call(
        _in_proj_kernel,
        grid=(m // tm,),
        in_specs=[pl.BlockSpec((tm, D_MODEL), lambda i: (i, 0)),
                  _const_spec((1, D_MODEL)),
                  _const_spec((D_MODEL, IN_PAD))],
        out_specs=[pl.BlockSpec((tm, n), lambda i: (i, 0)) for _, _, n in _IN_SPLITS],
        out_shape=[jax.ShapeDtypeStruct((m, n), F32) for _, _, n in _IN_SPLITS],
        compiler_params=_params(("parallel",)),
        name="in_proj",
    )(x, gain, w_pad)


def _pad_xr(v):
    z = jnp.zeros(v.shape[:-1] + (LANES - DECAY_RANK,), v.dtype)
    c = 3 * RWKV_WIDTH
    return jnp.concatenate([v[..., :c + DECAY_RANK], z, v[..., c + DECAY_RANK:c + DECAY_RANK + A_RANK], z,
                            v[..., c + DECAY_RANK + A_RANK:]], axis=-1)


def _unpad_xr(v):
    c = 3 * RWKV_WIDTH
    return jnp.concatenate([v[..., :c + DECAY_RANK], v[..., c + LANES:c + LANES + A_RANK], v[..., c + 2 * LANES:]],
                           axis=-1)


def _pad_w_in(w_in):
    c = Q_COLS + KV_COLS + WIN_COLS
    gl = jnp.pad(w_in[:, c:c + GATE_COLS], ((0, 0), (0, LANES - GATE_COLS)))
    return jnp.concatenate([w_in[:, :c], gl, _pad_xr(w_in[:, c + GATE_COLS:])], axis=-1).astype(BF16)


def _ffn_kernel(x_ref, on_ref, or_ref, pe_ref, wout_ref, nf_ref, wg_ref, wu_ref, wd_ref, np_ref, wpg_ref,
                wple_ref, y_ref, x1_s, hf_s, acc_s):
    j = pl.program_id(1)

    @pl.when(j == 0)
    def _():
        half = D_MODEL // 2
        o = _dot(on_ref[...].astype(BF16), wout_ref[0:half, :]) + _dot(or_ref[...].astype(BF16), wout_ref[half:, :])
        x1 = x_ref[...] + o
        x1_s[...] = x1
        hf_s[...] = _rmsnorm_rows(x1, nf_ref[...]).astype(BF16)
        acc_s[...] = jnp.zeros_like(acc_s)

    hf = hf_s[...]
    g = _dot(hf, wg_ref[...])
    u = _dot(hf, wu_ref[...])
    act = (g * _sigmoid(g)) * u
    acc_s[...] += _dot(act.astype(BF16), wd_ref[...])

    @pl.when(j == pl.num_programs(1) - 1)
    def _():
        x2 = x1_s[...] + acc_s[...]
        hp = _rmsnorm_rows(x2, np_ref[...]).astype(BF16)
        pg = _sigmoid(_dot(hp, wpg_ref[...]))
        y_ref[...] = x2 + pg * _dot(pe_ref[...].astype(BF16), wple_ref[...])


def _out_ffn(x, o_nsa, o_rwkv, pe, lw):
    m = x.shape[0]
    tm = min(m, 512)
    hid = lw["w_gate"].shape[1]
    th = 256
    assert m % tm == 0 and hid % th == 0
    ple = pe.shape[1]
    row = lambda i, j: (i, 0)
    return pl.pallas_call(
        _ffn_kernel,
        grid=(m // tm, hid // th),
        in_specs=[pl.BlockSpec((tm, D_MODEL), row),
                  pl.BlockSpec((tm, D_MODEL // 2), row),
                  pl.BlockSpec((tm, D_MODEL // 2), row),
                  pl.BlockSpec((tm, ple), row),
                  _const_spec((D_MODEL, D_MODEL)),
                  _const_spec((1, D_MODEL)),
                  pl.BlockSpec((D_MODEL, th), lambda i, j: (0, j)),
                  pl.BlockSpec((D_MODEL, th), lambda i, j: (0, j)),
                  pl.BlockSpec((th, D_MODEL), lambda i, j: (j, 0)),
                  _const_spec((1, D_MODEL)),
                  _const_spec((D_MODEL, D_MODEL)),
                  _const_spec((ple, D_MODEL))],
        out_specs=pl.BlockSpec((tm, D_MODEL), row),
        out_shape=jax.ShapeDtypeStruct((m, D_MODEL), F32),
        scratch_shapes=[pltpu.VMEM((tm, D_MODEL), F32), pltpu.VMEM((tm, D_MODEL), BF16),
                        pltpu.VMEM((tm, D_MODEL), F32)],
        compiler_params=_params(("parallel", "arbitrary")),
        name="out_ffn_ple",
    )(x, o_nsa, o_rwkv, pe, lw["w_out"], lw["norm_ffn"], lw["w_gate"], lw["w_up"], lw["w_down"],
      lw["norm_ple"], lw["w_ple_gate"], lw["w_ple"])


def _gelu_tanh(x):
    return 0.5 * x * (1.0 + jnp.tanh(0.7978845608028654 * (x + 0.044715 * x * x * x)))


def _compress_body(load_rows, nbp, first_step, pe_ref, wbd_ref, b1_ref, w2lo_ref, w2hi_ref, b2_ref, kn_ref,
                   kcn_ref, kcr_ref, vcn_ref, vcr_ref, bias_s):
    hid = CMP_HIDDEN

    @pl.when(first_step)
    def _():
        for ty in range(2):
            acc = jnp.zeros((1, hid), F32)
            for r in range(CMP_STRIDE):
                t = _dot(pe_ref[ty, r:r + 1, :].astype(BF16), wbd_ref[ty, r])
                acc = acc + t[:, 0:hid] + t[:, 3 * hid:4 * hid]
            bias_s[ty:ty + 1, :] = acc + b1_ref[ty:ty + 1, :]

    acc = [jnp.zeros((nbp, 4 * hid), F32) for _ in range(2)]
    for r in range(CMP_STRIDE):
        for ty in range(2):
            acc[ty] = acc[ty] + _dot(load_rows(r, ty).astype(BF16), wbd_ref[ty, r])
    outs = ((kcn_ref, kcr_ref), (vcn_ref, vcr_ref))
    for ty in range(2):
        hs = []
        for g in range(2):
            a = acc[ty][:, 2 * hid * g:2 * hid * g + hid]
            b = acc[ty][:, 2 * hid * g + hid:2 * hid * (g + 1)]
            b_next = pltpu.roll(b, nbp - 1, 0)
            hs.append(_gelu_tanh(a + b_next + bias_s[ty:ty + 1, :]).astype(BF16))
        o_n = _dot(hs[0], w2lo_ref[ty]) + _dot(hs[1], w2hi_ref[ty]) + b2_ref[ty:ty + 1, :]
        o_r = _dot(hs[0], w2hi_ref[ty]) + _dot(hs[1], w2lo_ref[ty]) + b2_ref[ty:ty + 1, :]
        if ty == 0:
            o_n = _seg_rmsnorm(o_n, kn_ref[...])
            o_r = _seg_rmsnorm(o_r, kn_ref[...])
        outs[ty][0][0] = o_n.astype(outs[ty][0].dtype)
        outs[ty][1][0] = o_r.astype(outs[ty][1].dtype)


def _compress_prompt_kernel(k_ref, v_ref, *refs, nbp):
    load = lambda r, ty: (k_ref, v_ref)[ty][0, pl.ds(r, nbp, stride=CMP_STRIDE), :]
    _compress_body(load, nbp, pl.program_id(0) == 0, *refs)


def _compress_paged_kernel(pt_ref, *refs, nbp, n_pages, page):
    del pt_ref
    page_refs, rest, rows_s = refs[:n_pages], refs[n_pages:-2], refs[-2:]
    for j in range(n_pages):
        for ty in range(2):
            rows_s[ty][j * page:(j + 1) * page, :] = page_refs[j][0, :, LANES * ty:LANES * (ty + 1)]
    load = lambda r, ty: rows_s[ty][pl.ds(r, nbp, stride=CMP_STRIDE), :]
    _compress_body(load, nbp, pl.program_id(0) == 0, *rest)


def _compress_weight_specs():
    hid = CMP_HIDDEN
    return [_const_spec((2, CMP_STRIDE, LANES)), _const_spec((2, CMP_STRIDE, LANES, 4 * hid)),
            _const_spec((2, hid)), _const_spec((2, hid, LANES)), _const_spec((2, hid, LANES)),
            _const_spec((2, LANES)), _const_spec((1, LANES))]


def _compress_outs(b, nbp):
    spec = pl.BlockSpec((1, nbp, LANES), lambda i, *_: (i, 0, 0))
    shapes = [jax.ShapeDtypeStruct((b, nbp, LANES), dt) for dt in (F32, F32, BF16, BF16)]
    return [spec] * 4, shapes


def _compress_prompt(kv, cw):
    b, t, _ = kv.shape
    nbp = t // CMP_STRIDE
    out_specs, out_shape = _compress_outs(b, nbp)
    return pl.pallas_call(
        functools.partial(_compress_prompt_kernel, nbp=nbp),
        grid=(b,),
        in_specs=[pl.BlockSpec((1, t, LANES), lambda i: (i, 0, 0)),
                  pl.BlockSpec((1, t, LANES), lambda i: (i, 0, 1))] + _compress_weight_specs(),
        out_specs=out_specs, out_shape=out_shape,
        scratch_shapes=[pltpu.VMEM((2, CMP_HIDDEN), F32)],
        compiler_params=_params(("arbitrary",)),
        name="compress_prompt",
    )(kv, kv, *cw)


def _compress_paged(cache, pt_flat, n_seq, n_pages, cw):
    page = cache.shape[1]
    nbp = n_pages * page // CMP_STRIDE
    out_specs, out_shape = _compress_outs(n_seq, nbp)
    page_specs = [pl.BlockSpec((1, page, 2 * LANES), functools.partial(
        lambda i, pt, j: (pt[i * n_pages + j], 0, 0), j=j)) for j in range(n_pages)]
    return pl.pallas_call(
        functools.partial(_compress_paged_kernel, nbp=nbp, n_pages=n_pages, page=page),
        grid_spec=pltpu.PrefetchScalarGridSpec(
            num_scalar_prefetch=1, grid=(n_seq,),
            in_specs=page_specs + _compress_weight_specs(),
            out_specs=out_specs,
            scratch_shapes=[pltpu.VMEM((2, CMP_HIDDEN), F32), pltpu.VMEM((n_pages * page, LANES), F32),
                            pltpu.VMEM((n_pages * page, LANES), F32)]),
        out_shape=out_shape,
        compiler_params=_params(("arbitrary",)),
        name="compress_paged",
    )(pt_flat, *([cache] * n_pages), *cw)


def _compress_weights(cmp_pe, cmp_w1, cmp_b1, cmp_w2, cmp_b2, k_norm0):
    hid = CMP_HIDDEN
    w1 = cmp_w1.reshape(2, CMP_BLOCK, HEAD_DIM, hid)
    wcat = jnp.concatenate([w1[:, :CMP_STRIDE], w1[:, CMP_STRIDE:]], axis=-1)
    z = jnp.zeros_like(wcat)
    wbd = jnp.concatenate([jnp.concatenate([wcat, z], -1), jnp.concatenate([z, wcat], -1)], axis=2).astype(BF16)
    pe_pair = jnp.concatenate([cmp_pe[:, :CMP_STRIDE], cmp_pe[:, CMP_STRIDE:]], axis=-1)
    z2 = jnp.zeros_like(cmp_w2)
    w2lo = jnp.concatenate([cmp_w2, z2], -1).astype(BF16)
    w2hi = jnp.concatenate([z2, cmp_w2], -1).astype(BF16)
    b2 = jnp.concatenate([cmp_b2, cmp_b2], -1)
    kn = jnp.concatenate([k_norm0, k_norm0])[None]
    return (pe_pair, wbd, cmp_b1, w2lo, w2hi, b2, kn)


def _kv_prep_kernel(kv_ref, kw_ref, kns_ref, knw_ref, ksn_ref, ksr_ref, vsn_ref, vsr_ref, kwn_ref, kwr_ref,
                    vwn_ref, vwr_ref):
    def put(x, n_ref, r_ref):
        n_ref[0] = x.astype(BF16)
        r_ref[0] = pltpu.roll(x, HEAD_DIM, 1).astype(BF16)

    put(_seg_rmsnorm(kv_ref[0, :, 2 * LANES:3 * LANES], kns_ref[...]), ksn_ref, ksr_ref)
    put(kv_ref[0, :, 3 * LANES:4 * LANES], vsn_ref, vsr_ref)
    put(_seg_rmsnorm(kw_ref[0, :, 0:LANES], knw_ref[...]), kwn_ref, kwr_ref)
    put(kw_ref[0, :, LANES:2 * LANES], vwn_ref, vwr_ref)


def _kv_prep(kv, kw, kn_sel, kn_win):
    b, t, _ = kv.shape
    tm = min(t, 512)
    spec = lambda n: pl.BlockSpec((1, tm, n), lambda i, j: (i, j, 0))
    return pl.pallas_call(
        _kv_prep_kernel,
        grid=(b, t // tm),
        in_specs=[spec(4 * LANES), spec(2 * LANES), _const_spec((1, LANES)), _const_spec((1, LANES))],
        out_specs=[spec(LANES)] * 8,
        out_shape=[jax.ShapeDtypeStruct((b, t, LANES), BF16)] * 8,
        compiler_params=_params(("parallel", "parallel")),
        name="kv_prep",
    )(kv, kw, kn_sel, kn_win)


def _overlap_t(ns, nbp):
    j = _iota((ns, nbp), 0) * SEL_BLOCK
    n = _iota((ns, nbp), 1) * CMP_STRIDE
    return jnp.where((n < j + SEL_BLOCK) & (n + CMP_BLOCK > j), 1.0, 0.0).astype(F32)


def _nsa_prompt_kernel(q_ref, gl_ref, qn_ref, kcn_ref, kcr_ref, vcn_ref, vcr_ref, ksn_ref, ksr_ref, vsn_ref,
                       vsr_ref, kwn_ref, kwr_ref, vwn_ref, vwr_ref, o_ref, *, seq, tk, wk):
    i = pl.program_id(1)
    qb = Q_BLOCK
    nbp = kcn_ref.shape[1]
    nb = nbp - 1
    ns = seq // SEL_BLOCK
    n_sel = min(SEL_TOPN, ns)
    lo = _lo_mask((qb, LANES))
    pos = i * qb + _iota((qb, 1), 0)
    sig = _sigmoid(gl_ref[0])

    qp = [_seg_rmsnorm(q_ref[0, :, LANES * p:LANES * (p + 1)], qn_ref[...]) * (HEAD_DIM ** -0.5) for p in range(4)]
    q_lo = [jnp.where(lo, x, 0.0) for x in qp]
    q_hi = [jnp.where(lo, 0.0, x) for x in qp]

    n_c = _iota((qb, nbp), 1)
    mask_c = (n_c * CMP_STRIDE + (CMP_BLOCK - 1) <= pos) & (n_c < nb)
    ov_t = _overlap_t(ns, nbp)
    j_t = _iota((ns, qb), 0)
    pos_t = i * qb + _iota((ns, qb), 1)
    cur_t = pos_t >> 6
    forced_t = (j_t == 0) | (j_t == cur_t) | (j_t == cur_t - 1)
    causal_t = j_t * SEL_BLOCK <= pos_t
    eye = jnp.where(_iota((qb, qb), 0) == _iota((qb, qb), 1), 1.0, 0.0).astype(BF16)

    w_start = pl.multiple_of(jnp.maximum(i * qb - (wk - qb), 0), qb)
    w_pos = w_start + _iota((qb, wk), 1)
    w_dist = pos - w_pos
    mask_w = (w_dist >= 0) & (w_dist <= WINDOW)
    n_tiles = (i * qb + qb - 1) // tk + 1

    order = lambda g, n, r: (n, r) if g == 0 else (r, n)
    heads = []
    for p in range(4):
        heads += [(p, p // 2, q_lo[p], 0), (p, p // 2, q_hi[p], 1)]

    p_cmp = [_masked_softmax(_dot_nt(qt, order(g, kcn_ref, kcr_ref)[w][0], HI), mask_c) for _, g, qt, w in heads]
    o_cmp = [_dot(pc.astype(BF16), order(g, vcn_ref, vcr_ref)[w][0]) for pc, (_, g, _, w) in zip(p_cmp, heads)]
    unsel = []
    for g in range(2):
        psum = p_cmp[4 * g] + p_cmp[4 * g + 1] + p_cmp[4 * g + 2] + p_cmp[4 * g + 3]
        imp_t = _dot_nt(ov_t, psum, HI)
        score = jnp.where(forced_t, BIG, jnp.where(causal_t, imp_t, -BIG))
        rank = jnp.zeros((ns, qb), jnp.int32)
        for jp in range(ns):
            row = score[jp:jp + 1, :]
            rank = rank + jnp.where((row > score) | ((row == score) & (j_t > jp)), 1, 0)
        unsel.append(_dot_nt(eye, jnp.where(rank < n_sel, 0.0, NEG).astype(BF16)).astype(BF16))

    q16 = [qt.astype(BF16) for _, _, qt, _ in heads]

    def tile(kt, carry, diagonal):
        k0 = pl.multiple_of(kt * tk, tk)
        blk = (k0 + _iota((ns, tk), 1)) >> 6
        expand = jnp.where(_iota((ns, tk), 0) == blk, 1.0, 0.0).astype(BF16)
        bias = [_dot(u, expand) for u in unsel]
        if diagonal:
            causal = k0 + _iota((qb, tk), 1) <= pos
            bias = [jnp.where(causal, x, NEG) for x in bias]
        out = []
        for (m, l, acc), q_, (_, g, _, w) in zip(carry, q16, heads):
            k_ref = order(g, ksn_ref, ksr_ref)[w]
            v_ref = order(g, vsn_ref, vsr_ref)[w]
            s = _dot_nt(q_, k_ref[0, pl.ds(k0, tk), :]) + bias[g]
            m_new = jnp.maximum(m, jnp.max(s, axis=-1, keepdims=True))
            alpha = jnp.exp(m - m_new)
            e = jnp.exp(s - m_new)
            l_new = alpha * l + jnp.sum(e, axis=-1, keepdims=True)
            acc_new = alpha * acc + _dot(e.astype(BF16), v_ref[0, pl.ds(k0, tk), :])
            out.append((m_new, l_new, acc_new))
        return tuple(out)

    init = (jnp.full((qb, 1), NEG, F32), jnp.zeros((qb, 1), F32), jnp.zeros((qb, LANES), F32))
    carry = lax.fori_loop(0, n_tiles - 1, functools.partial(tile, diagonal=False), (init,) * len(heads))
    o_sel = [acc / l for _, l, acc in tile(n_tiles - 1, carry, True)]

    o_win = []
    for q_, (_, g, _, w) in zip(q16, heads):
        pw = _masked_softmax(_dot_nt(q_, order(g, kwn_ref, kwr_ref)[w][0, pl.ds(w_start, wk), :]), mask_w)
        o_win.append(_dot(pw.astype(BF16), order(g, vwn_ref, vwr_ref)[w][0, pl.ds(w_start, wk), :]))

    for p in range(4):
        gate = lambda c: jnp.where(lo, sig[:, 6 * p + c:6 * p + c + 1], sig[:, 6 * p + 3 + c:6 * p + 4 + c])
        both = lambda o: jnp.where(lo, o[2 * p], o[2 * p + 1])
        o_ref[0, :, LANES * p:LANES * (p + 1)] = gate(0) * both(o_cmp) + gate(1) * both(o_sel) + gate(2) * both(o_win)


def _nsa_prompt(q, gl, qn, kc, ksv, seq):
    b = q.shape[0]
    nbp = kc[0].shape[1]
    tk = min(seq, 256)
    wk = min(seq, WINDOW + Q_BLOCK)
    assert seq % Q_BLOCK == 0 and seq % tk == 0
    blk = lambda n: pl.BlockSpec((1, Q_BLOCK, n), lambda bi, i: (bi, i, 0))
    full = lambda rows: pl.BlockSpec((1, rows, LANES), lambda bi, i: (bi, 0, 0))
    return pl.pallas_call(
        functools.partial(_nsa_prompt_kernel, seq=seq, tk=tk, wk=wk),
        grid=(b, seq // Q_BLOCK),
        in_specs=[blk(4 * LANES), blk(LANES), _const_spec((1, LANES))] + [full(nbp)] * 4 + [full(seq)] * 8,
        out_specs=blk(4 * LANES),
        out_shape=jax.ShapeDtypeStruct((b, seq, 4 * LANES), F32),
        compiler_params=_params(("parallel", "arbitrary")),
        name="nsa_prompt",
    )(q, gl, qn, *kc, *ksv)


def _nsa_step_kernel(pt_ref, *refs, n_pages, page, past):
    del pt_ref
    page_refs = refs[:n_pages]
    (qm_ref, gl_ref, kcn_ref, vcn_ref, kvn_ref, cw_ref, kwn_ref, qn_ref, kns_ref, knw_ref,
     o_ref, win_ref) = refs[n_pages:]
    nh = NSA_HEADS
    nbp = kcn_ref.shape[1]
    nb = nbp - 1
    ns = past // SEL_BLOCK + 1
    n_sel = min(SEL_TOPN, ns)
    wbuf = cw_ref.shape[1]

    qn = _seg_rmsnorm(qm_ref[0], qn_ref[...]) * (HEAD_DIM ** -0.5)
    qb16 = qn.astype(BF16)
    sg = _sigmoid(gl_ref[0])

    n_c = _iota((nh, nbp), 1)
    mask_c = (n_c * CMP_STRIDE + (CMP_BLOCK - 1) <= past) & (n_c < nb)
    p_c = _masked_softmax(_dot_nt(qn, kcn_ref[0], HI), mask_c)
    o_cmp = _dot(p_c.astype(BF16), vcn_ref[0])
    rep = NSA_HEADS // NSA_KV_HEADS
    same_group = jnp.where(_iota((nh, nh), 0) // rep == _iota((nh, nh), 1) // rep, 1.0, 0.0).astype(F32)
    psum = _dot(same_group, p_c, HI)
    imp = _dot_nt(psum, _overlap_t(LANES, nbp), HI)
    j = _iota((nh, LANES), 1)
    cur = past // SEL_BLOCK
    forced = (j == 0) | (j == cur) | (j == cur - 1)
    score = jnp.where(forced, BIG, jnp.where(j * SEL_BLOCK <= past, imp, -BIG))
    score = jnp.where(j < ns, score, -4.0 * BIG)
    rank = jnp.zeros((nh, LANES), jnp.int32)
    for jp in range(ns):
        col = score[:, jp:jp + 1]
        rank = rank + jnp.where((col > score) | ((col == score) & (j > jp)), 1, 0)
    sel = jnp.where((rank < n_sel) & (j < ns), 1.0, 0.0).astype(BF16)

    kv_new = kvn_ref[0]
    k_new = _seg_rmsnorm(kv_new[:, 0:LANES], kns_ref[...])
    s_new = jnp.sum(qn * k_new, axis=-1, keepdims=True)
    s_parts, m_parts, v_parts = [], [], []
    for pj in range(n_pages):
        pg = page_refs[pj][0]
        kp = _seg_rmsnorm(pg[:, 0:LANES], kns_ref[...]).astype(BF16)
        v_parts.append(pg[:, LANES:2 * LANES].astype(BF16))
        blk = (pj * page + _iota((LANES, page), 1)) >> 6
        expand = jnp.where(_iota((LANES, page), 0) == blk, 1.0, 0.0).astype(BF16)
        m_parts.append(_dot(sel, expand) > 0.5)
        s_parts.append(_dot_nt(qb16, kp))
    s_all = jnp.concatenate(s_parts, axis=-1)
    valid = jnp.concatenate(m_parts, axis=-1)
    s_all = jnp.where(valid, s_all, NEG)
    m = jnp.maximum(jnp.max(s_all, axis=-1, keepdims=True), s_new)
    e_all = jnp.where(valid, jnp.exp(s_all - m), 0.0)
    e_new = jnp.exp(s_new - m)
    acc = e_new * kv_new[:, LANES:2 * LANES]
    for pj in range(n_pages):
        acc = acc + _dot(e_all[:, pj * page:(pj + 1) * page].astype(BF16), v_parts[pj])
    o_sel = acc / (jnp.sum(e_all, axis=-1, keepdims=True) + e_new)

    cw = cw_ref[0]
    kw_new = kwn_ref[0]
    kwn = _seg_rmsnorm(cw[:, 0:LANES], knw_ref[...]).astype(BF16)
    s_w = _dot_nt(qb16, kwn)
    s_wn = jnp.sum(qn * _seg_rmsnorm(kw_new[:, 0:LANES], knw_ref[...]), axis=-1, keepdims=True)
    m_w = jnp.maximum(jnp.max(s_w, axis=-1, keepdims=True), s_wn)
    e_w = jnp.exp(s_w - m_w)
    e_wn = jnp.exp(s_wn - m_w)
    o_win = (_dot(e_w.astype(BF16), cw[:, LANES:2 * LANES].astype(BF16)) + e_wn * kw_new[:, LANES:2 * LANES]) / (
        jnp.sum(e_w, axis=-1, keepdims=True) + e_wn)

    o = sg[:, 0:1] * o_cmp + sg[:, 1:2] * o_sel + sg[:, 2:3] * o_win
    first_group = _iota((nh, LANES), 0) < rep
    o_ref[0] = jnp.where(first_group, o, pltpu.roll(o, HEAD_DIM, 1))[:, 0:HEAD_DIM]
    last = _iota((wbuf, 2 * LANES), 0) == wbuf - 1
    win_ref[0] = jnp.where(last, kw_new, pltpu.roll(cw, wbuf - 1, 0))


def _nsa_step(cache, pt_flat, qm, glm, kcn, vcn, kv_new, cache_win, kw_new, gains, n_pages):
    db = qm.shape[0]
    page = cache.shape[1]
    past = n_pages * page
    nbp = kcn.shape[1]
    wbuf = cache_win.shape[1]
    assert wbuf == min(WINDOW, past) and page == LANES and past // SEL_BLOCK + 1 <= LANES
    per = lambda shape, last=0: pl.BlockSpec((1,) + shape, lambda i, pt: (i, 0, last))
    page_specs = [pl.BlockSpec((1, page, 2 * LANES), functools.partial(
        lambda i, pt, j: (pt[i * n_pages + j], 0, 1), j=j)) for j in range(n_pages)]
    return pl.pallas_call(
        functools.partial(_nsa_step_kernel, n_pages=n_pages, page=page, past=past),
        grid_spec=pltpu.PrefetchScalarGridSpec(
            num_scalar_prefetch=1, grid=(db,),
            in_specs=page_specs + [per((NSA_HEADS, LANES)), per((NSA_HEADS, LANES)), per((nbp, LANES)),
                                   per((nbp, LANES)), per((1, 2 * LANES), 1), per((wbuf, 2 * LANES)),
                                   per((1, 2 * LANES))] + [_const_spec((1, LANES))] * 3,
            out_specs=[per((NSA_HEADS, HEAD_DIM)), per((wbuf, 2 * LANES))]),
        out_shape=[jax.ShapeDtypeStruct((db, NSA_HEADS, HEAD_DIM), F32),
                   jax.ShapeDtypeStruct((db, wbuf, 2 * LANES), F32)],
        compiler_params=_params(("parallel",)),
        name="nsa_step",
    )(pt_flat, *([cache] * n_pages), qm, glm, kcn, vcn, kv_new, cache_win, kw_new, *gains)


def _softplus(z):
    return jnp.maximum(z, 0.0) + jnp.log(1.0 + jnp.exp(-jnp.abs(z)))


def _rwkv_features(x, prev, mu_ref, w0_ref, a0_ref, w2_ref, a2_ref, g2_ref, kk_ref, ka_ref, outs):
    r_ref, lw_ref, k_ref, v_ref, a_ref, b_ref, g_ref = outs
    rw = RWKV_WIDTH
    xs = x + (prev - x) * mu_ref[...]
    r, k, v = xs[:, 0:rw], xs[:, rw:2 * rw], xs[:, 2 * rw:3 * rw]
    wd = xs[:, 3 * rw:3 * rw + LANES]
    ad = xs[:, 3 * rw + LANES:3 * rw + 2 * LANES]
    gd = xs[:, 3 * rw + 2 * LANES:3 * rw + 3 * LANES]
    w_log = -_softplus(-(w0_ref[...] + _dot(jnp.tanh(wd).astype(BF16), w2_ref[...]))) - 0.5
    a_sig = _sigmoid(a0_ref[...] + _dot(ad.astype(BF16), a2_ref[...]))
    kk = k * kk_ref[...]
    kk2 = kk * kk
    norm2 = jnp.concatenate([_seg_sum(kk2[:, LANES * p:LANES * (p + 1)]) for p in range(rw // LANES)], axis=-1)
    kk = kk / jnp.maximum(jnp.sqrt(norm2), 1e-12)
    r_ref[...] = r.reshape(r_ref.shape)
    lw_ref[...] = (-jnp.exp(w_log)).reshape(lw_ref.shape)
    k_ref[...] = (k * (1.0 + (a_sig - 1.0) * ka_ref[...])).reshape(k_ref.shape)
    v_ref[...] = v.reshape(v_ref.shape)
    a_ref[...] = (-kk).reshape(a_ref.shape)
    b_ref[...] = (kk * a_sig).reshape(b_ref.shape)
    g_ref[...] = _dot(_sigmoid(gd).astype(BF16), g2_ref[...]).reshape(g_ref.shape)


def _rwkv_prep_seq_kernel(xr_ref, shift_ref, *refs):
    params, outs, last_s = refs[:8], refs[8:15], refs[15]
    t = pl.program_id(1)

    @pl.when(t == 0)
    def _():
        last_s[...] = shift_ref[0]

    x = xr_ref[0]
    tm = x.shape[0]
    prev = jnp.where(_iota(x.shape, 0) == 0, last_s[...], pltpu.roll(x, 1, 0))
    last_s[...] = x[tm - 1:tm, :]
    _rwkv_features(x, prev, *params, outs)


def _rwkv_prep_step_kernel(xr_ref, shift_ref, *refs):
    _rwkv_features(xr_ref[...], shift_ref[...], *refs[:8], refs[8:15])


def _rwkv_param_specs():
    one = _const_spec((1, RWKV_WIDTH))
    low = _const_spec((LANES, RWKV_WIDTH))
    return [_const_spec((1, XR_PAD)), one, one, low, low, low, one, one]


def _rwkv_prep_seq(xr, shift0, rp):
    b, t, _ = xr.shape
    tm = min(t, 512)
    out = pl.BlockSpec((1, tm, RWKV_WIDTH), lambda i, j: (i, j, 0))
    return pl.pallas_call(
        _rwkv_prep_seq_kernel,
        grid=(b, t // tm),
        in_specs=[pl.BlockSpec((1, tm, XR_PAD), lambda i, j: (i, j, 0)),
                  pl.BlockSpec((1, 1, XR_PAD), lambda i, j: (i, 0, 0))] + _rwkv_param_specs(),
        out_specs=[out] * 7,
        out_shape=[jax.ShapeDtypeStruct((b, t, RWKV_WIDTH), F32)] * 7,
        scratch_shapes=[pltpu.VMEM((1, XR_PAD), F32)],
        compiler_params=_params(("parallel", "arbitrary")),
        name="rwkv_prep_seq",
    )(xr, shift0, *rp)


def _rwkv_prep_step(xr, shift0, rp):
    m = xr.shape[0]
    return pl.pallas_call(
        _rwkv_prep_step_kernel,
        grid=(1,),
        in_specs=[_const_spec((m, XR_PAD)), _const_spec((m, XR_PAD))] + _rwkv_param_specs(),
        out_specs=[_const_spec((m, RWKV_WIDTH))] * 7,
        out_shape=[jax.ShapeDtypeStruct((m, RWKV_WIDTH), F32)] * 7,
        compiler_params=_params(("arbitrary",)),
        name="rwkv_prep_step",
    )(xr, shift0, *rp)


def _rwkv_params(shift_mu, w0, w2, a0, a2, g2, k_k, k_a):
    pad = lambda w: jnp.pad(w, ((0, LANES - w.shape[0]), (0, 0))).astype(BF16)
    return (_pad_xr(shift_mu)[None], w0[None], a0[None], pad(w2), pad(a2), g2.astype(BF16), k_k[None], k_a[None])


def _rwkv_finish(y, r, k, v, gate, rk, lnw, lnb):
    mu = _seg_sum(y) * (1.0 / HEAD_DIM)
    d = y - mu
    var = _seg_sum(d * d) * (1.0 / HEAD_DIM)
    yn = d * lax.rsqrt(var + GN_EPS) * lnw + lnb
    return (yn + _seg_sum(r * k * rk) * v) * gate


def _rwkv_scan_kernel(r_ref, lw_ref, k_ref, v_ref, a_ref, b_ref, g_ref, rk_ref, lnw_ref, lnb_ref,
                      o_ref, s_out_ref, s_s):
    c = pl.program_id(1)
    ch = r_ref.shape[1]
    gw = 4 * HEAD_DIM
    rows = 4 * ch

    @pl.when(c == 0)
    def _():
        s_s[...] = jnp.zeros_like(s_s)

    tril = jnp.where(_iota((ch, ch), 0) >= _iota((ch, ch), 1), 1.0, 0.0).astype(F32)
    t_row = _iota((rows, rows), 0) % ch
    t_col = _iota((rows, rows), 1) % ch
    strict = t_row > t_col
    incl = t_row >= t_col
    eye = jnp.where(_iota((rows, rows), 0) == _iota((rows, rows), 1), 1.0, 0.0).astype(F32)
    head_of_lane = _iota((ch, gw), 1) // HEAD_DIM

    def wide(x):
        return jnp.concatenate([jnp.where(head_of_lane == h, x, 0.0) for h in range(4)], axis=0).astype(BF16)

    units = [(i, q) for i in range(r_ref.shape[0]) for q in range(RWKV_WIDTH // gw)]
    each = lambda f, *cols: [f(*xs) for xs in zip(*cols)]
    load = lambda ref: [ref[i, :, gw * q:gw * (q + 1)] for i, q in units]
    r, lw, k, v, a, b = (load(ref) for ref in (r_ref, lw_ref, k_ref, v_ref, a_ref, b_ref))
    s0 = [s_s[i, q] for i, q in units]
    cl = each(lambda x: _dot(tril, x, HI), lw)
    g_in = each(jnp.exp, cl)
    g_inv = each(lambda x: jnp.exp(-x), cl)
    w_ar = each(lambda a_, r_, cl_, lw_, gi: jnp.concatenate([wide(a_ * jnp.exp(cl_ - lw_)), wide(r_ * gi)], axis=0),
                a, r, cl, lw, g_in)
    w_bk = each(lambda b_, k_, gv: jnp.concatenate([wide(b_ * gv), wide(k_ * gv)], axis=0), b, k, g_inv)
    w_v = each(wide, v)
    p_all = each(_dot_nt, w_ar, w_bk)
    xy0 = each(lambda x, s: _dot_nt(x, s.astype(BF16)), w_ar, s0)
    n = each(lambda p: jnp.where(strict, p[:rows, :rows], 0.0), p_all)
    l_ak = each(lambda p: jnp.where(strict, p[:rows, rows:], 0.0).astype(BF16), p_all)
    m_rb = each(lambda p: jnp.where(incl, p[rows:, :rows], 0.0).astype(BF16), p_all)
    m_rk = each(lambda p: jnp.where(incl, p[rows:, rows:], 0.0).astype(BF16), p_all)
    t_inv = each(lambda x: eye + x, n)
    pw = each(lambda x: x.astype(BF16), n)
    for _ in range(max(ch.bit_length() - 2, 0)):
        pw = each(lambda x: _dot(x, x).astype(BF16), pw)
        t_inv = each(lambda t, x: t + _dot(t.astype(BF16), x), t_inv, pw)
    x_w = each(lambda xy, l, vv: (xy[:rows] + _dot(l, vv)).astype(BF16), xy0, l_ak, w_v)
    u = each(lambda t, x: _dot(t.astype(BF16), x).astype(BF16), t_inv, x_w)
    y_w = each(lambda xy, mb, uu, mk, vv: xy[rows:] + _dot(mb, uu) + _dot(mk, vv), xy0, m_rb, u, m_rk, w_v)
    upd = each(lambda uu, vv, bk: _dot_tn(jnp.concatenate([uu, vv], axis=0), bk), u, w_v, w_bk)
    for j, (i, q) in enumerate(units):
        s_s[i, q] = (s0[j] + upd[j]) * g_in[j][ch - 1:ch, :]
        y = y_w[j][0:ch] + y_w[j][ch:2 * ch] + y_w[j][2 * ch:3 * ch] + y_w[j][3 * ch:4 * ch]
        for p in range(gw // LANES):
            ps = slice(LANES * p, LANES * (p + 1))
            po = slice(gw * q + LANES * p, gw * q + LANES * (p + 1))
            o_ref[i, :, po] = _rwkv_finish(y[:, ps], r[j][:, ps], k[j][:, ps], v[j][:, ps], g_ref[i, :, po],
                                           rk_ref[:, po], lnw_ref[:, po], lnb_ref[:, po])

    @pl.when(c == pl.num_programs(1) - 1)
    def _():
        s_out_ref[...] = s_s[...]


def _rwkv_scan(feats, rk, lnw, lnb):
    b, t, _ = feats[0].shape
    ch = min(t, RWKV_CHUNK)
    assert t % ch == 0 and ch & (ch - 1) == 0
    bb = RWKV_SEQS_PER_STEP if b % RWKV_SEQS_PER_STEP == 0 else 1
    blk = pl.BlockSpec((bb, ch, RWKV_WIDTH), lambda i, c: (i, c, 0))
    one = _const_spec((1, RWKV_WIDTH))
    gw = 4 * HEAD_DIM
    n_groups = RWKV_WIDTH // gw
    return pl.pallas_call(
        _rwkv_scan_kernel,
        grid=(b // bb, t // ch),
        in_specs=[blk] * 7 + [one] * 3,
        out_specs=[blk, pl.BlockSpec((bb, n_groups, gw, gw), lambda i, c: (i, 0, 0, 0))],
        out_shape=[jax.ShapeDtypeStruct((b, t, RWKV_WIDTH), F32),
                   jax.ShapeDtypeStruct((b, n_groups, gw, gw), F32)],
        scratch_shapes=[pltpu.VMEM((bb, n_groups, gw, gw), F32)],
        compiler_params=_params(("parallel", "arbitrary")),
        name="rwkv_scan",
    )(*feats, rk, lnw, lnb)


def _unblock_states(s_bd):
    hd = HEAD_DIM
    blocks = [s_bd[:, :, hd * h:hd * (h + 1), hd * h:hd * (h + 1)] for h in range(s_bd.shape[2] // hd)]
    return jnp.stack(blocks, axis=2).reshape(s_bd.shape[0], -1, hd, hd)


def _rwkv_step_kernel(s_ref, r_ref, lw_ref, k_ref, v_ref, a_ref, b_ref, g_ref, rk_ref, lnw_ref, lnb_ref,
                      o_ref, s_out_ref):
    hd = HEAD_DIM
    eye = _iota((hd, hd), 0) == _iota((hd, hd), 1)
    for h in range(RWKV_HEADS):
        s0 = s_ref[0, h]
        r, lw, k, v, a, b = (ref[0, h] for ref in (r_ref, lw_ref, k_ref, v_ref, a_ref, b_ref))
        v_col = jnp.sum(jnp.where(eye, v, 0.0), axis=-1, keepdims=True)
        sa = jnp.sum(s0 * a, axis=-1, keepdims=True)
        s1 = s0 * jnp.exp(lw) + sa * b + v_col * k
        s_out_ref[0, h] = s1
        y_col = jnp.sum(s1 * r, axis=-1, keepdims=True)
        y = jnp.sum(jnp.where(eye, y_col, 0.0), axis=0, keepdims=True)
        mu = jnp.mean(y, axis=-1, keepdims=True)
        d = y - mu
        var = jnp.mean(d * d, axis=-1, keepdims=True)
        yn = d * lax.rsqrt(var + GN_EPS) * lnw_ref[h] + lnb_ref[h]
        bonus = jnp.sum(r * k * rk_ref[h], axis=-1, keepdims=True) * v
        o_ref[0, h] = (yn + bonus) * g_ref[0, h]


def _rwkv_step(state, feats, rk, lnw, lnb):
    m = state.shape[0]
    hd, nh = HEAD_DIM, RWKV_HEADS
    heads = lambda x: x.reshape(x.shape[:-1] + (nh, 1, hd))
    vec = pl.BlockSpec((1, nh, 1, hd), lambda i: (i, 0, 0, 0))
    st = pl.BlockSpec((1, nh, hd, hd), lambda i: (i, 0, 0, 0))
    par = _const_spec((nh, 1, hd))
    o, s1 = pl.pallas_call(
        _rwkv_step_kernel,
        grid=(m,),
        in_specs=[st] + [vec] * 7 + [par] * 3,
        out_specs=[vec, st],
        out_shape=[jax.ShapeDtypeStruct((m, nh, 1, hd), F32), jax.ShapeDtypeStruct((m, nh, hd, hd), F32)],
        compiler_params=_params(("parallel",)),
        name="rwkv_step",
    )(state, *[heads(f) for f in feats], heads(rk)[0], heads(lnw)[0], heads(lnb)[0])
    return o.reshape(m, nh * hd), s1


def _layer_params(i, norm_mix, w_in, q_norm, k_norm, cmp_pe, cmp_w1, cmp_b1, cmp_w2, cmp_b2, shift_mu, w0, w2,
                  a0, a2, g2, k_k, k_a, r_k, ln_w, ln_b, w_out, norm_ffn, w_gate, w_up, w_down, norm_ple,
                  w_ple_gate, w_ple):
    pair = lambda g: jnp.concatenate([g, g])[None]
    return dict(
        norm_mix=norm_mix[i][None], w_in=_pad_w_in(w_in[i]),
        qn=pair(q_norm[i]), kn_sel=pair(k_norm[i, 1]), kn_win=pair(k_norm[i, 2]),
        cw=_compress_weights(cmp_pe[i], cmp_w1[i], cmp_b1[i], cmp_w2[i], cmp_b2[i], k_norm[i, 0]),
        rp=_rwkv_params(shift_mu[i], w0[i], w2[i], a0[i], a2[i], g2[i], k_k[i], k_a[i]),
        rk=r_k[i].reshape(1, RWKV_WIDTH), lnw=ln_w[i][None], lnb=ln_b[i][None],
        w_out=w_out[i].astype(BF16), norm_ffn=norm_ffn[i][None], w_gate=w_gate[i].astype(BF16),
        w_up=w_up[i].astype(BF16), w_down=w_down[i].astype(BF16), norm_ple=norm_ple[i][None],
        w_ple_gate=w_ple_gate[i].astype(BF16), w_ple=w_ple[i].astype(BF16))


def _prompt_layer(x, pe, lp, b, t):
    g, hd = NSA_KV_HEADS, HEAD_DIM
    q, kv, kw, gl, xr = _in_proj(x, lp["norm_mix"], lp["w_in"])
    kv = kv.reshape(b, t, KV_COLS)
    kw = kw.reshape(b, t, WIN_COLS)
    xr = xr.reshape(b, t, XR_PAD)
    kc = _compress_prompt(kv, lp["cw"])
    ksv = _kv_prep(kv, kw, lp["kn_sel"], lp["kn_win"])
    o_nsa = _nsa_prompt(q.reshape(b, t, Q_COLS), gl.reshape(b, t, LANES), lp["qn"], kc, ksv, t)
    feats = _rwkv_prep_seq(xr, jnp.zeros((b, 1, XR_PAD), F32), lp["rp"])
    o_rwkv, s_bd = _rwkv_scan(feats, lp["rk"], lp["lnw"], lp["lnb"])
    x = _out_ffn(x, o_nsa.reshape(b * t, Q_COLS), o_rwkv.reshape(b * t, RWKV_WIDTH), pe, lp)
    keep = min(WINDOW, t)
    return (x, kv.reshape(b, t, 4, g, hd), kw[:, t - keep:].reshape(b, keep, 2, g, hd), _unblock_states(s_bd),
            _unpad_xr(xr[:, -1]))


def _sample_layer(x, pe, lp, cache, pt_flat, n_pages, cache_win, state_wkv, state_shift):
    db = x.shape[0]
    g, hd = NSA_KV_HEADS, HEAD_DIM
    q, kv, kw, gl, xr = _in_proj(x, lp["norm_mix"], lp["w_in"])
    kcn, _, vcn, _ = _compress_paged(cache, pt_flat, db, n_pages, lp["cw"])
    q3 = q.reshape(db, NSA_HEADS, hd)
    z = jnp.zeros((db, NSA_HEADS // 2, hd), F32)
    qm = jnp.concatenate([jnp.concatenate([q3[:, :4], z], -1), jnp.concatenate([z, q3[:, 4:]], -1)], axis=1)
    glm = jnp.pad(gl[:, :GATE_COLS].reshape(db, NSA_HEADS, 3), ((0, 0), (0, 0), (0, LANES - 3)))
    wbuf = cache_win.shape[1]
    o8, win = _nsa_step(cache, pt_flat, qm, glm, kcn, vcn, kv[:, None, :], cache_win.reshape(db, wbuf, WIN_COLS),
                        kw[:, None, :], (lp["qn"], lp["kn_sel"], lp["kn_win"]), n_pages)
    o_nsa = o8.reshape(db, Q_COLS)
    feats = _rwkv_prep_step(xr, _pad_xr(state_shift), lp["rp"])
    o_rwkv, s1 = _rwkv_step(state_wkv, feats, lp["rk"], lp["lnw"], lp["lnb"])
    x = _out_ffn(x, o_nsa, o_rwkv, pe, lp)
    return (x, kv.reshape(db, 1, 4, g, hd), win.reshape(db, wbuf, 2, g, hd), s1, _unpad_xr(xr))


def kernel(x_prompt, x_sample, p_prompt, p_sample, cache_kv, cache_win, state_wkv, state_shift, page_table, norm_mix, w_in, q_norm, k_norm, cmp_pe, cmp_w1, cmp_b1, cmp_w2, cmp_b2, shift_mu, w0, w2, a0, a2, g2, k_k, k_a, r_k, ln_w, ln_b, w_out, norm_ffn, w_gate, w_up, w_down, norm_ple, w_ple_gate, w_ple):
    b, t, d = x_prompt.shape
    db = x_sample.shape[0]
    depth, n_phys, page = cache_kv.shape[:3]
    n_pages = page_table.shape[1]
    assert x_sample.shape[1] == 1 and d == D_MODEL
    cache = cache_kv.reshape(depth * n_phys, page, KV_COLS)
    pt_flat = page_table.reshape(-1).astype(jnp.int32)
    xp = x_prompt.reshape(b * t, d)
    xs = x_sample.reshape(db, d)
    outs_p, outs_s = [], []
    for i in range(depth):
        lp = _layer_params(i, norm_mix, w_in, q_norm, k_norm, cmp_pe, cmp_w1, cmp_b1, cmp_w2, cmp_b2, shift_mu,
                           w0, w2, a0, a2, g2, k_k, k_a, r_k, ln_w, ln_b, w_out, norm_ffn, w_gate, w_up, w_down,
                           norm_ple, w_ple_gate, w_ple)
        xp, *rest_p = _prompt_layer(xp, p_prompt[i].reshape(b * t, -1), lp, b, t)
        xs, *rest_s = _sample_layer(xs, p_sample[i].reshape(db, -1), lp, cache, pt_flat + i * n_phys, n_pages,
                                    cache_win[i], state_wkv[i], state_shift[i])
        outs_p.append(rest_p)
        outs_s.append(rest_s)
    stack = lambda outs, k: jnp.stack([o[k] for o in outs])
    return (xp.reshape(b, t, d), xs.reshape(db, 1, d),
            stack(outs_p, 0), stack(outs_s, 0), stack(outs_p, 1), stack(outs_s, 1),
            stack(outs_p, 2), stack(outs_s, 2), stack(outs_p, 3), stack(outs_s, 3))
```

```python
import functools

import jax
import jax.numpy as jnp
from jax import lax
from jax.experimental import pallas as pl
from jax.experimental.pallas import tpu as pltpu

F32 = jnp.float32
BF16 = jnp.bfloat16
HI = lax.Precision.HIGHEST

D_MODEL = 1024
HEAD_DIM = 64
LANES = 128
NSA_HEADS = 8
NSA_KV_HEADS = 2
RWKV_HEADS = 8
RWKV_WIDTH = RWKV_HEADS * HEAD_DIM
CMP_BLOCK = 32
CMP_STRIDE = 16
CMP_HIDDEN = 256
SEL_BLOCK = 64
SEL_TOPN = 16
WINDOW = 512
Q_BLOCK = 128
DECAY_RANK = 64
A_RANK = 64
GATE_RANK = 128
Q_COLS = NSA_HEADS * HEAD_DIM
KV_COLS = 4 * NSA_KV_HEADS * HEAD_DIM
WIN_COLS = 2 * NSA_KV_HEADS * HEAD_DIM
GATE_COLS = 3 * NSA_HEADS
RWKV_PROJ = 3 * RWKV_WIDTH + DECAY_RANK + A_RANK + GATE_RANK
XR_PAD = 3 * RWKV_WIDTH + 3 * LANES
IN_PAD = Q_COLS + KV_COLS + WIN_COLS + LANES + XR_PAD
NORM_EPS = 1e-6
GN_EPS = 64e-5
BIG = 1e9
NEG = -1e30
RWKV_CHUNK = 64
RWKV_SEQS_PER_STEP = 2
VMEM_LIMIT =56 * 1024 * 1024


def _dot(a, b, prec=None):
    return lax.dot_general(a, b, (((1,), (0,)), ((), ())), precision=prec, preferred_element_type=F32)


def _dot_nt(a, b, prec=None):
    return lax.dot_general(a, b, (((1,), (1,)), ((), ())), precision=prec, preferred_element_type=F32)


def _dot_tn(a, b, prec=None):
    return lax.dot_general(a, b, (((0,), (0,)), ((), ())), precision=prec, preferred_element_type=F32)


def _iota(shape, dim):
    return lax.broadcasted_iota(jnp.int32, shape, dim)


def _lo_mask(shape):
    return (_iota(shape, len(shape) - 1) % LANES) < HEAD_DIM


def _seg_sum(x):
    lo = _lo_mask(x.shape)
    s_lo = jnp.sum(jnp.where(lo, x, 0.0), axis=-1, keepdims=True)
    s_hi = jnp.sum(jnp.where(lo, 0.0, x), axis=-1, keepdims=True)
    return jnp.where(lo, s_lo, s_hi)


def _seg_rmsnorm(x, gain):
    ms = _seg_sum(x * x) * (1.0 / HEAD_DIM)
    return x * lax.rsqrt(ms + NORM_EPS) * gain


def _rmsnorm_rows(x, gain):
    return x * lax.rsqrt(jnp.mean(x * x, axis=-1, keepdims=True) + NORM_EPS) * gain


def _sigmoid(x):
    return 1.0 / (1.0 + jnp.exp(-x))


def _masked_softmax(s, mask):
    sm = jnp.where(mask, s, NEG)
    m = jnp.max(sm, axis=-1, keepdims=True)
    m = jnp.where(m > 0.5 * NEG, m, 0.0)
    e = jnp.where(mask, jnp.exp(sm - m), 0.0)
    return e * (1.0 / jnp.maximum(jnp.sum(e, axis=-1, keepdims=True), 1e-30))


def _softmax_av(s, bias, v):
    s = s + bias
    e = jnp.exp(s - jnp.max(s, axis=-1, keepdims=True))
    return _dot(e.astype(BF16), v) * (1.0 / jnp.sum(e, axis=-1, keepdims=True))


def _params(sem):
    return pltpu.CompilerParams(dimension_semantics=sem, vmem_limit_bytes=VMEM_LIMIT)


def _const_spec(shape):
    nd = len(shape)
    return pl.BlockSpec(shape, lambda *_: (0,) * nd)


_IN_SPLITS = (("q", 0, Q_COLS), ("kv", Q_COLS, KV_COLS), ("kw", Q_COLS + KV_COLS, WIN_COLS),
              ("gl", Q_COLS + KV_COLS + WIN_COLS, LANES),
              ("xr", Q_COLS + KV_COLS + WIN_COLS + LANES, XR_PAD))


def _in_proj_kernel(x_ref, g_ref, w_ref, *out_refs):
    h = _rmsnorm_rows(x_ref[...], g_ref[...]).astype(BF16)
    for o_ref, (_, lo, n) in zip(out_refs, _IN_SPLITS):
        o_ref[...] = _dot(h, w_ref[:, lo:lo + n])


def _in_proj(x, gain, w_pad):
    m = x.shape[0]
    tm = min(m, 512)
    assert m % tm == 0
    return pl.pallas_call(
        _in_proj_kernel,
        grid=(m // tm,),
        in_specs=[pl.BlockSpec((tm, D_MODEL), lambda i: (i, 0)),
                  _const_spec((1, D_MODEL)),
                  _const_spec((D_MODEL, IN_PAD))],
        out_specs=[pl.BlockSpec((tm, n), lambda i: (i, 0)) for _, _, n in _IN_SPLITS],
        out_shape=[jax.ShapeDtypeStruct((m, n), F32) for _, _, n in _IN_SPLITS],
        compiler_params=_params(("parallel",)),
        name="in_proj",
    )(x, gain, w_pad)


def _pad_xr(v):
    z = jnp.zeros(v.shape[:-1] + (LANES - DECAY_RANK,), v.dtype)
    c = 3 * RWKV_WIDTH
    return jnp.concatenate([v[..., :c + DECAY_RANK], z, v[..., c + DECAY_RANK:c + DECAY_RANK + A_RANK], z,
                            v[..., c + DECAY_RANK + A_RANK:]], axis=-1)


def _unpad_xr(v):
    c = 3 * RWKV_WIDTH
    return jnp.concatenate([v[..., :c + DECAY_RANK], v[..., c + LANES:c + LANES + A_RANK], v[..., c + 2 * LANES:]],
                           axis=-1)


def _pad_w_in(w_in):
    c = Q_COLS + KV_COLS + WIN_COLS
    gl = jnp.pad(w_in[:, c:c + GATE_COLS], ((0, 0), (0, LANES - GATE_COLS)))
    return jnp.concatenate([w_in[:, :c], gl, _pad_xr(w_in[:, c + GATE_COLS:])], axis=-1).astype(BF16)


def _ffn_kernel(x_ref, on_ref, or_ref, pe_ref, wout_ref, nf_ref, wg_ref, wu_ref, wd_ref, np_ref, wpg_ref,
                wple_ref, y_ref, x1_s, hf_s, acc_s):
    j = pl.program_id(1)

    @pl.when(j == 0)
    def _():
        half = D_MODEL // 2
        o = _dot(on_ref[...].astype(BF16), wout_ref[0:half, :]) + _dot(or_ref[...].astype(BF16), wout_ref[half:, :])
        x1 = x_ref[...] + o
        x1_s[...] = x1
        hf_s[...] = _rmsnorm_rows(x1, nf_ref[...]).astype(BF16)
        acc_s[...] = jnp.zeros_like(acc_s)

    hf = hf_s[...]
    g = _dot(hf, wg_ref[...])
    u = _dot(hf, wu_ref[...])
    act = (g * _sigmoid(g)) * u
    acc_s[...] += _dot(act.astype(BF16), wd_ref[...])

    @pl.when(j == pl.num_programs(1) - 1)
    def _():
        x2 = x1_s[...] + acc_s[...]
        hp = _rmsnorm_rows(x2, np_ref[...]).astype(BF16)
        pg = _sigmoid(_dot(hp, wpg_ref[...]))
        y_ref[...] = x2 + pg * _dot(pe_ref[...].astype(BF16), wple_ref[...])


def _out_ffn(x, o_nsa, o_rwkv, pe, lw):
    m = x.shape[0]
    tm = min(m, 512)
    hid = lw["w_gate"].shape[1]
    th = 1408
    assert m % tm == 0 and hid % th == 0
    ple = pe.shape[1]
    row = lambda i, j: (i, 0)
    return pl.pallas_call(
        _ffn_kernel,
        grid=(m // tm, hid // th),
        in_specs=[pl.BlockSpec((tm, D_MODEL), row),
                  pl.BlockSpec((tm, D_MODEL // 2), row),
                  pl.BlockSpec((tm, D_MODEL // 2), row),
                  pl.BlockSpec((tm, ple), row),
                  _const_spec((D_MODEL, D_MODEL)),
                  _const_spec((1, D_MODEL)),
                  pl.BlockSpec((D_MODEL, th), lambda i, j: (0, j)),
                  pl.BlockSpec((D_MODEL, th), lambda i, j: (0, j)),
                  pl.BlockSpec((th, D_MODEL), lambda i, j: (j, 0)),
                  _const_spec((1, D_MODEL)),
                  _const_spec((D_MODEL, D_MODEL)),
                  _const_spec((ple, D_MODEL))],
        out_specs=pl.BlockSpec((tm, D_MODEL), row),
        out_shape=jax.ShapeDtypeStruct((m, D_MODEL), F32),
        scratch_shapes=[pltpu.VMEM((tm, D_MODEL), F32), pltpu.VMEM((tm, D_MODEL), BF16),
                        pltpu.VMEM((tm, D_MODEL), F32)],
        compiler_params=_params(("parallel", "arbitrary")),
        name="out_ffn_ple",
    )(x, o_nsa, o_rwkv, pe, lw["w_out"], lw["norm_ffn"], lw["w_gate"], lw["w_up"], lw["w_down"],
      lw["norm_ple"], lw["w_ple_gate"], lw["w_ple"])


def _gelu_tanh(x):
    return 0.5 * x * (1.0 + jnp.tanh(0.7978845608028654 * (x + 0.044715 * x * x * x)))


def _compress_body(load_rows, nbp, first_step, pe_ref, wbd_ref, b1_ref, w2lo_ref, w2hi_ref, b2_ref, kn_ref,
                   kcn_ref, kcr_ref, vcn_ref, vcr_ref, bias_s):
    hid = CMP_HIDDEN

    @pl.when(first_step)
    def _():
        for ty in range(2):
            acc = jnp.zeros((1, hid), F32)
            for r in range(CMP_STRIDE):
                t = _dot(pe_ref[ty, r:r + 1, :].astype(BF16), wbd_ref[ty, r])
                acc = acc + t[:, 0:hid] + t[:, 3 * hid:4 * hid]
            bias_s[ty:ty + 1, :] = acc + b1_ref[ty:ty + 1, :]

    acc = [jnp.zeros((nbp, 4 * hid), F32) for _ in range(2)]
    for r in range(CMP_STRIDE):
        for ty in range(2):
            acc[ty] = acc[ty] + _dot(load_rows(r, ty).astype(BF16), wbd_ref[ty, r])
    outs = ((kcn_ref, kcr_ref), (vcn_ref, vcr_ref))
    for ty in range(2):
        hs = []
        for g in range(2):
            a = acc[ty][:, 2 * hid * g:2 * hid * g + hid]
            b = acc[ty][:, 2 * hid * g + hid:2 * hid * (g + 1)]
            b_next = pltpu.roll(b, nbp - 1, 0)
            hs.append(_gelu_tanh(a + b_next + bias_s[ty:ty + 1, :]).astype(BF16))
        o_n = _dot(hs[0], w2lo_ref[ty]) + _dot(hs[1], w2hi_ref[ty]) + b2_ref[ty:ty + 1, :]
        o_r = _dot(hs[0], w2hi_ref[ty]) + _dot(hs[1], w2lo_ref[ty]) + b2_ref[ty:ty + 1, :]
        if ty == 0:
            o_n = _seg_rmsnorm(o_n, kn_ref[...])
            o_r = _seg_rmsnorm(o_r, kn_ref[...])
        outs[ty][0][0] = o_n.astype(outs[ty][0].dtype)
        outs[ty][1][0] = o_r.astype(outs[ty][1].dtype)


def _compress_prompt_kernel(k_ref, v_ref, *refs, nbp):
    load = lambda r, ty: (k_ref, v_ref)[ty][0, pl.ds(r, nbp, stride=CMP_STRIDE), :]
    _compress_body(load, nbp, pl.program_id(0) == 0, *refs)


def _compress_paged_kernel(pt_ref, *refs, nbp, n_pages, page):
    del pt_ref
    page_refs, rest, rows_s = refs[:n_pages], refs[n_pages:-2], refs[-2:]
    for j in range(n_pages):
        for ty in range(2):
            tiles = page_refs[j][0, 2 * ty:2 * ty + 2].reshape(2 * HEAD_DIM, page)
            rows_s[ty][j * page:(j + 1) * page, :] = tiles.T
    load = lambda r, ty: rows_s[ty][pl.ds(r, nbp, stride=CMP_STRIDE), :]
    _compress_body(load, nbp, pl.program_id(0) == 0, *rest)


def _compress_weight_specs():
    hid = CMP_HIDDEN
    return [_const_spec((2, CMP_STRIDE, LANES)), _const_spec((2, CMP_STRIDE, LANES, 4 * hid)),
            _const_spec((2, hid)), _const_spec((2, hid, LANES)), _const_spec((2, hid, LANES)),
            _const_spec((2, LANES)), _const_spec((1, LANES))]


def _compress_outs(b, nbp):
    spec = pl.BlockSpec((1, nbp, LANES), lambda i, *_: (i, 0, 0))
    shapes = [jax.ShapeDtypeStruct((b, nbp, LANES), dt) for dt in (F32, F32, BF16, BF16)]
    return [spec] * 4, shapes


def _compress_prompt(kv, cw):
    b, t, _ = kv.shape
    nbp = t // CMP_STRIDE
    out_specs, out_shape = _compress_outs(b, nbp)
    return pl.pallas_call(
        functools.partial(_compress_prompt_kernel, nbp=nbp),
        grid=(b,),
        in_specs=[pl.BlockSpec((1, t, LANES), lambda i: (i, 0, 0)),
                  pl.BlockSpec((1, t, LANES), lambda i: (i, 0, 1))] + _compress_weight_specs(),
        out_specs=out_specs, out_shape=out_shape,
        scratch_shapes=[pltpu.VMEM((2, CMP_HIDDEN), F32)],
        compiler_params=_params(("arbitrary",)),
        name="compress_prompt",
    )(kv, kv, *cw)


def _compress_paged(cache, pt_flat, n_seq, n_pages, cw):
    page = cache.shape[3]
    assert page == LANES
    nbp = n_pages * page // CMP_STRIDE
    out_specs, out_shape = _compress_outs(n_seq, nbp)
    page_specs = [pl.BlockSpec((1, 4, HEAD_DIM, page), functools.partial(
        lambda i, pt, j: (pt[i * n_pages + j], 0, 0, 0), j=j)) for j in range(n_pages)]
    return pl.pallas_call(
        functools.partial(_compress_paged_kernel, nbp=nbp, n_pages=n_pages, page=page),
        grid_spec=pltpu.PrefetchScalarGridSpec(
            num_scalar_prefetch=1, grid=(n_seq,),
            in_specs=page_specs + _compress_weight_specs(),
            out_specs=out_specs,
            scratch_shapes=[pltpu.VMEM((2, CMP_HIDDEN), F32), pltpu.VMEM((n_pages * page, LANES), F32),
                            pltpu.VMEM((n_pages * page, LANES), F32)]),
        out_shape=out_shape,
        compiler_params=_params(("arbitrary",)),
        name="compress_paged",
    )(pt_flat, *([cache] * n_pages), *cw)


def _compress_weights(cmp_pe, cmp_w1, cmp_b1, cmp_w2, cmp_b2, k_norm0):
    hid = CMP_HIDDEN
    w1 = cmp_w1.reshape(2, CMP_BLOCK, HEAD_DIM, hid)
    wcat = jnp.concatenate([w1[:, :CMP_STRIDE], w1[:, CMP_STRIDE:]], axis=-1)
    z = jnp.zeros_like(wcat)
    wbd = jnp.concatenate([jnp.concatenate([wcat, z], -1), jnp.concatenate([z, wcat], -1)], axis=2).astype(BF16)
    pe_pair = jnp.concatenate([cmp_pe[:, :CMP_STRIDE], cmp_pe[:, CMP_STRIDE:]], axis=-1)
    z2 = jnp.zeros_like(cmp_w2)
    w2lo = jnp.concatenate([cmp_w2, z2], -1).astype(BF16)
    w2hi = jnp.concatenate([z2, cmp_w2], -1).astype(BF16)
    b2 = jnp.concatenate([cmp_b2, cmp_b2], -1)
    kn = jnp.concatenate([k_norm0, k_norm0])[None]
    return (pe_pair, wbd, cmp_b1, w2lo, w2hi, b2, kn)


def _kv_prep_kernel(kv_ref, kw_ref, kns_ref, knw_ref, ksn_ref, ksr_ref, vsn_ref, vsr_ref, kwn_ref, kwr_ref,
                    vwn_ref, vwr_ref):
    def put(x, n_ref, r_ref):
        n_ref[0] = x.astype(BF16)
        r_ref[0] = pltpu.roll(x, HEAD_DIM, 1).astype(BF16)

    put(_seg_rmsnorm(kv_ref[0, :, 2 * LANES:3 * LANES], kns_ref[...]), ksn_ref, ksr_ref)
    put(kv_ref[0, :, 3 * LANES:4 * LANES], vsn_ref, vsr_ref)
    put(_seg_rmsnorm(kw_ref[0, :, 0:LANES], knw_ref[...]), kwn_ref, kwr_ref)
    put(kw_ref[0, :, LANES:2 * LANES], vwn_ref, vwr_ref)


def _kv_prep(kv, kw, kn_sel, kn_win):
    b, t, _ = kv.shape
    tm = min(t, 512)
    spec = lambda n: pl.BlockSpec((1, tm, n), lambda i, j: (i, j, 0))
    return pl.pallas_call(
        _kv_prep_kernel,
        grid=(b, t // tm),
        in_specs=[spec(4 * LANES), spec(2 * LANES), _const_spec((1, LANES)), _const_spec((1, LANES))],
        out_specs=[spec(LANES)] * 8,
        out_shape=[jax.ShapeDtypeStruct((b, t, LANES), BF16)] * 8,
        compiler_params=_params(("parallel", "parallel")),
        name="kv_prep",
    )(kv, kw, kn_sel, kn_win)


def _overlap_t(ns, nbp):
    j = _iota((ns, nbp), 0) * SEL_BLOCK
    n = _iota((ns, nbp), 1) * CMP_STRIDE
    return jnp.where((n < j + SEL_BLOCK) & (n + CMP_BLOCK > j), 1.0, 0.0).astype(F32)


def _nsa_prompt_kernel(q_ref, gl_ref, qn_ref, kcn_ref, kcr_ref, vcn_ref, vcr_ref, ksn_ref, ksr_ref, vsn_ref,
                       vsr_ref, kwn_ref, kwr_ref, vwn_ref, vwr_ref, o_ref, *, seq, tk, wk):
    i = pl.program_id(1)
    qb = Q_BLOCK
    nbp = kcn_ref.shape[1]
    nb = nbp - 1
    ns = seq // SEL_BLOCK
    n_sel = min(SEL_TOPN, ns)
    lo = _lo_mask((qb, LANES))
    pos = i * qb + _iota((qb, 1), 0)
    sig = _sigmoid(gl_ref[0])

    qp = [_seg_rmsnorm(q_ref[0, :, LANES * p:LANES * (p + 1)], qn_ref[...]) * (HEAD_DIM ** -0.5) for p in range(4)]
    q_lo = [jnp.where(lo, x, 0.0) for x in qp]
    q_hi = [jnp.where(lo, 0.0, x) for x in qp]

    n_c = _iota((qb, nbp), 1)
    mask_c = (n_c * CMP_STRIDE + (CMP_BLOCK - 1) <= pos) & (n_c < nb)
    ov_t = _overlap_t(ns, nbp)
    j_t = _iota((ns, qb), 0)
    pos_t = i * qb + _iota((ns, qb), 1)
    cur_t = pos_t >> 6
    forced_t = (j_t == 0) | (j_t == cur_t) | (j_t == cur_t - 1)
    causal_t = j_t * SEL_BLOCK <= pos_t
    eye = jnp.where(_iota((qb, qb), 0) == _iota((qb, qb), 1), 1.0, 0.0).astype(BF16)

    w_start = pl.multiple_of(jnp.maximum(i * qb - (wk - qb), 0), qb)
    w_pos = w_start + _iota((qb, wk), 1)
    w_dist = pos - w_pos
    mask_w = (w_dist >= 0) & (w_dist <= WINDOW)
    n_tiles = (i * qb + qb - 1) // tk + 1

    order = lambda g, n, r: (n, r) if g == 0 else (r, n)
    heads = []
    for p in range(4):
        heads += [(p, p // 2, q_lo[p], 0), (p, p // 2, q_hi[p], 1)]

    split = lambda x: (x.astype(BF16), (x - x.astype(BF16).astype(F32)).astype(BF16))
    kc_parts = [split(kcn_ref[0]), split(kcr_ref[0])]

    def score3(qt, kc):
        q_h, q_l = split(qt)
        return _dot_nt(q_h, kc[0]) + (_dot_nt(q_h, kc[1]) + _dot_nt(q_l, kc[0]))

    p_cmp = [_masked_softmax(score3(qt, order(g, *kc_parts)[w]), mask_c) for _, g, qt, w in heads]
    o_cmp = [_dot(pc.astype(BF16), order(g, vcn_ref, vcr_ref)[w][0]) for pc, (_, g, _, w) in zip(p_cmp, heads)]
    unsel = []
    for g in range(2):
        psum = p_cmp[4 * g] + p_cmp[4 * g + 1] + p_cmp[4 * g + 2] + p_cmp[4 * g + 3]
        imp_t = _dot_nt(ov_t, psum, HI)
        score = jnp.where(forced_t, BIG, jnp.where(causal_t, imp_t, -BIG))
        rank = jnp.zeros((ns, qb), jnp.int32)
        for jp in range(ns):
            row = score[jp:jp + 1, :]
            rank = rank + jnp.where((row > score) | ((row == score) & (j_t > jp)), 1, 0)
        unsel.append(_dot_nt(eye, jnp.where(rank < n_sel, 0.0, NEG).astype(BF16)).astype(BF16))

    q16 = [qt.astype(BF16) for _, _, qt, _ in heads]

    def tile(kt, carry, diagonal):
        k0 = pl.multiple_of(kt * tk, tk)
        blk = (k0 + _iota((ns, tk), 1)) >> 6
        expand = jnp.where(_iota((ns, tk), 0) == blk, 1.0, 0.0).astype(BF16)
        bias = [_dot(u, expand) for u in unsel]
        if diagonal:
            causal = k0 + _iota((qb, tk), 1) <= pos
            bias = [jnp.where(causal, x, NEG) for x in bias]
        out = []
        for (m, l, acc), q_, (_, g, _, w) in zip(carry, q16, heads):
            k_ref = order(g, ksn_ref, ksr_ref)[w]
            v_ref = order(g, vsn_ref, vsr_ref)[w]
            s = _dot_nt(q_, k_ref[0, pl.ds(k0, tk), :]) + bias[g]
            m_new = jnp.maximum(m, jnp.max(s, axis=-1, keepdims=True))
            alpha = jnp.exp(m - m_new)
            e = jnp.exp(s - m_new)
            l_new = alpha * l + jnp.sum(e, axis=-1, keepdims=True)
            acc_new = alpha * acc + _dot(e.astype(BF16), v_ref[0, pl.ds(k0, tk), :])
            out.append((m_new, l_new, acc_new))
        return tuple(out)

    init = (jnp.full((qb, 1), NEG, F32), jnp.zeros((qb, 1), F32), jnp.zeros((qb, LANES), F32))
    carry = lax.fori_loop(0, n_tiles - 1, functools.partial(tile, diagonal=False), (init,) * len(heads))
    o_sel = [acc / l for _, l, acc in tile(n_tiles - 1, carry, True)]

    bias_w = jnp.where(mask_w, 0.0, NEG)
    o_win = [_softmax_av(_dot_nt(q_, order(g, kwn_ref, kwr_ref)[w][0, pl.ds(w_start, wk), :]), bias_w,
                         order(g, vwn_ref, vwr_ref)[w][0, pl.ds(w_start, wk), :])
             for q_, (_, g, _, w) in zip(q16, heads)]

    for p in range(4):
        gate = lambda c: jnp.where(lo, sig[:, 6 * p + c:6 * p + c + 1], sig[:, 6 * p + 3 + c:6 * p + 4 + c])
        both = lambda o: jnp.where(lo, o[2 * p], o[2 * p + 1])
        o_ref[0, :, LANES * p:LANES * (p + 1)] = gate(0) * both(o_cmp) + gate(1) * both(o_sel) + gate(2) * both(o_win)


def _nsa_prompt(q, gl, qn, kc, ksv, seq):
    b = q.shape[0]
    nbp = kc[0].shape[1]
    tk = min(seq, 256)
    wk = min(seq, WINDOW + Q_BLOCK)
    assert seq % Q_BLOCK == 0 and seq % tk == 0
    blk = lambda n: pl.BlockSpec((1, Q_BLOCK, n), lambda bi, i: (bi, i, 0))
    full = lambda rows: pl.BlockSpec((1, rows, LANES), lambda bi, i: (bi, 0, 0))
    return pl.pallas_call(
        functools.partial(_nsa_prompt_kernel, seq=seq, tk=tk, wk=wk),
        grid=(b, seq // Q_BLOCK),
        in_specs=[blk(4 * LANES), blk(LANES), _const_spec((1, LANES))] + [full(nbp)] * 4 + [full(seq)] * 8,
        out_specs=blk(4 * LANES),
        out_shape=jax.ShapeDtypeStruct((b, seq, 4 * LANES), F32),
        compiler_params=_params(("parallel", "arbitrary")),
        name="nsa_prompt",
    )(q, gl, qn, *kc, *ksv)


def _nsa_step_kernel(pt_ref, *refs, n_pages, page, past):
    del pt_ref
    page_refs = refs[:n_pages]
    (qm_ref, gl_ref, kcn_ref, vcn_ref, kvn_ref, cw_ref, kwn_ref, qn_ref, kns_ref, knw_ref, kns_row_ref, knw_row_ref,
     o_ref, win_ref) = refs[n_pages:]
    nh = NSA_HEADS
    hd = HEAD_DIM
    nbp = kcn_ref.shape[1]
    nb = nbp - 1
    ns = past // SEL_BLOCK + 1
    n_sel = min(SEL_TOPN, ns)
    wbuf = cw_ref.shape[3]
    rep = NSA_HEADS // NSA_KV_HEADS
    first_group = _iota((nh, 1), 0) < rep

    def unpair(x):
        return jnp.where(first_group, x, pltpu.roll(x, hd, 1))[:, 0:hd]

    def by_group(f):
        return jnp.where(first_group, f(0), f(1))

    def norm_t(t, gain_ref):
        return (t * lax.rsqrt(jnp.mean(t * t, axis=0, keepdims=True) + NORM_EPS) * gain_ref[...]).astype(BF16)

    qn = _seg_rmsnorm(qm_ref[0], qn_ref[...]) * (HEAD_DIM ** -0.5)
    q8 = unpair(qn)
    q8b = q8.astype(BF16)
    sg = _sigmoid(gl_ref[0])

    n_c = _iota((nh, nbp), 1)
    mask_c = (n_c * CMP_STRIDE + (CMP_BLOCK - 1) <= past) & (n_c < nb)
    p_c = _masked_softmax(_dot_nt(qn, kcn_ref[0], HI), mask_c)
    o_cmp = unpair(_dot(p_c.astype(BF16), vcn_ref[0]))
    same_group = jnp.where(_iota((nh, nh), 0) // rep == _iota((nh, nh), 1) // rep, 1.0, 0.0).astype(F32)
    psum = _dot(same_group, p_c, HI)
    imp = _dot_nt(psum, _overlap_t(LANES, nbp), HI)
    j = _iota((nh, LANES), 1)
    cur = past // SEL_BLOCK
    forced = (j == 0) | (j == cur) | (j == cur - 1)
    score = jnp.where(forced, BIG, jnp.where(j * SEL_BLOCK <= past, imp, -BIG))
    score = jnp.where(j < ns, score, -4.0 * BIG)
    rank = jnp.zeros((nh, LANES), jnp.int32)
    for jp in range(ns):
        col = score[:, jp:jp + 1]
        rank = rank + jnp.where((col > score) | ((col == score) & (j > jp)), 1, 0)
    sel = jnp.where((rank < n_sel) & (j < ns), 1.0, 0.0).astype(BF16)

    kv_new = kvn_ref[0]
    k_new = by_group(lambda g: _rmsnorm_rows(kv_new[4 + g:5 + g, :], kns_row_ref[...]))
    s_new = jnp.sum(q8 * k_new, axis=-1, keepdims=True)
    s_parts, m_parts = [], []
    for pj in range(n_pages):
        blk = (pj * page + _iota((LANES, page), 1)) >> 6
        expand = jnp.where(_iota((LANES, page), 0) == blk, 1.0, 0.0).astype(BF16)
        m_parts.append(_dot(sel, expand) > 0.5)
        s_parts.append(by_group(lambda g: _dot(q8b, norm_t(page_refs[pj][0, g], kns_ref))))
    s_all = jnp.concatenate(s_parts, axis=-1)
    valid = jnp.concatenate(m_parts, axis=-1)
    s_all = jnp.where(valid, s_all, NEG)
    m = jnp.maximum(jnp.max(s_all, axis=-1, keepdims=True), s_new)
    e_all = jnp.where(valid, jnp.exp(s_all - m), 0.0)
    e_new = jnp.exp(s_new - m)
    acc = e_new * by_group(lambda g: kv_new[6 + g:7 + g, :])
    for pj in range(n_pages):
        e_pj = e_all[:, pj * page:(pj + 1) * page].astype(BF16)
        acc = acc + by_group(lambda g: _dot_nt(e_pj, page_refs[pj][0, 2 + g].astype(BF16)))
    o_sel = acc / (jnp.sum(e_all, axis=-1, keepdims=True) + e_new)

    kw_new = kwn_ref[0]
    s_w = by_group(lambda g: _dot(q8b, norm_t(cw_ref[0, g], knw_ref)))
    s_wn = jnp.sum(q8 * by_group(lambda g: _rmsnorm_rows(kw_new[g:g + 1, :], knw_row_ref[...])), axis=-1, keepdims=True)
    m_w = jnp.maximum(jnp.max(s_w, axis=-1, keepdims=True), s_wn)
    e_w = jnp.exp(s_w - m_w)
    e_wn = jnp.exp(s_wn - m_w)
    e_wb = e_w.astype(BF16)
    o_win = (by_group(lambda g: _dot_nt(e_wb, cw_ref[0, 2 + g].astype(BF16))) + e_wn * by_group(
        lambda g: kw_new[2 + g:3 + g, :])) / (jnp.sum(e_w, axis=-1, keepdims=True) + e_wn)

    o_ref[0] = sg[:, 0:1] * o_cmp + sg[:, 1:2] * o_sel + sg[:, 2:3] * o_win

    eye = _iota((hd, hd), 0) == _iota((hd, hd), 1)
    last = _iota((hd, wbuf), 1) == wbuf - 1
    for c in range(4):
        col = jnp.sum(jnp.where(eye, kw_new[c:c + 1, :], 0.0), axis=1, keepdims=True)
        win_ref[0, c] = jnp.where(last, col, pltpu.roll(cw_ref[0, c], wbuf - 1, 1))


def _nsa_step(cache, pt_flat, qm, glm, kcn, vcn, kv_new, cache_win, win_off, kw_new, gains, n_pages):
    db = qm.shape[0]
    page = cache.shape[3]
    past = n_pages * page
    nbp = kcn.shape[1]
    wbuf = cache_win.shape[3]
    hd = HEAD_DIM
    assert wbuf == min(WINDOW, past) and page == LANES and past // SEL_BLOCK + 1 <= LANES
    per = lambda shape: pl.BlockSpec((1,) + shape, lambda i, pt: (i,) + (0,) * len(shape))
    page_specs = [pl.BlockSpec((1, 4, hd, page), functools.partial(
        lambda i, pt, j: (pt[i * n_pages + j], 1, 0, 0), j=j)) for j in range(n_pages)]
    return pl.pallas_call(
        functools.partial(_nsa_step_kernel, n_pages=n_pages, page=page, past=past),
        grid_spec=pltpu.PrefetchScalarGridSpec(
            num_scalar_prefetch=1, grid=(db,),
            in_specs=page_specs + [per((NSA_HEADS, LANES)), per((NSA_HEADS, LANES)), per((nbp, LANES)),
                                   per((nbp, LANES)), per((8, hd)),
                                   pl.BlockSpec((1, 4, hd, wbuf), lambda i, pt: (i + win_off, 0, 0, 0)), per((4, hd))]
            + [_const_spec((1, LANES)), _const_spec((hd, 1)), _const_spec((hd, 1)), _const_spec((1, hd)),
               _const_spec((1, hd))],
            out_specs=[per((NSA_HEADS, hd)), per((4, hd, wbuf))]),
        out_shape=[jax.ShapeDtypeStruct((db, NSA_HEADS, hd), F32),
                   jax.ShapeDtypeStruct((db, 4, hd, wbuf), F32)],
        compiler_params=_params(("parallel",)),
        name="nsa_step",
    )(pt_flat, *([cache] * n_pages), qm, glm, kcn, vcn, kv_new, cache_win, kw_new, *gains)


def _softplus(z):
    return jnp.maximum(z, 0.0) + jnp.log(1.0 + jnp.exp(-jnp.abs(z)))


def _rwkv_features(x, prev, mu_ref, w0_ref, a0_ref, w2_ref, a2_ref, g2_ref, kk_ref, ka_ref, outs):
    r_ref, lw_ref, k_ref, v_ref, a_ref, b_ref, g_ref = outs
    rw = RWKV_WIDTH
    xs = x + (prev - x) * mu_ref[...]
    r, k, v = xs[:, 0:rw], xs[:, rw:2 * rw], xs[:, 2 * rw:3 * rw]
    wd = xs[:, 3 * rw:3 * rw + LANES]
    ad = xs[:, 3 * rw + LANES:3 * rw + 2 * LANES]
    gd = xs[:, 3 * rw + 2 * LANES:3 * rw + 3 * LANES]
    w_log = -_softplus(-(w0_ref[...] + _dot(jnp.tanh(wd).astype(BF16), w2_ref[...]))) - 0.5
    a_sig = _sigmoid(a0_ref[...] + _dot(ad.astype(BF16), a2_ref[...]))
    kk = k * kk_ref[...]
    kk2 = kk * kk
    norm2 = jnp.concatenate([_seg_sum(kk2[:, LANES * p:LANES * (p + 1)]) for p in range(rw // LANES)], axis=-1)
    kk = kk / jnp.maximum(jnp.sqrt(norm2), 1e-12)
    r_ref[...] = r.reshape(r_ref.shape)
    lw_ref[...] = (-jnp.exp(w_log)).reshape(lw_ref.shape)
    k_ref[...] = (k * (1.0 + (a_sig - 1.0) * ka_ref[...])).reshape(k_ref.shape)
    v_ref[...] = v.reshape(v_ref.shape)
    a_ref[...] = (-kk).reshape(a_ref.shape)
    b_ref[...] = (kk * a_sig).reshape(b_ref.shape)
    g_ref[...] = _dot(_sigmoid(gd).astype(BF16), g2_ref[...]).reshape(g_ref.shape)


def _rwkv_prep_seq_kernel(xr_ref, shift_ref, *refs):
    params, outs, last_s = refs[:8], refs[8:15], refs[15]
    t = pl.program_id(1)

    @pl.when(t == 0)
    def _():
        last_s[...] = shift_ref[0]

    x = xr_ref[0]
    tm = x.shape[0]
    prev = jnp.where(_iota(x.shape, 0) == 0, last_s[...], pltpu.roll(x, 1, 0))
    last_s[...] = x[tm - 1:tm, :]
    _rwkv_features(x, prev, *params, outs)


def _rwkv_prep_step_kernel(xr_ref, shift_ref, *refs):
    _rwkv_features(xr_ref[...], shift_ref[...], *refs[:8], refs[8:15])


def _rwkv_param_specs():
    one = _const_spec((1, RWKV_WIDTH))
    low = _const_spec((LANES, RWKV_WIDTH))
    return [_const_spec((1, XR_PAD)), one, one, low, low, low, one, one]


def _rwkv_prep_seq(xr, shift0, rp):
    b, t, _ = xr.shape
    tm = min(t, 512)
    out = pl.BlockSpec((1, tm, RWKV_WIDTH), lambda i, j: (i, j, 0))
    return pl.pallas_call(
        _rwkv_prep_seq_kernel,
        grid=(b, t // tm),
        in_specs=[pl.BlockSpec((1, tm, XR_PAD), lambda i, j: (i, j, 0)),
                  pl.BlockSpec((1, 1, XR_PAD), lambda i, j: (i, 0, 0))] + _rwkv_param_specs(),
        out_specs=[out] * 7,
        out_shape=[jax.ShapeDtypeStruct((b, t, RWKV_WIDTH), F32)] * 7,
        scratch_shapes=[pltpu.VMEM((1, XR_PAD), F32)],
        compiler_params=_params(("parallel", "arbitrary")),
        name="rwkv_prep_seq",
    )(xr, shift0, *rp)


def _rwkv_prep_step(xr, shift0, rp):
    m = xr.shape[0]
    return pl.pallas_call(
        _rwkv_prep_step_kernel,
        grid=(1,),
        in_specs=[_const_spec((m, XR_PAD)), _const_spec((m, XR_PAD))] + _rwkv_param_specs(),
        out_specs=[_const_spec((m, RWKV_WIDTH))] * 7,
        out_shape=[jax.ShapeDtypeStruct((m, RWKV_WIDTH), F32)] * 7,
        compiler_params=_params(("arbitrary",)),
        name="rwkv_prep_step",
    )(xr, shift0, *rp)


def _rwkv_params(shift_mu, w0, w2, a0, a2, g2, k_k, k_a):
    pad = lambda w: jnp.pad(w, ((0, LANES - w.shape[0]), (0, 0))).astype(BF16)
    return (_pad_xr(shift_mu)[None], w0[None], a0[None], pad(w2), pad(a2), g2.astype(BF16), k_k[None], k_a[None])


def _rwkv_finish(y, r, k, v, gate, rk, lnw, lnb):
    mu = _seg_sum(y) * (1.0 / HEAD_DIM)
    d = y - mu
    var = _seg_sum(d * d) * (1.0 / HEAD_DIM)
    yn = d * lax.rsqrt(var + GN_EPS) * lnw + lnb
    return (yn + _seg_sum(r * k * rk) * v) * gate


def _rwkv_scan_kernel(r_ref, lw_ref, k_ref, v_ref, a_ref, b_ref, g_ref, rk_ref, lnw_ref, lnb_ref,
                      o_ref, s_out_ref, s_s):
    c = pl.program_id(1)
    ch = r_ref.shape[1]
    gw = 4 * HEAD_DIM
    rows = 4 * ch

    @pl.when(c == 0)
    def _():
        s_s[...] = jnp.zeros_like(s_s)

    tril = jnp.where(_iota((ch, ch), 0) >= _iota((ch, ch), 1), 1.0, 0.0).astype(F32)
    t_row = _iota((rows, rows), 0) % ch
    t_col = _iota((rows, rows), 1) % ch
    strict = t_row > t_col
    incl = t_row >= t_col
    eye = jnp.where(_iota((rows, rows), 0) == _iota((rows, rows), 1), 1.0, 0.0).astype(F32)
    head_of_lane = _iota((ch, gw), 1) // HEAD_DIM

    def wide(x):
        return jnp.concatenate([jnp.where(head_of_lane == h, x, 0.0) for h in range(4)], axis=0).astype(BF16)

    units = [(i, q) for i in range(r_ref.shape[0]) for q in range(RWKV_WIDTH // gw)]
    each = lambda f, *cols: [f(*xs) for xs in zip(*cols)]
    load = lambda ref: [ref[i, :, gw * q:gw * (q + 1)] for i, q in units]
    r, lw, k, v, a, b = (load(ref) for ref in (r_ref, lw_ref, k_ref, v_ref, a_ref, b_ref))
    s0 = [s_s[i, q] for i, q in units]
    cl = each(lambda x: _dot(tril, x, HI), lw)
    g_in = each(jnp.exp, cl)
    g_inv = each(lambda x: jnp.exp(-x), cl)
    w_ar = each(lambda a_, r_, cl_, lw_, gi: jnp.concatenate([wide(a_ * jnp.exp(cl_ - lw_)), wide(r_ * gi)], axis=0),
                a, r, cl, lw, g_in)
    w_bk = each(lambda b_, k_, gv: jnp.concatenate([wide(b_ * gv), wide(k_ * gv)], axis=0), b, k, g_inv)
    w_v = each(wide, v)
    p_all = each(_dot_nt, w_ar, w_bk)
    xy0 = each(lambda x, s: _dot_nt(x, s.astype(BF16)), w_ar, s0)
    n = each(lambda p: jnp.where(strict, p[:rows, :rows], 0.0), p_all)
    l_ak = each(lambda p: jnp.where(strict, p[:rows, rows:], 0.0).astype(BF16), p_all)
    m_rb = each(lambda p: jnp.where(incl, p[rows:, :rows], 0.0).astype(BF16), p_all)
    m_rk = each(lambda p: jnp.where(incl, p[rows:, rows:], 0.0).astype(BF16), p_all)
    t_inv = each(lambda x: eye + x, n)
    pw = each(lambda x: x.astype(BF16), n)
    for _ in range(max(ch.bit_length() - 2, 0)):
        pw = each(lambda x: _dot(x, x).astype(BF16), pw)
        t_inv = each(lambda t, x: t + _dot(t.astype(BF16), x), t_inv, pw)
    x_w = each(lambda xy, l, vv: (xy[:rows] + _dot(l, vv)).astype(BF16), xy0, l_ak, w_v)
    u = each(lambda t, x: _dot(t.astype(BF16), x).astype(BF16), t_inv, x_w)
    y_w = each(lambda xy, mb, uu, mk, vv: xy[rows:] + _dot(mb, uu) + _dot(mk, vv), xy0, m_rb, u, m_rk, w_v)
    upd = each(lambda uu, vv, bk: _dot_tn(jnp.concatenate([uu, vv], axis=0), bk), u, w_v, w_bk)
    for j, (i, q) in enumerate(units):
        s_s[i, q] = (s0[j] + upd[j]) * g_in[j][ch - 1:ch, :]
        y = y_w[j][0:ch] + y_w[j][ch:2 * ch] + y_w[j][2 * ch:3 * ch] + y_w[j][3 * ch:4 * ch]
        for p in range(gw // LANES):
            ps = slice(LANES * p, LANES * (p + 1))
            po = slice(gw * q + LANES * p, gw * q + LANES * (p + 1))
            o_ref[i, :, po] = _rwkv_finish(y[:, ps], r[j][:, ps], k[j][:, ps], v[j][:, ps], g_ref[i, :, po],
                                           rk_ref[:, po], lnw_ref[:, po], lnb_ref[:, po])

    @pl.when(c == pl.num_programs(1) - 1)
    def _():
        s_out_ref[...] = s_s[...]


def _rwkv_scan(feats, rk, lnw, lnb):
    b, t, _ = feats[0].shape
    ch = min(t, RWKV_CHUNK)
    assert t % ch == 0 and ch & (ch - 1) == 0
    bb = RWKV_SEQS_PER_STEP if b % RWKV_SEQS_PER_STEP == 0 else 1
    blk = pl.BlockSpec((bb, ch, RWKV_WIDTH), lambda i, c: (i, c, 0))
    one = _const_spec((1, RWKV_WIDTH))
    gw = 4 * HEAD_DIM
    n_groups = RWKV_WIDTH // gw
    return pl.pallas_call(
        _rwkv_scan_kernel,
        grid=(b // bb, t // ch),
        in_specs=[blk] * 7 + [one] * 3,
        out_specs=[blk, pl.BlockSpec((bb, n_groups, gw, gw), lambda i, c: (i, 0, 0, 0))],
        out_shape=[jax.ShapeDtypeStruct((b, t, RWKV_WIDTH), F32),
                   jax.ShapeDtypeStruct((b, n_groups, gw, gw), F32)],
        scratch_shapes=[pltpu.VMEM((bb, n_groups, gw, gw), F32)],
        compiler_params=_params(("parallel", "arbitrary")),
        name="rwkv_scan",
    )(*feats, rk, lnw, lnb)


def _unblock_states(s_bd):
    hd = HEAD_DIM
    blocks = [s_bd[:, :, hd * h:hd * (h + 1), hd * h:hd * (h + 1)] for h in range(s_bd.shape[2] // hd)]
    return jnp.stack(blocks, axis=2).reshape(s_bd.shape[0], -1, hd, hd)


def _rwkv_step_kernel(s_ref, r_ref, lw_ref, k_ref, v_ref, a_ref, b_ref, g_ref, rk_ref, lnw_ref, lnb_ref,
                      o_ref, s_out_ref):
    hd = HEAD_DIM
    eye = _iota((hd, hd), 0) == _iota((hd, hd), 1)
    for h in range(RWKV_HEADS):
        for i in range(s_ref.shape[0]):
            s0 = s_ref[i, h]
            r, lw, k, v, a, b = (ref[i, h] for ref in (r_ref, lw_ref, k_ref, v_ref, a_ref, b_ref))
            v_col = jnp.sum(jnp.where(eye, v, 0.0), axis=-1, keepdims=True)
            sa = jnp.sum(s0 * a, axis=-1, keepdims=True)
            s1 = s0 * jnp.exp(lw) + sa * b + v_col * k
            s_out_ref[i, h] = s1
            y_col = jnp.sum(s1 * r, axis=-1, keepdims=True)
            y = jnp.sum(jnp.where(eye, y_col, 0.0), axis=0, keepdims=True)
            mu = jnp.mean(y, axis=-1, keepdims=True)
            d = y - mu
            var = jnp.mean(d * d, axis=-1, keepdims=True)
            yn = d * lax.rsqrt(var + GN_EPS) * lnw_ref[h] + lnb_ref[h]
            bonus = jnp.sum(r * k * rk_ref[h], axis=-1, keepdims=True) * v
            o_ref[i, h] = (yn + bonus) * g_ref[i, h]


def _rwkv_step(state, feats, rk, lnw, lnb):
    m = state.shape[0]
    hd, nh = HEAD_DIM, RWKV_HEADS
    heads = lambda x: x.reshape(x.shape[:-1] + (nh, 1, hd))
    vec = pl.BlockSpec((1, nh, 1, hd), lambda i: (i, 0, 0, 0))
    st = pl.BlockSpec((1, nh, hd, hd), lambda i: (i, 0, 0, 0))
    par = _const_spec((nh, 1, hd))
    o, s1 = pl.pallas_call(
        _rwkv_step_kernel,
        grid=(m,),
        in_specs=[st] + [vec] * 7 + [par] * 3,
        out_specs=[vec, st],
        out_shape=[jax.ShapeDtypeStruct((m, nh, 1, hd), F32), jax.ShapeDtypeStruct((m, nh, hd, hd), F32)],
        compiler_params=_params(("parallel",)),
        name="rwkv_step",
    )(state, *[heads(f) for f in feats], heads(rk)[0], heads(lnw)[0], heads(lnb)[0])
    return o.reshape(m, nh * hd), s1


def _layer_params(i, norm_mix, w_in, q_norm, k_norm, cmp_pe, cmp_w1, cmp_b1, cmp_w2, cmp_b2, shift_mu, w0, w2,
                  a0, a2, g2, k_k, k_a, r_k, ln_w, ln_b, w_out, norm_ffn, w_gate, w_up, w_down, norm_ple,
                  w_ple_gate, w_ple):
    pair = lambda g: jnp.concatenate([g, g])[None]
    return dict(
        norm_mix=norm_mix[i][None], w_in=_pad_w_in(w_in[i]),
        qn=pair(q_norm[i]), kn_sel=pair(k_norm[i, 1]), kn_win=pair(k_norm[i, 2]),
        step_gains=(pair(q_norm[i]), k_norm[i, 1][:, None], k_norm[i, 2][:, None], k_norm[i, 1][None],
                    k_norm[i, 2][None]),
        cw=_compress_weights(cmp_pe[i], cmp_w1[i], cmp_b1[i], cmp_w2[i], cmp_b2[i], k_norm[i, 0]),
        rp=_rwkv_params(shift_mu[i], w0[i], w2[i], a0[i], a2[i], g2[i], k_k[i], k_a[i]),
        rk=r_k[i].reshape(1, RWKV_WIDTH), lnw=ln_w[i][None], lnb=ln_b[i][None],
        w_out=w_out[i].astype(BF16), norm_ffn=norm_ffn[i][None], w_gate=w_gate[i].astype(BF16),
        w_up=w_up[i].astype(BF16), w_down=w_down[i].astype(BF16), norm_ple=norm_ple[i][None],
        w_ple_gate=w_ple_gate[i].astype(BF16), w_ple=w_ple[i].astype(BF16))


def _prompt_layer(x, pe, lp, b, t):
    g, hd = NSA_KV_HEADS, HEAD_DIM
    q, kv, kw, gl, xr = _in_proj(x, lp["norm_mix"], lp["w_in"])
    kv = kv.reshape(b, t, KV_COLS)
    kw = kw.reshape(b, t, WIN_COLS)
    xr = xr.reshape(b, t, XR_PAD)
    kc = _compress_prompt(kv, lp["cw"])
    ksv = _kv_prep(kv, kw, lp["kn_sel"], lp["kn_win"])
    o_nsa = _nsa_prompt(q.reshape(b, t, Q_COLS), gl.reshape(b, t, LANES), lp["qn"], kc, ksv, t)
    feats = _rwkv_prep_seq(xr, jnp.zeros((b, 1, XR_PAD), F32), lp["rp"])
    o_rwkv, s_bd = _rwkv_scan(feats, lp["rk"], lp["lnw"], lp["lnb"])
    x = _out_ffn(x, o_nsa.reshape(b * t, Q_COLS), o_rwkv.reshape(b * t, RWKV_WIDTH), pe, lp)
    keep = min(WINDOW, t)
    return (x, kv.reshape(b, t, 4, g, hd), kw[:, t - keep:].reshape(b, keep, 2, g, hd), _unblock_states(s_bd),
            _unpad_xr(xr[:, -1]))


def _sample_layer(x, pe, lp, cache_t, pt_flat, n_pages, cwin_t, win_off, state_wkv, state_shift):
    db = x.shape[0]
    g, hd = NSA_KV_HEADS, HEAD_DIM
    q, kv, kw, gl, xr = _in_proj(x, lp["norm_mix"], lp["w_in"])
    kcn, _, vcn, _ = _compress_paged(cache_t, pt_flat, db, n_pages, lp["cw"])
    q3 = q.reshape(db, NSA_HEADS, hd)
    z = jnp.zeros((db, NSA_HEADS // 2, hd), F32)
    qm = jnp.concatenate([jnp.concatenate([q3[:, :4], z], -1), jnp.concatenate([z, q3[:, 4:]], -1)], axis=1)
    glm = jnp.pad(gl[:, :GATE_COLS].reshape(db, NSA_HEADS, 3), ((0, 0), (0, 0), (0, LANES - 3)))
    o8, win = _nsa_step(cache_t, pt_flat, qm, glm, kcn, vcn, kv.reshape(db, 4 * g, hd), cwin_t, win_off,
                        kw.reshape(db, 2 * g, hd), lp["step_gains"], n_pages)
    o_nsa = o8.reshape(db, Q_COLS)
    feats = _rwkv_prep_step(xr, _pad_xr(state_shift), lp["rp"])
    o_rwkv, s1 = _rwkv_step(state_wkv, feats, lp["rk"], lp["lnw"], lp["lnb"])
    x = _out_ffn(x, o_nsa, o_rwkv, pe, lp)
    return (x, kv.reshape(db, 1, 4, g, hd), win, s1, _unpad_xr(xr))


def kernel(x_prompt, x_sample, p_prompt, p_sample, cache_kv, cache_win, state_wkv, state_shift, page_table, norm_mix, w_in, q_norm, k_norm, cmp_pe, cmp_w1, cmp_b1, cmp_w2, cmp_b2, shift_mu, w0, w2, a0, a2, g2, k_k, k_a, r_k, ln_w, ln_b, w_out, norm_ffn, w_gate, w_up, w_down, norm_ple, w_ple_gate, w_ple):
    b, t, d = x_prompt.shape
    db = x_sample.shape[0]
    depth, n_phys, page = cache_kv.shape[:3]
    n_pages = page_table.shape[1]
    assert x_sample.shape[1] == 1 and d == D_MODEL
    rows_last = (0, 1, 3, 4, 5, 2)
    cache_t = cache_kv.transpose(rows_last).reshape(depth * n_phys, -1, HEAD_DIM, page)
    wbuf = cache_win.shape[2]
    cwin_t = cache_win.transpose(rows_last).reshape(depth * db, -1, HEAD_DIM, wbuf)
    pt_flat = page_table.reshape(-1).astype(jnp.int32)
    xp = x_prompt.reshape(b * t, d)
    xs = x_sample.reshape(db, d)
    outs_p, outs_s = [], []
    for i in range(depth):
        lp = _layer_params(i, norm_mix, w_in, q_norm, k_norm, cmp_pe, cmp_w1, cmp_b1, cmp_w2, cmp_b2, shift_mu,
                           w0, w2, a0, a2, g2, k_k, k_a, r_k, ln_w, ln_b, w_out, norm_ffn, w_gate, w_up, w_down,
                           norm_ple, w_ple_gate, w_ple)
        xp, *rest_p = _prompt_layer(xp, p_prompt[i].reshape(b * t, -1), lp, b, t)
        xs, *rest_s = _sample_layer(xs, p_sample[i].reshape(db, -1), lp, cache_t, pt_flat + i * n_phys, n_pages,
                                    cwin_t, i * db, state_wkv[i], state_shift[i])
        outs_p.append(rest_p)
        outs_s.append(rest_s)
    stack = lambda outs, k: jnp.stack([o[k] for o in outs])
    g = NSA_KV_HEADS
    win_s = stack(outs_s, 1).reshape(depth, db, 2, g, HEAD_DIM, wbuf).transpose(0, 1, 5, 2, 3, 4)
    return (xp.reshape(b, t, d), xs.reshape(db, 1, d),
            stack(outs_p, 0), stack(outs_s, 0), stack(outs_p, 1), win_s,
            stack(outs_p, 2), stack(outs_s, 2), stack(outs_p, 3), stack(outs_s, 3))
```

```python
import functools

import jax
import jax.numpy as jnp
from jax import lax
from jax.experimental import pallas as pl
from jax.experimental.pallas import tpu as pltpu

F32 = jnp.float32
BF16 = jnp.bfloat16
HI = lax.Precision.HIGHEST

D_MODEL = 1024
HEAD_DIM = 64
LANES = 128
NSA_HEADS = 8
NSA_KV_HEADS = 2
RWKV_HEADS = 8
RWKV_WIDTH = RWKV_HEADS * HEAD_DIM
CMP_BLOCK = 32
CMP_STRIDE = 16
CMP_HIDDEN = 256
SEL_BLOCK = 64
SEL_TOPN = 16
WINDOW = 512
Q_BLOCK = 512
DECAY_RANK = 64
A_RANK = 64
GATE_RANK = 128
Q_COLS = NSA_HEADS * HEAD_DIM
KV_COLS = 4 * NSA_KV_HEADS * HEAD_DIM
WIN_COLS = 2 * NSA_KV_HEADS * HEAD_DIM
GATE_COLS = 3 * NSA_HEADS
RWKV_PROJ = 3 * RWKV_WIDTH + DECAY_RANK + A_RANK + GATE_RANK
XR_PAD = 3 * RWKV_WIDTH + 3 * LANES
IN_PAD = Q_COLS + KV_COLS + WIN_COLS + LANES + XR_PAD
NORM_EPS = 1e-6
GN_EPS = 64e-5
BIG = 1e9
NEG = -1e30
RWKV_CHUNK = 64
RWKV_SEQS_PER_STEP = 4
VMEM_LIMIT =56 * 1024 * 1024


def _dot(a, b, prec=None):
    return lax.dot_general(a, b, (((1,), (0,)), ((), ())), precision=prec, preferred_element_type=F32)


def _dot_nt(a, b, prec=None):
    return lax.dot_general(a, b, (((1,), (1,)), ((), ())), precision=prec, preferred_element_type=F32)


def _dot_tn(a, b, prec=None):
    return lax.dot_general(a, b, (((0,), (0,)), ((), ())), precision=prec, preferred_element_type=F32)


def _iota(shape, dim):
    return lax.broadcasted_iota(jnp.int32, shape, dim)


def _lo_mask(shape):
    return (_iota(shape, len(shape) - 1) % LANES) < HEAD_DIM


def _seg_sum(x):
    lo = _lo_mask(x.shape)
    s_lo = jnp.sum(jnp.where(lo, x, 0.0), axis=-1, keepdims=True)
    s_hi = jnp.sum(jnp.where(lo, 0.0, x), axis=-1, keepdims=True)
    return jnp.where(lo, s_lo, s_hi)


def _seg_rmsnorm(x, gain):
    ms = _seg_sum(x * x) * (1.0 / HEAD_DIM)
    return x * lax.rsqrt(ms + NORM_EPS) * gain


def _rmsnorm_rows(x, gain):
    return x * lax.rsqrt(jnp.mean(x * x, axis=-1, keepdims=True) + NORM_EPS) * gain


def _sigmoid(x):
    return 1.0 / (1.0 + jnp.exp(-x))


def _masked_softmax(s, mask):
    sm = jnp.where(mask, s, NEG)
    m = jnp.max(sm, axis=-1, keepdims=True)
    m = jnp.where(m > 0.5 * NEG, m, 0.0)
    e = jnp.where(mask, jnp.exp(sm - m), 0.0)
    return e * (1.0 / jnp.maximum(jnp.sum(e, axis=-1, keepdims=True), 1e-30))


def _softmax_av(s, bias, v):
    s = s + bias
    e = jnp.exp(s - jnp.max(s, axis=-1, keepdims=True))
    return _dot(e.astype(BF16), v) * (1.0 / jnp.sum(e, axis=-1, keepdims=True))


def _params(sem):
    return pltpu.CompilerParams(dimension_semantics=sem, vmem_limit_bytes=VMEM_LIMIT)


def _const_spec(shape):
    nd = len(shape)
    return pl.BlockSpec(shape, lambda *_: (0,) * nd)


_IN_SPLITS = (("q", 0, Q_COLS), ("kv", Q_COLS, KV_COLS), ("kw", Q_COLS + KV_COLS, WIN_COLS),
              ("gl", Q_COLS + KV_COLS + WIN_COLS, LANES),
              ("xr", Q_COLS + KV_COLS + WIN_COLS + LANES, XR_PAD))


def _in_proj_kernel(x_ref, g_ref, w_ref, *out_refs):
    h = _rmsnorm_rows(x_ref[...], g_ref[...]).astype(BF16)
    for o_ref, (_, lo, n) in zip(out_refs, _IN_SPLITS):
        o_ref[...] = _dot(h, w_ref[:, lo:lo + n])


def _in_proj(x, gain, w_pad):
    m = x.shape[0]
    tm = min(m, 512)
    assert m % tm == 0
    return pl.pallas_call(
        _in_proj_kernel,
        grid=(m // tm,),
        in_specs=[pl.BlockSpec((tm, D_MODEL), lambda i: (i, 0)),
                  _const_spec((1, D_MODEL)),
                  _const_spec((D_MODEL, IN_PAD))],
        out_specs=[pl.BlockSpec((tm, n), lambda i: (i, 0)) for _, _, n in _IN_SPLITS],
        out_shape=[jax.ShapeDtypeStruct((m, n), F32) for _, _, n in _IN_SPLITS],
        compiler_params=_params(("parallel",)),
        name="in_proj",
    )(x, gain, w_pad)


def _pad_xr(v):
    z = jnp.zeros(v.shape[:-1] + (LANES - DECAY_RANK,), v.dtype)
    c = 3 * RWKV_WIDTH
    return jnp.concatenate([v[..., :c + DECAY_RANK], z, v[..., c + DECAY_RANK:c + DECAY_RANK + A_RANK], z,
                            v[..., c + DECAY_RANK + A_RANK:]], axis=-1)


def _unpad_xr(v):
    c = 3 * RWKV_WIDTH
    return jnp.concatenate([v[..., :c + DECAY_RANK], v[..., c + LANES:c + LANES + A_RANK], v[..., c + 2 * LANES:]],
                           axis=-1)


def _pad_w_in(w_in):
    c = Q_COLS + KV_COLS + WIN_COLS
    gl = jnp.pad(w_in[:, c:c + GATE_COLS], ((0, 0), (0, LANES - GATE_COLS)))
    return jnp.concatenate([w_in[:, :c], gl, _pad_xr(w_in[:, c + GATE_COLS:])], axis=-1).astype(BF16)


def _ffn_kernel(x_ref, on_ref, or_ref, pe_ref, wout_ref, nf_ref, wg_ref, wu_ref, wd_ref, np_ref, wpg_ref,
                wple_ref, y_ref, x1_s, hf_s, acc_s):
    j = pl.program_id(1)

    @pl.when(j == 0)
    def _():
        half = D_MODEL // 2
        o = _dot(on_ref[...].astype(BF16), wout_ref[0:half, :]) + _dot(or_ref[...].astype(BF16), wout_ref[half:, :])
        x1 = x_ref[...] + o
        x1_s[...] = x1
        hf_s[...] = _rmsnorm_rows(x1, nf_ref[...]).astype(BF16)
        acc_s[...] = jnp.zeros_like(acc_s)

    hf = hf_s[...]
    g = _dot(hf, wg_ref[...])
    u = _dot(hf, wu_ref[...])
    act = (g * _sigmoid(g)) * u
    acc_s[...] += _dot(act.astype(BF16), wd_ref[...])

    @pl.when(j == pl.num_programs(1) - 1)
    def _():
        x2 = x1_s[...] + acc_s[...]
        hp = _rmsnorm_rows(x2, np_ref[...]).astype(BF16)
        pg = _sigmoid(_dot(hp, wpg_ref[...]))
        y_ref[...] = x2 + pg * _dot(pe_ref[...].astype(BF16), wple_ref[...])


def _out_ffn(x, o_nsa, o_rwkv, pe, lw):
    m = x.shape[0]
    tm = min(m, 512)
    hid = lw["w_gate"].shape[1]
    th = 1408
    assert m % tm == 0 and hid % th == 0
    ple = pe.shape[1]
    row = lambda i, j: (i, 0)
    return pl.pallas_call(
        _ffn_kernel,
        grid=(m // tm, hid // th),
        in_specs=[pl.BlockSpec((tm, D_MODEL), row),
                  pl.BlockSpec((tm, D_MODEL // 2), row),
                  pl.BlockSpec((tm, D_MODEL // 2), row),
                  pl.BlockSpec((tm, ple), row),
                  _const_spec((D_MODEL, D_MODEL)),
                  _const_spec((1, D_MODEL)),
                  pl.BlockSpec((D_MODEL, th), lambda i, j: (0, j)),
                  pl.BlockSpec((D_MODEL, th), lambda i, j: (0, j)),
                  pl.BlockSpec((th, D_MODEL), lambda i, j: (j, 0)),
                  _const_spec((1, D_MODEL)),
                  _const_spec((D_MODEL, D_MODEL)),
                  _const_spec((ple, D_MODEL))],
        out_specs=pl.BlockSpec((tm, D_MODEL), row),
        out_shape=jax.ShapeDtypeStruct((m, D_MODEL), F32),
        scratch_shapes=[pltpu.VMEM((tm, D_MODEL), F32), pltpu.VMEM((tm, D_MODEL), BF16),
                        pltpu.VMEM((tm, D_MODEL), F32)],
        compiler_params=_params(("parallel", "arbitrary")),
        name="out_ffn_ple",
    )(x, o_nsa, o_rwkv, pe, lw["w_out"], lw["norm_ffn"], lw["w_gate"], lw["w_up"], lw["w_down"],
      lw["norm_ple"], lw["w_ple_gate"], lw["w_ple"])


def _gelu_tanh(x):
    return 0.5 * x * (1.0 + jnp.tanh(0.7978845608028654 * (x + 0.044715 * x * x * x)))


def _compress_body(load_rows, nbp, first_step, pe_ref, wbd_ref, b1_ref, w2lo_ref, w2hi_ref, b2_ref, kn_ref,
                   kcn_ref, kcr_ref, vcn_ref, vcr_ref, bias_s):
    hid = CMP_HIDDEN

    @pl.when(first_step)
    def _():
        for ty in range(2):
            acc = jnp.zeros((1, hid), F32)
            for r2 in range(CMP_STRIDE // 2):
                t = _dot(pe_ref[ty, r2:r2 + 1, :].astype(BF16), wbd_ref[ty, r2])
                acc = acc + t[:, 0:hid] + t[:, 3 * hid:4 * hid]
            bias_s[ty:ty + 1, :] = acc + b1_ref[ty:ty + 1, :]

    acc = [jnp.zeros((nbp, 4 * hid), F32) for _ in range(2)]
    for r2 in range(CMP_STRIDE // 2):
        for ty in range(2):
            lhs = jnp.concatenate([load_rows(2 * r2, ty), load_rows(2 * r2 + 1, ty)], axis=-1).astype(BF16)
            acc[ty] = acc[ty] + _dot(lhs, wbd_ref[ty, r2])
    outs = ((kcn_ref, kcr_ref), (vcn_ref, vcr_ref))
    for ty in range(2):
        hs = []
        for g in range(2):
            a = acc[ty][:, 2 * hid * g:2 * hid * g + hid]
            b = acc[ty][:, 2 * hid * g + hid:2 * hid * (g + 1)]
            b_next = pltpu.roll(b, nbp - 1, 0)
            hs.append(_gelu_tanh(a + b_next + bias_s[ty:ty + 1, :]).astype(BF16))
        o_n = _dot(hs[0], w2lo_ref[ty]) + _dot(hs[1], w2hi_ref[ty]) + b2_ref[ty:ty + 1, :]
        o_r = _dot(hs[0], w2hi_ref[ty]) + _dot(hs[1], w2lo_ref[ty]) + b2_ref[ty:ty + 1, :]
        if ty == 0:
            o_n = _seg_rmsnorm(o_n, kn_ref[...])
            o_r = _seg_rmsnorm(o_r, kn_ref[...])
        outs[ty][0][0] = o_n.astype(outs[ty][0].dtype)
        outs[ty][1][0] = o_r.astype(outs[ty][1].dtype)


def _compress_prompt_kernel(k_ref, v_ref, *refs, nbp):
    load = lambda r, ty: (k_ref, v_ref)[ty][0, pl.ds(r, nbp, stride=CMP_STRIDE), :]
    _compress_body(load, nbp, pl.program_id(0) == 0, *refs)


def _compress_paged_kernel(pt_ref, *refs, nbp, n_pages, page):
    del pt_ref
    page_refs, rest, rows_s = refs[:n_pages], refs[n_pages:-2], refs[-2:]
    for j in range(n_pages):
        for ty in range(2):
            tiles = page_refs[j][0, 2 * ty:2 * ty + 2].reshape(2 * HEAD_DIM, page)
            rows_s[ty][j * page:(j + 1) * page, :] = tiles.T
    load = lambda r, ty: rows_s[ty][pl.ds(r, nbp, stride=CMP_STRIDE), :]
    _compress_body(load, nbp, pl.program_id(0) == 0, *rest)


def _compress_weight_specs():
    hid = CMP_HIDDEN
    return [_const_spec((2, CMP_STRIDE // 2, 2 * LANES)), _const_spec((2, CMP_STRIDE // 2, 2 * LANES, 4 * hid)),
            _const_spec((2, hid)), _const_spec((2, hid, LANES)), _const_spec((2, hid, LANES)),
            _const_spec((2, LANES)), _const_spec((1, LANES))]


def _compress_outs(b, nbp):
    spec = pl.BlockSpec((1, nbp, LANES), lambda i, *_: (i, 0, 0))
    shapes = [jax.ShapeDtypeStruct((b, nbp, LANES), dt) for dt in (F32, F32, BF16, BF16)]
    return [spec] * 4, shapes


def _compress_prompt(kv, cw):
    b, t, _ = kv.shape
    nbp = t // CMP_STRIDE
    out_specs, out_shape = _compress_outs(b, nbp)
    return pl.pallas_call(
        functools.partial(_compress_prompt_kernel, nbp=nbp),
        grid=(b,),
        in_specs=[pl.BlockSpec((1, t, LANES), lambda i: (i, 0, 0)),
                  pl.BlockSpec((1, t, LANES), lambda i: (i, 0, 1))] + _compress_weight_specs(),
        out_specs=out_specs, out_shape=out_shape,
        scratch_shapes=[pltpu.VMEM((2, CMP_HIDDEN), F32)],
        compiler_params=_params(("arbitrary",)),
        name="compress_prompt",
    )(kv, kv, *cw)


def _compress_paged(cache, pt_flat, n_seq, n_pages, cw):
    page = cache.shape[3]
    assert page == LANES
    nbp = n_pages * page // CMP_STRIDE
    out_specs, out_shape = _compress_outs(n_seq, nbp)
    page_specs = [pl.BlockSpec((1, 4, HEAD_DIM, page), functools.partial(
        lambda i, pt, j: (pt[i * n_pages + j], 0, 0, 0), j=j)) for j in range(n_pages)]
    return pl.pallas_call(
        functools.partial(_compress_paged_kernel, nbp=nbp, n_pages=n_pages, page=page),
        grid_spec=pltpu.PrefetchScalarGridSpec(
            num_scalar_prefetch=1, grid=(n_seq,),
            in_specs=page_specs + _compress_weight_specs(),
            out_specs=out_specs,
            scratch_shapes=[pltpu.VMEM((2, CMP_HIDDEN), F32), pltpu.VMEM((n_pages * page, LANES), F32),
                            pltpu.VMEM((n_pages * page, LANES), F32)]),
        out_shape=out_shape,
        compiler_params=_params(("arbitrary",)),
        name="compress_paged",
    )(pt_flat, *([cache] * n_pages), *cw)


def _compress_weights(cmp_pe, cmp_w1, cmp_b1, cmp_w2, cmp_b2, k_norm0):
    hid = CMP_HIDDEN
    w1 = cmp_w1.reshape(2, CMP_BLOCK, HEAD_DIM, hid)
    wcat = jnp.concatenate([w1[:, :CMP_STRIDE], w1[:, CMP_STRIDE:]], axis=-1)
    z = jnp.zeros_like(wcat)
    wbd = jnp.concatenate([jnp.concatenate([wcat, z], -1), jnp.concatenate([z, wcat], -1)], axis=2).astype(BF16)
    wbd = wbd.reshape(2, CMP_STRIDE // 2, 2 * LANES, 4 * hid)
    pe_pair = jnp.concatenate([cmp_pe[:, :CMP_STRIDE], cmp_pe[:, CMP_STRIDE:]], axis=-1)
    pe_pair = pe_pair.reshape(2, CMP_STRIDE // 2, 2 * LANES)
    z2 = jnp.zeros_like(cmp_w2)
    w2lo = jnp.concatenate([cmp_w2, z2], -1).astype(BF16)
    w2hi = jnp.concatenate([z2, cmp_w2], -1).astype(BF16)
    b2 = jnp.concatenate([cmp_b2, cmp_b2], -1)
    kn = jnp.concatenate([k_norm0, k_norm0])[None]
    return (pe_pair, wbd, cmp_b1, w2lo, w2hi, b2, kn)


def _kv_prep_kernel(kv_ref, kw_ref, kns_ref, knw_ref, ksn_ref, ksr_ref, vsn_ref, vsr_ref, kwn_ref, kwr_ref,
                    vwn_ref, vwr_ref):
    def put(x, n_ref, r_ref):
        n_ref[0] = x.astype(BF16)
        r_ref[0] = pltpu.roll(x, HEAD_DIM, 1).astype(BF16)

    put(_seg_rmsnorm(kv_ref[0, :, 2 * LANES:3 * LANES], kns_ref[...]), ksn_ref, ksr_ref)
    put(kv_ref[0, :, 3 * LANES:4 * LANES], vsn_ref, vsr_ref)
    put(_seg_rmsnorm(kw_ref[0, :, 0:LANES], knw_ref[...]), kwn_ref, kwr_ref)
    put(kw_ref[0, :, LANES:2 * LANES], vwn_ref, vwr_ref)


def _kv_prep(kv, kw, kn_sel, kn_win):
    b, t, _ = kv.shape
    tm = min(t, 512)
    spec = lambda n: pl.BlockSpec((1, tm, n), lambda i, j: (i, j, 0))
    return pl.pallas_call(
        _kv_prep_kernel,
        grid=(b, t // tm),
        in_specs=[spec(4 * LANES), spec(2 * LANES), _const_spec((1, LANES)), _const_spec((1, LANES))],
        out_specs=[spec(LANES)] * 8,
        out_shape=[jax.ShapeDtypeStruct((b, t, LANES), BF16)] * 8,
        compiler_params=_params(("parallel", "parallel")),
        name="kv_prep",
    )(kv, kw, kn_sel, kn_win)


def _overlap_t(ns, nbp):
    j = _iota((ns, nbp), 0) * SEL_BLOCK
    n = _iota((ns, nbp), 1) * CMP_STRIDE
    return jnp.where((n < j + SEL_BLOCK) & (n + CMP_BLOCK > j), 1.0, 0.0).astype(F32)


def _nsa_prompt_kernel(q_ref, gl_ref, qn_ref, kcn_ref, kcr_ref, vcn_ref, vcr_ref, ksn_ref, ksr_ref, vsn_ref,
                       vsr_ref, kwn_ref, kwr_ref, vwn_ref, vwr_ref, o_ref, *, seq, tk, wk):
    i = pl.program_id(1)
    qb = Q_BLOCK
    nbp = kcn_ref.shape[1]
    nb = nbp - 1
    ns = seq // SEL_BLOCK
    n_sel = min(SEL_TOPN, ns)
    lo = _lo_mask((qb, LANES))
    pos = i * qb + _iota((qb, 1), 0)
    sig = _sigmoid(gl_ref[0])

    qp = [_seg_rmsnorm(q_ref[0, :, LANES * p:LANES * (p + 1)], qn_ref[...]) * (HEAD_DIM ** -0.5) for p in range(4)]
    q_lo = [jnp.where(lo, x, 0.0) for x in qp]
    q_hi = [jnp.where(lo, 0.0, x) for x in qp]

    n_c = _iota((qb, nbp), 1)
    mask_c = (n_c * CMP_STRIDE + (CMP_BLOCK - 1) <= pos) & (n_c < nb)
    ov_t = _overlap_t(ns, nbp)
    j_t = _iota((ns, qb), 0)
    pos_t = i * qb + _iota((ns, qb), 1)
    cur_t = pos_t >> 6
    forced_t = (j_t == 0) | (j_t == cur_t) | (j_t == cur_t - 1)
    causal_t = j_t * SEL_BLOCK <= pos_t
    eye = jnp.where(_iota((qb, qb), 0) == _iota((qb, qb), 1), 1.0, 0.0).astype(BF16)

    w_start = pl.multiple_of(jnp.maximum(i * qb - (wk - qb), 0), qb)
    w_pos = w_start + _iota((qb, wk), 1)
    w_dist = pos - w_pos
    mask_w = (w_dist >= 0) & (w_dist <= WINDOW)
    n_tiles = (i * qb + qb - 1) // tk + 1

    order = lambda g, n, r: (n, r) if g == 0 else (r, n)
    heads = []
    for p in range(4):
        heads += [(p, p // 2, q_lo[p], 0), (p, p // 2, q_hi[p], 1)]

    split = lambda x: (x.astype(BF16), (x - x.astype(BF16).astype(F32)).astype(BF16))
    kc_parts = [split(kcn_ref[0]), split(kcr_ref[0])]

    def score3(qt, kc):
        q_h, q_l = split(qt)
        return _dot_nt(q_h, kc[0]) + (_dot_nt(q_h, kc[1]) + _dot_nt(q_l, kc[0]))

    p_cmp = [_masked_softmax(score3(qt, order(g, *kc_parts)[w]), mask_c) for _, g, qt, w in heads]
    o_cmp = [_dot(pc.astype(BF16), order(g, vcn_ref, vcr_ref)[w][0]) for pc, (_, g, _, w) in zip(p_cmp, heads)]
    unsel = []
    for g in range(2):
        psum = p_cmp[4 * g] + p_cmp[4 * g + 1] + p_cmp[4 * g + 2] + p_cmp[4 * g + 3]
        imp_t = _dot_nt(ov_t, psum, HI)
        score = jnp.where(forced_t, BIG, jnp.where(causal_t, imp_t, -BIG))
        rank = jnp.zeros((ns, qb), jnp.int32)
        for jp in range(ns):
            row = score[jp:jp + 1, :]
            rank = rank + jnp.where((row > score) | ((row == score) & (j_t > jp)), 1, 0)
        unsel.append(_dot_nt(eye, jnp.where(rank < n_sel, 0.0, NEG).astype(BF16)).astype(BF16))

    q16 = [qt.astype(BF16) for _, _, qt, _ in heads]

    def tile(kt, carry, diagonal):
        k0 = pl.multiple_of(kt * tk, tk)
        blk = (k0 + _iota((ns, tk), 1)) >> 6
        expand = jnp.where(_iota((ns, tk), 0) == blk, 1.0, 0.0).astype(BF16)
        bias = [_dot(u, expand) for u in unsel]
        if diagonal:
            causal = k0 + _iota((qb, tk), 1) <= pos
            bias = [jnp.where(causal, x, NEG) for x in bias]
        out = []
        for (m, l, acc), q_, (_, g, _, w) in zip(carry, q16, heads):
            k_ref = order(g, ksn_ref, ksr_ref)[w]
            v_ref = order(g, vsn_ref, vsr_ref)[w]
            s = _dot_nt(q_, k_ref[0, pl.ds(k0, tk), :]) + bias[g]
            m_new = jnp.maximum(m, jnp.max(s, axis=-1, keepdims=True))
            alpha = jnp.exp(m - m_new)
            e = jnp.exp(s - m_new)
            l_new = alpha * l + jnp.sum(e, axis=-1, keepdims=True)
            acc_new = alpha * acc + _dot(e.astype(BF16), v_ref[0, pl.ds(k0, tk), :])
            out.append((m_new, l_new, acc_new))
        return tuple(out)

    init = (jnp.full((qb, 1), NEG, F32), jnp.zeros((qb, 1), F32), jnp.zeros((qb, LANES), F32))
    carry = lax.fori_loop(0, n_tiles - 1, functools.partial(tile, diagonal=False), (init,) * len(heads))
    o_sel = [acc / l for _, l, acc in tile(n_tiles - 1, carry, True)]

    bias_w = jnp.where(mask_w, 0.0, NEG)
    o_win = [_softmax_av(_dot_nt(q_, order(g, kwn_ref, kwr_ref)[w][0, pl.ds(w_start, wk), :]), bias_w,
                         order(g, vwn_ref, vwr_ref)[w][0, pl.ds(w_start, wk), :])
             for q_, (_, g, _, w) in zip(q16, heads)]

    for p in range(4):
        gate = lambda c: jnp.where(lo, sig[:, 6 * p + c:6 * p + c + 1], sig[:, 6 * p + 3 + c:6 * p + 4 + c])
        both = lambda o: jnp.where(lo, o[2 * p], o[2 * p + 1])
        o_ref[0, :, LANES * p:LANES * (p + 1)] = gate(0) * both(o_cmp) + gate(1) * both(o_sel) + gate(2) * both(o_win)


def _nsa_prompt(q, gl, qn, kc, ksv, seq):
    b = q.shape[0]
    nbp = kc[0].shape[1]
    tk = min(seq, 512)
    wk = min(seq, WINDOW + Q_BLOCK)
    assert seq % Q_BLOCK == 0 and seq % tk == 0
    blk = lambda n: pl.BlockSpec((1, Q_BLOCK, n), lambda bi, i: (bi, i, 0))
    full = lambda rows: pl.BlockSpec((1, rows, LANES), lambda bi, i: (bi, 0, 0))
    return pl.pallas_call(
        functools.partial(_nsa_prompt_kernel, seq=seq, tk=tk, wk=wk),
        grid=(b, seq // Q_BLOCK),
        in_specs=[blk(4 * LANES), blk(LANES), _const_spec((1, LANES))] + [full(nbp)] * 4 + [full(seq)] * 8,
        out_specs=blk(4 * LANES),
        out_shape=jax.ShapeDtypeStruct((b, seq, 4 * LANES), F32),
        compiler_params=_params(("parallel", "arbitrary")),
        name="nsa_prompt",
    )(q, gl, qn, *kc, *ksv)


def _nsa_step_kernel(pt_ref, *refs, n_pages, page, past):
    del pt_ref
    page_refs = refs[:n_pages]
    (qm_ref, gl_ref, kcn_ref, vcn_ref, kvn_ref, cw_ref, kwn_ref, qn_ref, kns_ref, knw_ref, kns_row_ref, knw_row_ref,
     o_ref, win_ref) = refs[n_pages:]
    nh = NSA_HEADS
    hd = HEAD_DIM
    nbp = kcn_ref.shape[1]
    nb = nbp - 1
    ns = past // SEL_BLOCK + 1
    n_sel = min(SEL_TOPN, ns)
    wbuf = cw_ref.shape[3]
    rep = NSA_HEADS // NSA_KV_HEADS
    first_group = _iota((nh, 1), 0) < rep

    def unpair(x):
        return jnp.where(first_group, x, pltpu.roll(x, hd, 1))[:, 0:hd]

    def by_group(f):
        return jnp.where(first_group, f(0), f(1))

    def norm_t(t, gain_ref):
        return (t * lax.rsqrt(jnp.mean(t * t, axis=0, keepdims=True) + NORM_EPS) * gain_ref[...]).astype(BF16)

    qn = _seg_rmsnorm(qm_ref[0], qn_ref[...]) * (HEAD_DIM ** -0.5)
    q8 = unpair(qn)
    q8b = q8.astype(BF16)
    sg = _sigmoid(gl_ref[0])

    n_c = _iota((nh, nbp), 1)
    mask_c = (n_c * CMP_STRIDE + (CMP_BLOCK - 1) <= past) & (n_c < nb)
    p_c = _masked_softmax(_dot_nt(qn, kcn_ref[0], HI), mask_c)
    o_cmp = unpair(_dot(p_c.astype(BF16), vcn_ref[0]))
    same_group = jnp.where(_iota((nh, nh), 0) // rep == _iota((nh, nh), 1) // rep, 1.0, 0.0).astype(F32)
    psum = _dot(same_group, p_c, HI)
    imp = _dot_nt(psum, _overlap_t(LANES, nbp), HI)
    j = _iota((nh, LANES), 1)
    cur = past // SEL_BLOCK
    forced = (j == 0) | (j == cur) | (j == cur - 1)
    score = jnp.where(forced, BIG, jnp.where(j * SEL_BLOCK <= past, imp, -BIG))
    score = jnp.where(j < ns, score, -4.0 * BIG)
    rank = jnp.zeros((nh, LANES), jnp.int32)
    for jp in range(ns):
        col = score[:, jp:jp + 1]
        rank = rank + jnp.where((col > score) | ((col == score) & (j > jp)), 1, 0)
    sel = jnp.where((rank < n_sel) & (j < ns), 1.0, 0.0).astype(BF16)

    kv_new = kvn_ref[0]
    k_new = by_group(lambda g: _rmsnorm_rows(kv_new[4 + g:5 + g, :], kns_row_ref[...]))
    s_new = jnp.sum(q8 * k_new, axis=-1, keepdims=True)
    s_parts, m_parts = [], []
    for pj in range(n_pages):
        blk = (pj * page + _iota((LANES, page), 1)) >> 6
        expand = jnp.where(_iota((LANES, page), 0) == blk, 1.0, 0.0).astype(BF16)
        m_parts.append(_dot(sel, expand) > 0.5)
        s_parts.append(by_group(lambda g: _dot(q8b, norm_t(page_refs[pj][0, g], kns_ref))))
    s_all = jnp.concatenate(s_parts, axis=-1)
    valid = jnp.concatenate(m_parts, axis=-1)
    s_all = jnp.where(valid, s_all, NEG)
    m = jnp.maximum(jnp.max(s_all, axis=-1, keepdims=True), s_new)
    e_all = jnp.where(valid, jnp.exp(s_all - m), 0.0)
    e_new = jnp.exp(s_new - m)
    acc = e_new * by_group(lambda g: kv_new[6 + g:7 + g, :])
    for pj in range(n_pages):
        e_pj = e_all[:, pj * page:(pj + 1) * page].astype(BF16)
        acc = acc + by_group(lambda g: _dot_nt(e_pj, page_refs[pj][0, 2 + g].astype(BF16)))
    o_sel = acc / (jnp.sum(e_all, axis=-1, keepdims=True) + e_new)

    kw_new = kwn_ref[0]
    s_w = by_group(lambda g: _dot(q8b, norm_t(cw_ref[0, g], knw_ref)))
    s_wn = jnp.sum(q8 * by_group(lambda g: _rmsnorm_rows(kw_new[g:g + 1, :], knw_row_ref[...])), axis=-1, keepdims=True)
    m_w = jnp.maximum(jnp.max(s_w, axis=-1, keepdims=True), s_wn)
    e_w = jnp.exp(s_w - m_w)
    e_wn = jnp.exp(s_wn - m_w)
    e_wb = e_w.astype(BF16)
    o_win = (by_group(lambda g: _dot_nt(e_wb, cw_ref[0, 2 + g].astype(BF16))) + e_wn * by_group(
        lambda g: kw_new[2 + g:3 + g, :])) / (jnp.sum(e_w, axis=-1, keepdims=True) + e_wn)

    o_ref[0] = sg[:, 0:1] * o_cmp + sg[:, 1:2] * o_sel + sg[:, 2:3] * o_win

    eye = _iota((hd, hd), 0) == _iota((hd, hd), 1)
    last = _iota((hd, wbuf), 1) == wbuf - 1
    for c in range(4):
        col = jnp.sum(jnp.where(eye, kw_new[c:c + 1, :], 0.0), axis=1, keepdims=True)
        win_ref[0, c] = jnp.where(last, col, pltpu.roll(cw_ref[0, c], wbuf - 1, 1))


def _nsa_step(cache, pt_flat, qm, glm, kcn, vcn, kv_new, cache_win, win_off, kw_new, gains, n_pages):
    db = qm.shape[0]
    page = cache.shape[3]
    past = n_pages * page
    nbp = kcn.shape[1]
    wbuf = cache_win.shape[3]
    hd = HEAD_DIM
    assert wbuf == min(WINDOW, past) and page == LANES and past // SEL_BLOCK + 1 <= LANES
    per = lambda shape: pl.BlockSpec((1,) + shape, lambda i, pt: (i,) + (0,) * len(shape))
    page_specs = [pl.BlockSpec((1, 4, hd, page), functools.partial(
        lambda i, pt, j: (pt[i * n_pages + j], 1, 0, 0), j=j)) for j in range(n_pages)]
    return pl.pallas_call(
        functools.partial(_nsa_step_kernel, n_pages=n_pages, page=page, past=past),
        grid_spec=pltpu.PrefetchScalarGridSpec(
            num_scalar_prefetch=1, grid=(db,),
            in_specs=page_specs + [per((NSA_HEADS, LANES)), per((NSA_HEADS, LANES)), per((nbp, LANES)),
                                   per((nbp, LANES)), per((8, hd)),
                                   pl.BlockSpec((1, 4, hd, wbuf), lambda i, pt: (i + win_off, 0, 0, 0)), per((4, hd))]
            + [_const_spec((1, LANES)), _const_spec((hd, 1)), _const_spec((hd, 1)), _const_spec((1, hd)),
               _const_spec((1, hd))],
            out_specs=[per((NSA_HEADS, hd)), per((4, hd, wbuf))]),
        out_shape=[jax.ShapeDtypeStruct((db, NSA_HEADS, hd), F32),
                   jax.ShapeDtypeStruct((db, 4, hd, wbuf), F32)],
        compiler_params=_params(("parallel",)),
        name="nsa_step",
    )(pt_flat, *([cache] * n_pages), qm, glm, kcn, vcn, kv_new, cache_win, kw_new, *gains)


def _softplus(z):
    return jnp.maximum(z, 0.0) + jnp.log(1.0 + jnp.exp(-jnp.abs(z)))


def _rwkv_features(x, prev, mu_ref, w0_ref, a0_ref, w2_ref, a2_ref, g2_ref, kk_ref, ka_ref, outs):
    r_ref, lw_ref, k_ref, v_ref, a_ref, b_ref, g_ref = outs
    rw = RWKV_WIDTH
    xs = x + (prev - x) * mu_ref[...]
    r, k, v = xs[:, 0:rw], xs[:, rw:2 * rw], xs[:, 2 * rw:3 * rw]
    wd = xs[:, 3 * rw:3 * rw + LANES]
    ad = xs[:, 3 * rw + LANES:3 * rw + 2 * LANES]
    gd = xs[:, 3 * rw + 2 * LANES:3 * rw + 3 * LANES]
    w_log = -_softplus(-(w0_ref[...] + _dot(jnp.tanh(wd).astype(BF16), w2_ref[...]))) - 0.5
    a_sig = _sigmoid(a0_ref[...] + _dot(ad.astype(BF16), a2_ref[...]))
    kk = k * kk_ref[...]
    kk2 = kk * kk
    norm2 = jnp.concatenate([_seg_sum(kk2[:, LANES * p:LANES * (p + 1)]) for p in range(rw // LANES)], axis=-1)
    kk = kk / jnp.maximum(jnp.sqrt(norm2), 1e-12)
    r_ref[...] = r.reshape(r_ref.shape)
    lw_ref[...] = (-jnp.exp(w_log)).reshape(lw_ref.shape)
    k_ref[...] = (k * (1.0 + (a_sig - 1.0) * ka_ref[...])).reshape(k_ref.shape)
    v_ref[...] = v.reshape(v_ref.shape)
    a_ref[...] = (-kk).reshape(a_ref.shape)
    b_ref[...] = (kk * a_sig).reshape(b_ref.shape)
    g_ref[...] = _dot(_sigmoid(gd).astype(BF16), g2_ref[...]).reshape(g_ref.shape)


def _rwkv_prep_seq_kernel(xr_ref, shift_ref, *refs):
    params, outs, last_s = refs[:8], refs[8:15], refs[15]
    t = pl.program_id(1)

    @pl.when(t == 0)
    def _():
        last_s[...] = shift_ref[0]

    x = xr_ref[0]
    tm = x.shape[0]
    prev = jnp.where(_iota(x.shape, 0) == 0, last_s[...], pltpu.roll(x, 1, 0))
    last_s[...] = x[tm - 1:tm, :]
    _rwkv_features(x, prev, *params, outs)


def _rwkv_prep_step_kernel(xr_ref, shift_ref, *refs):
    _rwkv_features(xr_ref[...], shift_ref[...], *refs[:8], refs[8:15])


def _rwkv_param_specs():
    one = _const_spec((1, RWKV_WIDTH))
    low = _const_spec((LANES, RWKV_WIDTH))
    return [_const_spec((1, XR_PAD)), one, one, low, low, low, one, one]


def _rwkv_prep_seq(xr, shift0, rp):
    b, t, _ = xr.shape
    tm = min(t, 512)
    out = pl.BlockSpec((1, tm, RWKV_WIDTH), lambda i, j: (i, j, 0))
    return pl.pallas_call(
        _rwkv_prep_seq_kernel,
        grid=(b, t // tm),
        in_specs=[pl.BlockSpec((1, tm, XR_PAD), lambda i, j: (i, j, 0)),
                  pl.BlockSpec((1, 1, XR_PAD), lambda i, j: (i, 0, 0))] + _rwkv_param_specs(),
        out_specs=[out] * 7,
        out_shape=[jax.ShapeDtypeStruct((b, t, RWKV_WIDTH), F32)] * 7,
        scratch_shapes=[pltpu.VMEM((1, XR_PAD), F32)],
        compiler_params=_params(("parallel", "arbitrary")),
        name="rwkv_prep_seq",
    )(xr, shift0, *rp)


def _rwkv_prep_step(xr, shift0, rp):
    m = xr.shape[0]
    return pl.pallas_call(
        _rwkv_prep_step_kernel,
        grid=(1,),
        in_specs=[_const_spec((m, XR_PAD)), _const_spec((m, XR_PAD))] + _rwkv_param_specs(),
        out_specs=[_const_spec((m, RWKV_WIDTH))] * 7,
        out_shape=[jax.ShapeDtypeStruct((m, RWKV_WIDTH), F32)] * 7,
        compiler_params=_params(("arbitrary",)),
        name="rwkv_prep_step",
    )(xr, shift0, *rp)


def _rwkv_params(shift_mu, w0, w2, a0, a2, g2, k_k, k_a):
    pad = lambda w: jnp.pad(w, ((0, LANES - w.shape[0]), (0, 0))).astype(BF16)
    return (_pad_xr(shift_mu)[None], w0[None], a0[None], pad(w2), pad(a2), g2.astype(BF16), k_k[None], k_a[None])


def _rwkv_finish(y, r, k, v, gate, rk, lnw, lnb):
    mu = _seg_sum(y) * (1.0 / HEAD_DIM)
    d = y - mu
    var = _seg_sum(d * d) * (1.0 / HEAD_DIM)
    yn = d * lax.rsqrt(var + GN_EPS) * lnw + lnb
    return (yn + _seg_sum(r * k * rk) * v) * gate


def _rwkv_scan_kernel(r_ref, lw_ref, k_ref, v_ref, a_ref, b_ref, g_ref, rk_ref, lnw_ref, lnb_ref,
                      o_ref, s_out_ref, s_s):
    c = pl.program_id(1)
    ch = r_ref.shape[1]
    gw = 4 * HEAD_DIM
    rows = 4 * ch

    @pl.when(c == 0)
    def _():
        s_s[...] = jnp.zeros_like(s_s)

    tril = jnp.where(_iota((ch, ch), 0) >= _iota((ch, ch), 1), 1.0, 0.0).astype(F32)
    t_row = _iota((rows, rows), 0) % ch
    t_col = _iota((rows, rows), 1) % ch
    strict = t_row > t_col
    incl = t_row >= t_col
    eye = jnp.where(_iota((rows, rows), 0) == _iota((rows, rows), 1), 1.0, 0.0).astype(F32)
    head_of_lane = _iota((ch, gw), 1) // HEAD_DIM

    def wide(x):
        return jnp.concatenate([jnp.where(head_of_lane == h, x, 0.0) for h in range(4)], axis=0).astype(BF16)

    units = [(i, q) for i in range(r_ref.shape[0]) for q in range(RWKV_WIDTH // gw)]
    each = lambda f, *cols: [f(*xs) for xs in zip(*cols)]
    load = lambda ref: [ref[i, :, gw * q:gw * (q + 1)] for i, q in units]
    r, lw, k, v, a, b = (load(ref) for ref in (r_ref, lw_ref, k_ref, v_ref, a_ref, b_ref))
    s0 = [s_s[i, q] for i, q in units]
    cl = each(lambda x: _dot(tril, x, HI), lw)
    g_in = each(jnp.exp, cl)
    g_inv = each(lambda x: jnp.exp(-x), cl)
    w_ar = each(lambda a_, r_, cl_, lw_, gi: jnp.concatenate([wide(a_ * jnp.exp(cl_ - lw_)), wide(r_ * gi)], axis=0),
                a, r, cl, lw, g_in)
    w_bk = each(lambda b_, k_, gv: jnp.concatenate([wide(b_ * gv), wide(k_ * gv)], axis=0), b, k, g_inv)
    w_v = each(wide, v)
    p_all = each(_dot_nt, w_ar, w_bk)
    xy0 = each(lambda x, s: _dot_nt(x, s.astype(BF16)), w_ar, s0)
    n = each(lambda p: jnp.where(strict, p[:rows, :rows], 0.0), p_all)
    l_ak = each(lambda p: jnp.where(strict, p[:rows, rows:], 0.0).astype(BF16), p_all)
    m_rb = each(lambda p: jnp.where(incl, p[rows:, :rows], 0.0).astype(BF16), p_all)
    m_rk = each(lambda p: jnp.where(incl, p[rows:, rows:], 0.0).astype(BF16), p_all)
    t_inv = each(lambda x: eye + x, n)
    pw = each(lambda x: x.astype(BF16), n)
    for _ in range(max(ch.bit_length() - 2, 0)):
        pw = each(lambda x: _dot(x, x).astype(BF16), pw)
        t_inv = each(lambda t, x: t + _dot(t.astype(BF16), x), t_inv, pw)
    x_w = each(lambda xy, l, vv: (xy[:rows] + _dot(l, vv)).astype(BF16), xy0, l_ak, w_v)
    u = each(lambda t, x: _dot(t.astype(BF16), x).astype(BF16), t_inv, x_w)
    y_w = each(lambda xy, mb, uu, mk, vv: xy[rows:] + _dot(mb, uu) + _dot(mk, vv), xy0, m_rb, u, m_rk, w_v)
    upd = each(lambda uu, vv, bk: _dot_tn(jnp.concatenate([uu, vv], axis=0), bk), u, w_v, w_bk)
    for j, (i, q) in enumerate(units):
        s_s[i, q] = (s0[j] + upd[j]) * g_in[j][ch - 1:ch, :]
        y = y_w[j][0:ch] + y_w[j][ch:2 * ch] + y_w[j][2 * ch:3 * ch] + y_w[j][3 * ch:4 * ch]
        for p in range(gw // LANES):
            ps = slice(LANES * p, LANES * (p + 1))
            po = slice(gw * q + LANES * p, gw * q + LANES * (p + 1))
            o_ref[i, :, po] = _rwkv_finish(y[:, ps], r[j][:, ps], k[j][:, ps], v[j][:, ps], g_ref[i, :, po],
                                           rk_ref[:, po], lnw_ref[:, po], lnb_ref[:, po])

    @pl.when(c == pl.num_programs(1) - 1)
    def _():
        s_out_ref[...] = s_s[...]


def _rwkv_scan(feats, rk, lnw, lnb):
    b, t, _ = feats[0].shape
    ch = min(t, RWKV_CHUNK)
    assert t % ch == 0 and ch & (ch - 1) == 0
    bb = RWKV_SEQS_PER_STEP if b % RWKV_SEQS_PER_STEP == 0 else 1
    blk = pl.BlockSpec((bb, ch, RWKV_WIDTH), lambda i, c: (i, c, 0))
    one = _const_spec((1, RWKV_WIDTH))
    gw = 4 * HEAD_DIM
    n_groups = RWKV_WIDTH // gw
    return pl.pallas_call(
        _rwkv_scan_kernel,
        grid=(b // bb, t // ch),
        in_specs=[blk] * 7 + [one] * 3,
        out_specs=[blk, pl.BlockSpec((bb, n_groups, gw, gw), lambda i, c: (i, 0, 0, 0))],
        out_shape=[jax.ShapeDtypeStruct((b, t, RWKV_WIDTH), F32),
                   jax.ShapeDtypeStruct((b, n_groups, gw, gw), F32)],
        scratch_shapes=[pltpu.VMEM((bb, n_groups, gw, gw), F32)],
        compiler_params=_params(("parallel", "arbitrary")),
        name="rwkv_scan",
    )(*feats, rk, lnw, lnb)


def _unblock_states(s_bd):
    hd = HEAD_DIM
    blocks = [s_bd[:, :, hd * h:hd * (h + 1), hd * h:hd * (h + 1)] for h in range(s_bd.shape[2] // hd)]
    return jnp.stack(blocks, axis=2).reshape(s_bd.shape[0], -1, hd, hd)


def _rwkv_step_kernel(s_ref, r_ref, lw_ref, k_ref, v_ref, a_ref, b_ref, g_ref, rk_ref, lnw_ref, lnb_ref,
                      o_ref, s_out_ref):
    hd = HEAD_DIM
    eye = _iota((hd, hd), 0) == _iota((hd, hd), 1)
    for h in range(RWKV_HEADS):
        for i in range(s_ref.shape[0]):
            s0 = s_ref[i, h]
            r, lw, k, v, a, b = (ref[i, h] for ref in (r_ref, lw_ref, k_ref, v_ref, a_ref, b_ref))
            v_col = jnp.sum(jnp.where(eye, v, 0.0), axis=-1, keepdims=True)
            sa = jnp.sum(s0 * a, axis=-1, keepdims=True)
            s1 = s0 * jnp.exp(lw) + sa * b + v_col * k
            s_out_ref[i, h] = s1
            y_col = jnp.sum(s1 * r, axis=-1, keepdims=True)
            y = jnp.sum(jnp.where(eye, y_col, 0.0), axis=0, keepdims=True)
            mu = jnp.mean(y, axis=-1, keepdims=True)
            d = y - mu
            var = jnp.mean(d * d, axis=-1, keepdims=True)
            yn = d * lax.rsqrt(var + GN_EPS) * lnw_ref[h] + lnb_ref[h]
            bonus = jnp.sum(r * k * rk_ref[h], axis=-1, keepdims=True) * v
            o_ref[i, h] = (yn + bonus) * g_ref[i, h]


def _rwkv_step(state, feats, rk, lnw, lnb):
    m = state.shape[0]
    hd, nh = HEAD_DIM, RWKV_HEADS
    heads = lambda x: x.reshape(x.shape[:-1] + (nh, 1, hd))
    vec = pl.BlockSpec((1, nh, 1, hd), lambda i: (i, 0, 0, 0))
    st = pl.BlockSpec((1, nh, hd, hd), lambda i: (i, 0, 0, 0))
    par = _const_spec((nh, 1, hd))
    o, s1 = pl.pallas_call(
        _rwkv_step_kernel,
        grid=(m,),
        in_specs=[st] + [vec] * 7 + [par] * 3,
        out_specs=[vec, st],
        out_shape=[jax.ShapeDtypeStruct((m, nh, 1, hd), F32), jax.ShapeDtypeStruct((m, nh, hd, hd), F32)],
        compiler_params=_params(("parallel",)),
        name="rwkv_step",
    )(state, *[heads(f) for f in feats], heads(rk)[0], heads(lnw)[0], heads(lnb)[0])
    return o.reshape(m, nh * hd), s1


def _layer_params(i, norm_mix, w_in, q_norm, k_norm, cmp_pe, cmp_w1, cmp_b1, cmp_w2, cmp_b2, shift_mu, w0, w2,
                  a0, a2, g2, k_k, k_a, r_k, ln_w, ln_b, w_out, norm_ffn, w_gate, w_up, w_down, norm_ple,
                  w_ple_gate, w_ple):
    pair = lambda g: jnp.concatenate([g, g])[None]
    return dict(
        norm_mix=norm_mix[i][None], w_in=_pad_w_in(w_in[i]),
        qn=pair(q_norm[i]), kn_sel=pair(k_norm[i, 1]), kn_win=pair(k_norm[i, 2]),
        step_gains=(pair(q_norm[i]), k_norm[i, 1][:, None], k_norm[i, 2][:, None], k_norm[i, 1][None],
                    k_norm[i, 2][None]),
        cw=_compress_weights(cmp_pe[i], cmp_w1[i], cmp_b1[i], cmp_w2[i], cmp_b2[i], k_norm[i, 0]),
        rp=_rwkv_params(shift_mu[i], w0[i], w2[i], a0[i], a2[i], g2[i], k_k[i], k_a[i]),
        rk=r_k[i].reshape(1, RWKV_WIDTH), lnw=ln_w[i][None], lnb=ln_b[i][None],
        w_out=w_out[i].astype(BF16), norm_ffn=norm_ffn[i][None], w_gate=w_gate[i].astype(BF16),
        w_up=w_up[i].astype(BF16), w_down=w_down[i].astype(BF16), norm_ple=norm_ple[i][None],
        w_ple_gate=w_ple_gate[i].astype(BF16), w_ple=w_ple[i].astype(BF16))


def _prompt_layer(x, pe, lp, b, t):
    g, hd = NSA_KV_HEADS, HEAD_DIM
    q, kv, kw, gl, xr = _in_proj(x, lp["norm_mix"], lp["w_in"])
    kv = kv.reshape(b, t, KV_COLS)
    kw = kw.reshape(b, t, WIN_COLS)
    xr = xr.reshape(b, t, XR_PAD)
    kc = _compress_prompt(kv, lp["cw"])
    ksv = _kv_prep(kv, kw, lp["kn_sel"], lp["kn_win"])
    o_nsa = _nsa_prompt(q.reshape(b, t, Q_COLS), gl.reshape(b, t, LANES), lp["qn"], kc, ksv, t)
    feats = _rwkv_prep_seq(xr, jnp.zeros((b, 1, XR_PAD), F32), lp["rp"])
    o_rwkv, s_bd = _rwkv_scan(feats, lp["rk"], lp["lnw"], lp["lnb"])
    x = _out_ffn(x, o_nsa.reshape(b * t, Q_COLS), o_rwkv.reshape(b * t, RWKV_WIDTH), pe, lp)
    keep = min(WINDOW, t)
    return (x, kv.reshape(b, t, 4, g, hd), kw[:, t - keep:].reshape(b, keep, 2, g, hd), _unblock_states(s_bd),
            _unpad_xr(xr[:, -1]))


def _sample_layer(x, pe, lp, cache_t, pt_flat, n_pages, cwin_t, win_off, state_wkv, state_shift):
    db = x.shape[0]
    g, hd = NSA_KV_HEADS, HEAD_DIM
    q, kv, kw, gl, xr = _in_proj(x, lp["norm_mix"], lp["w_in"])
    kcn, _, vcn, _ = _compress_paged(cache_t, pt_flat, db, n_pages, lp["cw"])
    q3 = q.reshape(db, NSA_HEADS, hd)
    z = jnp.zeros((db, NSA_HEADS // 2, hd), F32)
    qm = jnp.concatenate([jnp.concatenate([q3[:, :4], z], -1), jnp.concatenate([z, q3[:, 4:]], -1)], axis=1)
    glm = jnp.pad(gl[:, :GATE_COLS].reshape(db, NSA_HEADS, 3), ((0, 0), (0, 0), (0, LANES - 3)))
    o8, win = _nsa_step(cache_t, pt_flat, qm, glm, kcn, vcn, kv.reshape(db, 4 * g, hd), cwin_t, win_off,
                        kw.reshape(db, 2 * g, hd), lp["step_gains"], n_pages)
    o_nsa = o8.reshape(db, Q_COLS)
    feats = _rwkv_prep_step(xr, _pad_xr(state_shift), lp["rp"])
    o_rwkv, s1 = _rwkv_step(state_wkv, feats, lp["rk"], lp["lnw"], lp["lnb"])
    x = _out_ffn(x, o_nsa, o_rwkv, pe, lp)
    return (x, kv.reshape(db, 1, 4, g, hd), win, s1, _unpad_xr(xr))


def kernel(x_prompt, x_sample, p_prompt, p_sample, cache_kv, cache_win, state_wkv, state_shift, page_table, norm_mix, w_in, q_norm, k_norm, cmp_pe, cmp_w1, cmp_b1, cmp_w2, cmp_b2, shift_mu, w0, w2, a0, a2, g2, k_k, k_a, r_k, ln_w, ln_b, w_out, norm_ffn, w_gate, w_up, w_down, norm_ple, w_ple_gate, w_ple):
    b, t, d = x_prompt.shape
    db = x_sample.shape[0]
    depth, n_phys, page = cache_kv.shape[:3]
    n_pages = page_table.shape[1]
    assert x_sample.shape[1] == 1 and d == D_MODEL
    rows_last = (0, 1, 3, 4, 5, 2)
    cache_t = cache_kv.transpose(rows_last).reshape(depth * n_phys, -1, HEAD_DIM, page)
    wbuf = cache_win.shape[2]
    cwin_t = cache_win.transpose(rows_last).reshape(depth * db, -1, HEAD_DIM, wbuf)
    pt_flat = page_table.reshape(-1).astype(jnp.int32)
    xp = x_prompt.reshape(b * t, d)
    xs = x_sample.reshape(db, d)
    outs_p, outs_s = [], []
    for i in range(depth):
        lp = _layer_params(i, norm_mix, w_in, q_norm, k_norm, cmp_pe, cmp_w1, cmp_b1, cmp_w2, cmp_b2, shift_mu,
                           w0, w2, a0, a2, g2, k_k, k_a, r_k, ln_w, ln_b, w_out, norm_ffn, w_gate, w_up, w_down,
                           norm_ple, w_ple_gate, w_ple)
        xp, *rest_p = _prompt_layer(xp, p_prompt[i].reshape(b * t, -1), lp, b, t)
        xs, *rest_s = _sample_layer(xs, p_sample[i].reshape(db, -1), lp, cache_t, pt_flat + i * n_phys, n_pages,
                                    cwin_t, i * db, state_wkv[i], state_shift[i])
        outs_p.append(rest_p)
        outs_s.append(rest_s)
    stack = lambda outs, k: jnp.stack([o[k] for o in outs])
    g = NSA_KV_HEADS
    win_s = stack(outs_s, 1).reshape(depth, db, 2, g, HEAD_DIM, wbuf).transpose(0, 1, 5, 2, 3, 4)
    return (xp.reshape(b, t, d), xs.reshape(db, 1, d),
            stack(outs_p, 0), stack(outs_s, 0), stack(outs_p, 1), win_s,
            stack(outs_p, 2), stack(outs_s, 2), stack(outs_p, 3), stack(outs_s, 3))
```

```python
import functools
import itertools

import jax
import jax.numpy as jnp
from jax import lax
from jax.experimental import pallas as pl
from jax.experimental.pallas import tpu as pltpu

F32 = jnp.float32
BF16 = jnp.bfloat16
HI = lax.Precision.HIGHEST

D_MODEL = 1024
HEAD_DIM = 64
LANES = 128
NSA_HEADS = 8
NSA_KV_HEADS = 2
RWKV_HEADS = 8
RWKV_WIDTH = RWKV_HEADS * HEAD_DIM
CMP_BLOCK = 32
CMP_STRIDE = 16
CMP_HIDDEN = 256
SEL_BLOCK = 64
SEL_TOPN = 16
WINDOW = 512
Q_BLOCK = 512
COMPRESS_SEQS = 2
NSA_STEP_SEQS = 4
WINDOW_Q = 512
DECAY_RANK = 64
A_RANK = 64
GATE_RANK = 128
Q_COLS = NSA_HEADS * HEAD_DIM
KV_COLS = 4 * NSA_KV_HEADS * HEAD_DIM
WIN_COLS = 2 * NSA_KV_HEADS * HEAD_DIM
GATE_COLS = 3 * NSA_HEADS
RWKV_PROJ = 3 * RWKV_WIDTH + DECAY_RANK + A_RANK + GATE_RANK
XR_PAD = 3 * RWKV_WIDTH + 3 * LANES
IN_PAD = Q_COLS + KV_COLS + WIN_COLS + LANES + XR_PAD
NORM_EPS = 1e-6
GN_EPS = 64e-5
BIG = 1e9
NEG = -1e30
RWKV_CHUNK = 64
RWKV_SEQS_PER_STEP = 4
VMEM_LIMIT =56 * 1024 * 1024


def _dot(a, b, prec=None):
    return lax.dot_general(a, b, (((1,), (0,)), ((), ())), precision=prec, preferred_element_type=F32)


def _dot_nt(a, b, prec=None):
    return lax.dot_general(a, b, (((1,), (1,)), ((), ())), precision=prec, preferred_element_type=F32)


def _dot_tn(a, b, prec=None):
    return lax.dot_general(a, b, (((0,), (0,)), ((), ())), precision=prec, preferred_element_type=F32)


def _iota(shape, dim):
    return lax.broadcasted_iota(jnp.int32, shape, dim)


def _lo_mask(shape):
    return (_iota(shape, len(shape) - 1) % LANES) < HEAD_DIM


def _seg_sum(x):
    lo = _lo_mask(x.shape)
    s_lo = jnp.sum(jnp.where(lo, x, 0.0), axis=-1, keepdims=True)
    s_hi = jnp.sum(jnp.where(lo, 0.0, x), axis=-1, keepdims=True)
    return jnp.where(lo, s_lo, s_hi)


def _seg_rmsnorm(x, gain):
    ms = _seg_sum(x * x) * (1.0 / HEAD_DIM)
    return x * lax.rsqrt(ms + NORM_EPS) * gain


def _rmsnorm_rows(x, gain):
    return x * lax.rsqrt(jnp.mean(x * x, axis=-1, keepdims=True) + NORM_EPS) * gain


def _sigmoid(x):
    return 1.0 / (1.0 + jnp.exp(-x))


def _masked_softmax(s, mask):
    sm = jnp.where(mask, s, NEG)
    m = jnp.max(sm, axis=-1, keepdims=True)
    m = jnp.where(m > 0.5 * NEG, m, 0.0)
    e = jnp.where(mask, jnp.exp(sm - m), 0.0)
    return e * (1.0 / jnp.maximum(jnp.sum(e, axis=-1, keepdims=True), 1e-30))


def _softmax_av(s, bias, v):
    s = s + bias
    e = jnp.exp(s - jnp.max(s, axis=-1, keepdims=True))
    return _dot(e.astype(BF16), v) * (1.0 / jnp.sum(e, axis=-1, keepdims=True))


def _params(sem):
    return pltpu.CompilerParams(dimension_semantics=sem, vmem_limit_bytes=VMEM_LIMIT)


def _const_spec(shape):
    nd = len(shape)
    return pl.BlockSpec(shape, lambda *_: (0,) * nd)


_IN_SPLITS = (("q", 0, Q_COLS), ("kv", Q_COLS, KV_COLS), ("kw", Q_COLS + KV_COLS, WIN_COLS),
              ("gl", Q_COLS + KV_COLS + WIN_COLS, LANES),
              ("xr", Q_COLS + KV_COLS + WIN_COLS + LANES, XR_PAD))


def _in_proj_kernel(x_ref, g_ref, w_ref, *out_refs):
    h = _rmsnorm_rows(x_ref[...], g_ref[...]).astype(BF16)
    for o_ref, (_, lo, n) in zip(out_refs, _IN_SPLITS):
        o_ref[...] = _dot(h, w_ref[:, lo:lo + n])


def _in_proj(x, gain, w_pad):
    m = x.shape[0]
    tm = min(m, 512)
    assert m % tm == 0
    return pl.pallas_call(
        _in_proj_kernel,
        grid=(m // tm,),
        in_specs=[pl.BlockSpec((tm, D_MODEL), lambda i: (i, 0)),
                  _const_spec((1, D_MODEL)),
                  _const_spec((D_MODEL, IN_PAD))],
        out_specs=[pl.BlockSpec((tm, n), lambda i: (i, 0)) for _, _, n in _IN_SPLITS],
        out_shape=[jax.ShapeDtypeStruct((m, n), F32) for _, _, n in _IN_SPLITS],
        compiler_params=_params(("parallel",)),
        name="in_proj",
    )(x, gain, w_pad)


def _pad_xr(v):
    z = jnp.zeros(v.shape[:-1] + (LANES - DECAY_RANK,), v.dtype)
    c = 3 * RWKV_WIDTH
    return jnp.concatenate([v[..., :c + DECAY_RANK], z, v[..., c + DECAY_RANK:c + DECAY_RANK + A_RANK], z,
                            v[..., c + DECAY_RANK + A_RANK:]], axis=-1)


def _unpad_xr(v):
    c = 3 * RWKV_WIDTH
    return jnp.concatenate([v[..., :c + DECAY_RANK], v[..., c + LANES:c + LANES + A_RANK], v[..., c + 2 * LANES:]],
                           axis=-1)


def _pad_w_in(w_in):
    c = Q_COLS + KV_COLS + WIN_COLS
    gl = jnp.pad(w_in[:, c:c + GATE_COLS], ((0, 0), (0, LANES - GATE_COLS)))
    return jnp.concatenate([w_in[:, :c], gl, _pad_xr(w_in[:, c + GATE_COLS:])], axis=-1).astype(BF16)


def _ffn_kernel(x_ref, on_ref, or_ref, pe_ref, wout_ref, nf_ref, wg_ref, wu_ref, wd_ref, np_ref, wpg_ref,
                wple_ref, y_ref, x1_s, hf_s, acc_s):
    j = pl.program_id(1)

    @pl.when(j == 0)
    def _():
        half = D_MODEL // 2
        o = _dot(on_ref[...].astype(BF16), wout_ref[0:half, :]) + _dot(or_ref[...].astype(BF16), wout_ref[half:, :])
        x1 = x_ref[...] + o
        x1_s[...] = x1
        hf_s[...] = _rmsnorm_rows(x1, nf_ref[...]).astype(BF16)
        acc_s[...] = jnp.zeros_like(acc_s)

    hf = hf_s[...]
    g = _dot(hf, wg_ref[...])
    u = _dot(hf, wu_ref[...])
    act = (g * _sigmoid(g)) * u
    acc_s[...] += _dot(act.astype(BF16), wd_ref[...])

    @pl.when(j == pl.num_programs(1) - 1)
    def _():
        x2 = x1_s[...] + acc_s[...]
        hp = _rmsnorm_rows(x2, np_ref[...]).astype(BF16)
        pg = _sigmoid(_dot(hp, wpg_ref[...]))
        y_ref[...] = x2 + pg * _dot(pe_ref[...].astype(BF16), wple_ref[...])


def _out_ffn(x, o_nsa, o_rwkv, pe, lw):
    m = x.shape[0]
    tm = min(m, 512)
    hid = lw["w_gate"].shape[1]
    th = 1408
    assert m % tm == 0 and hid % th == 0
    ple = pe.shape[1]
    row = lambda i, j: (i, 0)
    return pl.pallas_call(
        _ffn_kernel,
        grid=(m // tm, hid // th),
        in_specs=[pl.BlockSpec((tm, D_MODEL), row),
                  pl.BlockSpec((tm, D_MODEL // 2), row),
                  pl.BlockSpec((tm, D_MODEL // 2), row),
                  pl.BlockSpec((tm, ple), row),
                  _const_spec((D_MODEL, D_MODEL)),
                  _const_spec((1, D_MODEL)),
                  pl.BlockSpec((D_MODEL, th), lambda i, j: (0, j)),
                  pl.BlockSpec((D_MODEL, th), lambda i, j: (0, j)),
                  pl.BlockSpec((th, D_MODEL), lambda i, j: (j, 0)),
                  _const_spec((1, D_MODEL)),
                  _const_spec((D_MODEL, D_MODEL)),
                  _const_spec((ple, D_MODEL))],
        out_specs=pl.BlockSpec((tm, D_MODEL), row),
        out_shape=jax.ShapeDtypeStruct((m, D_MODEL), F32),
        scratch_shapes=[pltpu.VMEM((tm, D_MODEL), F32), pltpu.VMEM((tm, D_MODEL), BF16),
                        pltpu.VMEM((tm, D_MODEL), F32)],
        compiler_params=_params(("parallel", "arbitrary")),
        name="out_ffn_ple",
    )(x, o_nsa, o_rwkv, pe, lw["w_out"], lw["norm_ffn"], lw["w_gate"], lw["w_up"], lw["w_down"],
      lw["norm_ple"], lw["w_ple_gate"], lw["w_ple"])


def _gelu_tanh(x):
    return 0.5 * x * (1.0 + jnp.tanh(0.7978845608028654 * (x + 0.044715 * x * x * x)))


def _compress_body(load_rows, nbp, first_step, pe_ref, wbd_ref, b1_ref, w2lo_ref, w2hi_ref, b2_ref, kn_ref,
                   kcn_ref, kcr_ref, vcn_ref, vcr_ref, bias_s):
    hid = CMP_HIDDEN

    @pl.when(first_step)
    def _():
        for ty in range(2):
            acc = jnp.zeros((1, hid), F32)
            for r2 in range(CMP_STRIDE // 2):
                t = _dot(pe_ref[ty, r2:r2 + 1, :].astype(BF16), wbd_ref[ty, r2])
                acc = acc + t[:, 0:hid] + t[:, 3 * hid:4 * hid]
            bias_s[ty:ty + 1, :] = acc + b1_ref[ty:ty + 1, :]

    acc = [jnp.zeros((nbp, 4 * hid), F32) for _ in range(2)]
    for r2 in range(CMP_STRIDE // 2):
        for ty in range(2):
            lhs = jnp.concatenate([load_rows(2 * r2, ty), load_rows(2 * r2 + 1, ty)], axis=-1).astype(BF16)
            acc[ty] = acc[ty] + _dot(lhs, wbd_ref[ty, r2])
    outs = ((kcn_ref, kcr_ref), (vcn_ref, vcr_ref))
    for ty in range(2):
        hs = []
        for g in range(2):
            a = acc[ty][:, 2 * hid * g:2 * hid * g + hid]
            b = acc[ty][:, 2 * hid * g + hid:2 * hid * (g + 1)]
            b_next = pltpu.roll(b, nbp - 1, 0)
            hs.append(_gelu_tanh(a + b_next + bias_s[ty:ty + 1, :]).astype(BF16))
        o_n = _dot(hs[0], w2lo_ref[ty]) + _dot(hs[1], w2hi_ref[ty]) + b2_ref[ty:ty + 1, :]
        o_r = _dot(hs[0], w2hi_ref[ty]) + _dot(hs[1], w2lo_ref[ty]) + b2_ref[ty:ty + 1, :]
        if ty == 0:
            o_n = _seg_rmsnorm(o_n, kn_ref[...])
            o_r = _seg_rmsnorm(o_r, kn_ref[...])
        for o_ref, o in zip(outs[ty], (o_n, o_r)):
            o_ref[...] = o.astype(o_ref.dtype).reshape(o_ref.shape)


def _compress_prompt_kernel(k_ref, v_ref, *refs, nbp):
    load = lambda r, ty: (k_ref, v_ref)[ty][0, pl.ds(r, nbp, stride=CMP_STRIDE), :]
    _compress_body(load, nbp, pl.program_id(0) == 0, *refs)


def _compress_paged_kernel(pt_ref, *refs, n_pages, page):
    del pt_ref
    page_refs, rest, rows_s = refs[:n_pages], refs[n_pages:-2], refs[-2:]
    for j in range(n_pages):
        for ty in range(2):
            tiles = page_refs[j][0, 2 * ty:2 * ty + 2].reshape(2 * HEAD_DIM, page)
            rows_s[ty][j * page:(j + 1) * page, :] = tiles.T
    nbp = n_pages * page // CMP_STRIDE
    load = lambda r, ty: rows_s[ty][pl.ds(r, nbp, stride=CMP_STRIDE), :]
    _compress_body(load, nbp, pl.program_id(0) == 0, *rest)


def _compress_weight_specs():
    hid = CMP_HIDDEN
    return [_const_spec((2, CMP_STRIDE // 2, 2 * LANES)), _const_spec((2, CMP_STRIDE // 2, 2 * LANES, 4 * hid)),
            _const_spec((2, hid)), _const_spec((2, hid, LANES)), _const_spec((2, hid, LANES)),
            _const_spec((2, LANES)), _const_spec((1, LANES))]


def _compress_outs(b, nbp, per_step=1):
    spec = pl.BlockSpec((per_step, nbp, LANES), lambda i, *_: (i, 0, 0))
    shapes = [jax.ShapeDtypeStruct((b, nbp, LANES), dt) for dt in (F32, F32, BF16, BF16)]
    return [spec] * 4, shapes


def _compress_prompt(kv, cw):
    b, t, _ = kv.shape
    nbp = t // CMP_STRIDE
    out_specs, out_shape = _compress_outs(b, nbp)
    return pl.pallas_call(
        functools.partial(_compress_prompt_kernel, nbp=nbp),
        grid=(b,),
        in_specs=[pl.BlockSpec((1, t, LANES), lambda i: (i, 0, 0)),
                  pl.BlockSpec((1, t, LANES), lambda i: (i, 0, 1))] + _compress_weight_specs(),
        out_specs=out_specs, out_shape=out_shape,
        scratch_shapes=[pltpu.VMEM((2, CMP_HIDDEN), F32)],
        compiler_params=_params(("arbitrary",)),
        name="compress_prompt",
    )(kv, kv, *cw)


def _compress_paged(cache, pt_flat, n_seq, n_pages, cw):
    page = cache.shape[3]
    assert page == LANES
    nbp = n_pages * page // CMP_STRIDE
    sq = COMPRESS_SEQS if n_seq % COMPRESS_SEQS == 0 else 1
    out_specs, out_shape = _compress_outs(n_seq, nbp, sq)
    page_specs = [pl.BlockSpec((1, 4, HEAD_DIM, page), functools.partial(
        lambda i, pt, j: (pt[i * sq * n_pages + j], 0, 0, 0), j=j)) for j in range(sq * n_pages)]
    return pl.pallas_call(
        functools.partial(_compress_paged_kernel, n_pages=sq * n_pages, page=page),
        grid_spec=pltpu.PrefetchScalarGridSpec(
            num_scalar_prefetch=1, grid=(n_seq // sq,),
            in_specs=page_specs + _compress_weight_specs(),
            out_specs=out_specs,
            scratch_shapes=[pltpu.VMEM((2, CMP_HIDDEN), F32), pltpu.VMEM((sq * n_pages * page, LANES), F32),
                            pltpu.VMEM((sq * n_pages * page, LANES), F32)]),
        out_shape=out_shape,
        compiler_params=_params(("arbitrary",)),
        name="compress_paged",
    )(pt_flat, *([cache] * (sq * n_pages)), *cw)


def _compress_weights(cmp_pe, cmp_w1, cmp_b1, cmp_w2, cmp_b2, k_norm0):
    hid = CMP_HIDDEN
    w1 = cmp_w1.reshape(2, CMP_BLOCK, HEAD_DIM, hid)
    wcat = jnp.concatenate([w1[:, :CMP_STRIDE], w1[:, CMP_STRIDE:]], axis=-1)
    z = jnp.zeros_like(wcat)
    wbd = jnp.concatenate([jnp.concatenate([wcat, z], -1), jnp.concatenate([z, wcat], -1)], axis=2).astype(BF16)
    wbd = wbd.reshape(2, CMP_STRIDE // 2, 2 * LANES, 4 * hid)
    pe_pair = jnp.concatenate([cmp_pe[:, :CMP_STRIDE], cmp_pe[:, CMP_STRIDE:]], axis=-1)
    pe_pair = pe_pair.reshape(2, CMP_STRIDE // 2, 2 * LANES)
    z2 = jnp.zeros_like(cmp_w2)
    w2lo = jnp.concatenate([cmp_w2, z2], -1).astype(BF16)
    w2hi = jnp.concatenate([z2, cmp_w2], -1).astype(BF16)
    b2 = jnp.concatenate([cmp_b2, cmp_b2], -1)
    kn = jnp.concatenate([k_norm0, k_norm0])[None]
    return (pe_pair, wbd, cmp_b1, w2lo, w2hi, b2, kn)


def _kv_prep_kernel(kv_ref, kw_ref, kns_ref, knw_ref, ksn_ref, ksr_ref, vsn_ref, vsr_ref, kwn_ref, kwr_ref,
                    vwn_ref, vwr_ref):
    def put(x, n_ref, r_ref):
        n_ref[0] = x.astype(BF16)
        r_ref[0] = pltpu.roll(x, HEAD_DIM, 1).astype(BF16)

    put(_seg_rmsnorm(kv_ref[0, :, 2 * LANES:3 * LANES], kns_ref[...]), ksn_ref, ksr_ref)
    put(kv_ref[0, :, 3 * LANES:4 * LANES], vsn_ref, vsr_ref)
    put(_seg_rmsnorm(kw_ref[0, :, 0:LANES], knw_ref[...]), kwn_ref, kwr_ref)
    put(kw_ref[0, :, LANES:2 * LANES], vwn_ref, vwr_ref)


def _kv_prep(kv, kw, kn_sel, kn_win):
    b, t, _ = kv.shape
    tm = min(t, 512)
    spec = lambda n: pl.BlockSpec((1, tm, n), lambda i, j: (i, j, 0))
    return pl.pallas_call(
        _kv_prep_kernel,
        grid=(b, t // tm),
        in_specs=[spec(4 * LANES), spec(2 * LANES), _const_spec((1, LANES)), _const_spec((1, LANES))],
        out_specs=[spec(LANES)] * 8,
        out_shape=[jax.ShapeDtypeStruct((b, t, LANES), BF16)] * 8,
        compiler_params=_params(("parallel", "parallel")),
        name="kv_prep",
    )(kv, kw, kn_sel, kn_win)


def _overlap_t(ns, nbp):
    j = _iota((ns, nbp), 0) * SEL_BLOCK
    n = _iota((ns, nbp), 1) * CMP_STRIDE
    return jnp.where((n < j + SEL_BLOCK) & (n + CMP_BLOCK > j), 1.0, 0.0).astype(F32)


def _nsa_prompt_kernel(q_ref, gl_ref, qn_ref, kcn_ref, kcr_ref, vcn_ref, vcr_ref, ksn_ref, ksr_ref, vsn_ref,
                       vsr_ref, kwn_ref, kwr_ref, vwn_ref, vwr_ref, o_ref, *, seq, tk, wq, wk):
    i = pl.program_id(1)
    qb = Q_BLOCK
    nbp = kcn_ref.shape[1]
    nb = nbp - 1
    ns = seq // SEL_BLOCK
    n_sel = min(SEL_TOPN, ns)
    lo = _lo_mask((qb, LANES))
    pos = i * qb + _iota((qb, 1), 0)
    sig = _sigmoid(gl_ref[0])

    qp = [_seg_rmsnorm(q_ref[0, :, LANES * p:LANES * (p + 1)], qn_ref[...]) * (HEAD_DIM ** -0.5) for p in range(4)]
    q_lo = [jnp.where(lo, x, 0.0) for x in qp]
    q_hi = [jnp.where(lo, 0.0, x) for x in qp]

    n_c = _iota((qb, nbp), 1)
    mask_c = (n_c * CMP_STRIDE + (CMP_BLOCK - 1) <= pos) & (n_c < nb)
    ov_t = _overlap_t(ns, nbp)
    j_t = _iota((ns, qb), 0)
    pos_t = i * qb + _iota((ns, qb), 1)
    cur_t = pos_t >> 6
    forced_t = (j_t == 0) | (j_t == cur_t) | (j_t == cur_t - 1)
    causal_t = j_t * SEL_BLOCK <= pos_t
    eye = jnp.where(_iota((qb, qb), 0) == _iota((qb, qb), 1), 1.0, 0.0).astype(BF16)

    n_tiles = (i * qb + qb - 1) // tk + 1

    order = lambda g, n, r: (n, r) if g == 0 else (r, n)
    heads = []
    for p in range(4):
        heads += [(p, p // 2, q_lo[p], 0), (p, p // 2, q_hi[p], 1)]

    split = lambda x: (x.astype(BF16), (x - x.astype(BF16).astype(F32)).astype(BF16))
    kc_parts = [split(kcn_ref[0]), split(kcr_ref[0])]

    def score3(qt, kc):
        q_h, q_l = split(qt)
        return _dot_nt(q_h, kc[0]) + (_dot_nt(q_h, kc[1]) + _dot_nt(q_l, kc[0]))

    p_cmp = [_masked_softmax(score3(qt, order(g, *kc_parts)[w]), mask_c) for _, g, qt, w in heads]
    o_cmp = [_dot(pc.astype(BF16), order(g, vcn_ref, vcr_ref)[w][0]) for pc, (_, g, _, w) in zip(p_cmp, heads)]
    unsel = []
    for g in range(2):
        psum = p_cmp[4 * g] + p_cmp[4 * g + 1] + p_cmp[4 * g + 2] + p_cmp[4 * g + 3]
        imp_t = _dot_nt(ov_t, psum, HI)
        score = jnp.where(forced_t, BIG, jnp.where(causal_t, imp_t, -BIG))
        rank = jnp.zeros((ns, qb), jnp.int32)
        for jp in range(ns):
            row = score[jp:jp + 1, :]
            rank = rank + jnp.where((row > score) | ((row == score) & (j_t > jp)), 1, 0)
        unsel.append(_dot_nt(eye, jnp.where(rank < n_sel, 0.0, NEG).astype(BF16)).astype(BF16))

    q16 = [qt.astype(BF16) for _, _, qt, _ in heads]

    def tile(kt, carry, diagonal):
        k0 = pl.multiple_of(kt * tk, tk)
        blk = (k0 + _iota((ns, tk), 1)) >> 6
        expand = jnp.where(_iota((ns, tk), 0) == blk, 1.0, 0.0).astype(BF16)
        bias = [_dot(u, expand) for u in unsel]
        if diagonal:
            causal = k0 + _iota((qb, tk), 1) <= pos
            bias = [jnp.where(causal, x, NEG) for x in bias]
        out = []
        for (m, l, acc), q_, (_, g, _, w) in zip(carry, q16, heads):
            k_ref = order(g, ksn_ref, ksr_ref)[w]
            v_ref = order(g, vsn_ref, vsr_ref)[w]
            s = _dot_nt(q_, k_ref[0, pl.ds(k0, tk), :]) + bias[g]
            m_new = jnp.maximum(m, jnp.max(s, axis=-1, keepdims=True))
            alpha = jnp.exp(m - m_new)
            e = jnp.exp(s - m_new)
            l_new = alpha * l + jnp.sum(e, axis=-1, keepdims=True)
            acc_new = alpha * acc + _dot(e.astype(BF16), v_ref[0, pl.ds(k0, tk), :])
            out.append((m_new, l_new, acc_new))
        return tuple(out)

    init = (jnp.full((qb, 1), NEG, F32), jnp.zeros((qb, 1), F32), jnp.zeros((qb, LANES), F32))
    carry = lax.fori_loop(0, n_tiles - 1, functools.partial(tile, diagonal=False), (init,) * len(heads))
    o_sel = [acc / l for _, l, acc in tile(n_tiles - 1, carry, True)]

    o_win = [[] for _ in heads]
    for sub in range(qb // wq):
        w_start = pl.multiple_of(jnp.maximum(i * qb + sub * wq - (wk - wq), 0), wq)
        w_dist = pos[sub * wq:(sub + 1) * wq] - (w_start + _iota((wq, wk), 1))
        bias_w = jnp.where((w_dist >= 0) & (w_dist <= WINDOW), 0.0, NEG)
        for n, (q_, (_, g, _, w)) in enumerate(zip(q16, heads)):
            o_win[n].append(_softmax_av(
                _dot_nt(q_[sub * wq:(sub + 1) * wq], order(g, kwn_ref, kwr_ref)[w][0, pl.ds(w_start, wk), :]), bias_w,
                order(g, vwn_ref, vwr_ref)[w][0, pl.ds(w_start, wk), :]))
    o_win = [jnp.concatenate(parts, axis=0) for parts in o_win]

    for p in range(4):
        gate = lambda c: jnp.where(lo, sig[:, 6 * p + c:6 * p + c + 1], sig[:, 6 * p + 3 + c:6 * p + 4 + c])
        both = lambda o: jnp.where(lo, o[2 * p], o[2 * p + 1])
        o_ref[0, :, LANES * p:LANES * (p + 1)] = gate(0) * both(o_cmp) + gate(1) * both(o_sel) + gate(2) * both(o_win)


def _nsa_prompt(q, gl, qn, kc, ksv, seq):
    b = q.shape[0]
    nbp = kc[0].shape[1]
    tk = min(seq, 512)
    wq = min(Q_BLOCK, WINDOW_Q)
    wk = min(seq, WINDOW + wq)
    assert seq % Q_BLOCK == 0 and seq % tk == 0
    blk = lambda n: pl.BlockSpec((1, Q_BLOCK, n), lambda bi, i: (bi, i, 0))
    full = lambda rows: pl.BlockSpec((1, rows, LANES), lambda bi, i: (bi, 0, 0))
    return pl.pallas_call(
        functools.partial(_nsa_prompt_kernel, seq=seq, tk=tk, wq=wq, wk=wk),
        grid=(b, seq // Q_BLOCK),
        in_specs=[blk(4 * LANES), blk(LANES), _const_spec((1, LANES))] + [full(nbp)] * 4 + [full(seq)] * 8,
        out_specs=blk(4 * LANES),
        out_shape=jax.ShapeDtypeStruct((b, seq, 4 * LANES), F32),
        compiler_params=_params(("parallel", "arbitrary")),
        name="nsa_prompt",
    )(q, gl, qn, *kc, *ksv)


def _nsa_step_kernel(pt_ref, *refs, n_pages, page, past):
    del pt_ref
    n_page_refs = len(refs) - 14
    page_refs = refs[:n_page_refs]
    (qm_ref, gl_ref, kcn_ref, vcn_ref, kvn_ref, cw_ref, kwn_ref, qn_ref, kns_ref, knw_ref, kns_row_ref, knw_row_ref,
     o_ref, win_ref) = refs[n_page_refs:]
    nh = NSA_HEADS
    hd = HEAD_DIM
    nbp = kcn_ref.shape[1]
    nb = nbp - 1
    ns = past // SEL_BLOCK + 1
    n_sel = min(SEL_TOPN, ns)
    wbuf = cw_ref.shape[3]
    rep = NSA_HEADS // NSA_KV_HEADS
    first_group = _iota((nh, 1), 0) < rep

    def unpair(x):
        return jnp.where(first_group, x, pltpu.roll(x, hd, 1))[:, 0:hd]

    def by_group(f):
        return jnp.where(first_group, f(0), f(1))

    def norm_t(t, gain_ref):
        return (t * lax.rsqrt(jnp.mean(t * t, axis=0, keepdims=True) + NORM_EPS) * gain_ref[...]).astype(BF16)

    def one(s):
        qn = _seg_rmsnorm(qm_ref[s], qn_ref[...]) * (HEAD_DIM ** -0.5)
        q8 = unpair(qn)
        q8b = q8.astype(BF16)
        sg = _sigmoid(gl_ref[s])
        yield

        n_c = _iota((nh, nbp), 1)
        mask_c = (n_c * CMP_STRIDE + (CMP_BLOCK - 1) <= past) & (n_c < nb)
        p_c = _masked_softmax(_dot_nt(qn, kcn_ref[s], HI), mask_c)
        yield
        o_cmp = unpair(_dot(p_c.astype(BF16), vcn_ref[s]))
        same_group = jnp.where(_iota((nh, nh), 0) // rep == _iota((nh, nh), 1) // rep, 1.0, 0.0).astype(F32)
        psum = _dot(same_group, p_c, HI)
        yield
        imp = _dot_nt(psum, _overlap_t(LANES, nbp), HI)
        yield
        j = _iota((nh, LANES), 1)
        cur = past // SEL_BLOCK
        forced = (j == 0) | (j == cur) | (j == cur - 1)
        score = jnp.where(forced, BIG, jnp.where(j * SEL_BLOCK <= past, imp, -BIG))
        score = jnp.where(j < ns, score, -4.0 * BIG)
        rank = jnp.zeros((nh, LANES), jnp.int32)
        for jp in range(ns):
            col = score[:, jp:jp + 1]
            rank = rank + jnp.where((col > score) | ((col == score) & (j > jp)), 1, 0)
        sel = jnp.where((rank < n_sel) & (j < ns), 1.0, 0.0).astype(BF16)
        yield

        kv_new = kvn_ref[s]
        k_new = by_group(lambda g: _rmsnorm_rows(kv_new[4 + g:5 + g, :], kns_row_ref[...]))
        s_new = jnp.sum(q8 * k_new, axis=-1, keepdims=True)
        s_parts, m_parts = [], []
        for pj in range(n_pages):
            blk = (pj * page + _iota((LANES, page), 1)) >> 6
            expand = jnp.where(_iota((LANES, page), 0) == blk, 1.0, 0.0).astype(BF16)
            m_parts.append(_dot(sel, expand) > 0.5)
            s_parts.append(by_group(lambda g: _dot(q8b, norm_t(page_refs[s * n_pages + pj][0, g], kns_ref))))
        s_all = jnp.concatenate(s_parts, axis=-1)
        valid = jnp.concatenate(m_parts, axis=-1)
        s_all = jnp.where(valid, s_all, NEG)
        yield
        m = jnp.maximum(jnp.max(s_all, axis=-1, keepdims=True), s_new)
        e_all = jnp.where(valid, jnp.exp(s_all - m), 0.0)
        e_new = jnp.exp(s_new - m)
        yield
        acc = e_new * by_group(lambda g: kv_new[6 + g:7 + g, :])
        for pj in range(n_pages):
            e_pj = e_all[:, pj * page:(pj + 1) * page].astype(BF16)
            acc = acc + by_group(lambda g: _dot_nt(e_pj, page_refs[s * n_pages + pj][0, 2 + g].astype(BF16)))
        o_sel = acc / (jnp.sum(e_all, axis=-1, keepdims=True) + e_new)
        yield

        kw_new = kwn_ref[s]
        s_w = by_group(lambda g: _dot(q8b, norm_t(cw_ref[s, g], knw_ref)))
        s_wn = jnp.sum(q8 * by_group(lambda g: _rmsnorm_rows(kw_new[g:g + 1, :], knw_row_ref[...])), axis=-1,
                       keepdims=True)
        m_w = jnp.maximum(jnp.max(s_w, axis=-1, keepdims=True), s_wn)
        e_w = jnp.exp(s_w - m_w)
        e_wn = jnp.exp(s_wn - m_w)
        e_wb = e_w.astype(BF16)
        yield
        o_win = (by_group(lambda g: _dot_nt(e_wb, cw_ref[s, 2 + g].astype(BF16))) + e_wn * by_group(
            lambda g: kw_new[2 + g:3 + g, :])) / (jnp.sum(e_w, axis=-1, keepdims=True) + e_wn)

        o_ref[s] = sg[:, 0:1] * o_cmp + sg[:, 1:2] * o_sel + sg[:, 2:3] * o_win

        eye = _iota((hd, hd), 0) == _iota((hd, hd), 1)
        last = _iota((hd, wbuf), 1) == wbuf - 1
        for c in range(4):
            col = jnp.sum(jnp.where(eye, kw_new[c:c + 1, :], 0.0), axis=1, keepdims=True)
            win_ref[s, c] = jnp.where(last, col, pltpu.roll(cw_ref[s, c], wbuf - 1, 1))

    for _ in itertools.zip_longest(*[one(s) for s in range(qm_ref.shape[0])]):
        pass


def _nsa_step(cache, pt_flat, qm, glm, kcn, vcn, kv_new, cache_win, win_off, kw_new, gains, n_pages):
    db = qm.shape[0]
    page = cache.shape[3]
    past = n_pages * page
    nbp = kcn.shape[1]
    wbuf = cache_win.shape[3]
    hd = HEAD_DIM
    assert wbuf == min(WINDOW, past) and page == LANES and past // SEL_BLOCK + 1 <= LANES
    sq = NSA_STEP_SEQS if db % NSA_STEP_SEQS == 0 and win_off % NSA_STEP_SEQS == 0 else 1
    per = lambda shape: pl.BlockSpec((sq,) + shape, lambda i, pt: (i,) + (0,) * len(shape))
    page_specs = [pl.BlockSpec((1, 4, hd, page), functools.partial(
        lambda i, pt, s, j: (pt[(i * sq + s) * n_pages + j], 1, 0, 0), s=s, j=j))
        for s in range(sq) for j in range(n_pages)]
    return pl.pallas_call(
        functools.partial(_nsa_step_kernel, n_pages=n_pages, page=page, past=past),
        grid_spec=pltpu.PrefetchScalarGridSpec(
            num_scalar_prefetch=1, grid=(db // sq,),
            in_specs=page_specs + [per((NSA_HEADS, LANES)), per((NSA_HEADS, LANES)), per((nbp, LANES)),
                                   per((nbp, LANES)), per((8, hd)),
                                   pl.BlockSpec((sq, 4, hd, wbuf), lambda i, pt: (i + win_off // sq, 0, 0, 0)),
                                   per((4, hd))]
            + [_const_spec((1, LANES)), _const_spec((hd, 1)), _const_spec((hd, 1)), _const_spec((1, hd)),
               _const_spec((1, hd))],
            out_specs=[per((NSA_HEADS, hd)), per((4, hd, wbuf))]),
        out_shape=[jax.ShapeDtypeStruct((db, NSA_HEADS, hd), F32),
                   jax.ShapeDtypeStruct((db, 4, hd, wbuf), F32)],
        compiler_params=_params(("parallel",)),
        name="nsa_step",
    )(pt_flat, *([cache] * (sq * n_pages)), qm, glm, kcn, vcn, kv_new, cache_win, kw_new, *gains)


def _softplus(z):
    return jnp.maximum(z, 0.0) + jnp.log(1.0 + jnp.exp(-jnp.abs(z)))


def _rwkv_features(x, prev, mu_ref, w0_ref, a0_ref, w2_ref, a2_ref, g2_ref, kk_ref, ka_ref):
    rw = RWKV_WIDTH
    xs = x + (prev - x) * mu_ref[...]
    r, k, v = xs[:, 0:rw], xs[:, rw:2 * rw], xs[:, 2 * rw:3 * rw]
    wd = xs[:, 3 * rw:3 * rw + LANES]
    ad = xs[:, 3 * rw + LANES:3 * rw + 2 * LANES]
    gd = xs[:, 3 * rw + 2 * LANES:3 * rw + 3 * LANES]
    w_log = -_softplus(-(w0_ref[...] + _dot(jnp.tanh(wd).astype(BF16), w2_ref[...]))) - 0.5
    a_sig = _sigmoid(a0_ref[...] + _dot(ad.astype(BF16), a2_ref[...]))
    kk = k * kk_ref[...]
    kk2 = kk * kk
    norm2 = jnp.concatenate([_seg_sum(kk2[:, LANES * p:LANES * (p + 1)]) for p in range(rw // LANES)], axis=-1)
    kk = kk / jnp.maximum(jnp.sqrt(norm2), 1e-12)
    return (r, -jnp.exp(w_log), k * (1.0 + (a_sig - 1.0) * ka_ref[...]), v, -kk, kk * a_sig,
            _dot(_sigmoid(gd).astype(BF16), g2_ref[...]))


def _rwkv_prep_step_kernel(xr_ref, shift_ref, *refs):
    for o_ref, val in zip(refs[8:15], _rwkv_features(xr_ref[...], shift_ref[...], *refs[:8])):
        o_ref[...] = val


def _rwkv_param_specs():
    one = _const_spec((1, RWKV_WIDTH))
    low = _const_spec((LANES, RWKV_WIDTH))
    return [_const_spec((1, XR_PAD)), one, one, low, low, low, one, one]


def _rwkv_prep_step(xr, shift0, rp):
    m = xr.shape[0]
    return pl.pallas_call(
        _rwkv_prep_step_kernel,
        grid=(1,),
        in_specs=[_const_spec((m, XR_PAD)), _const_spec((m, XR_PAD))] + _rwkv_param_specs(),
        out_specs=[_const_spec((m, RWKV_WIDTH))] * 7,
        out_shape=[jax.ShapeDtypeStruct((m, RWKV_WIDTH), F32)] * 7,
        compiler_params=_params(("arbitrary",)),
        name="rwkv_prep_step",
    )(xr, shift0, *rp)


def _rwkv_params(shift_mu, w0, w2, a0, a2, g2, k_k, k_a):
    pad = lambda w: jnp.pad(w, ((0, LANES - w.shape[0]), (0, 0))).astype(BF16)
    return (_pad_xr(shift_mu)[None], w0[None], a0[None], pad(w2), pad(a2), g2.astype(BF16), k_k[None], k_a[None])


def _rwkv_finish(y, r, k, v, gate, rk, lnw, lnb):
    mu = _seg_sum(y) * (1.0 / HEAD_DIM)
    d = y - mu
    var = _seg_sum(d * d) * (1.0 / HEAD_DIM)
    yn = d * lax.rsqrt(var + GN_EPS) * lnw + lnb
    return (yn + _seg_sum(r * k * rk) * v) * gate


def _rwkv_scan_kernel(xr_ref, shift_ref, *refs):
    feat_params, (rk_ref, lnw_ref, lnb_ref, o_ref, s_out_ref, s_s, last_s) = refs[:8], refs[8:]
    c = pl.program_id(1)
    nseq, ch = xr_ref.shape[:2]
    gw = 4 * HEAD_DIM
    rows = 4 * ch

    @pl.when(c == 0)
    def _():
        s_s[...] = jnp.zeros_like(s_s)
        last_s[...] = shift_ref[...]

    x = xr_ref[...].reshape(nseq * ch, XR_PAD)
    prev = pltpu.roll(x, 1, 0)
    row = _iota(x.shape, 0)
    for i in range(nseq):
        prev = jnp.where(row == i * ch, last_s[i], prev)
        last_s[i] = x[(i + 1) * ch - 1:(i + 1) * ch, :]
    feats = _rwkv_features(x, prev, *feat_params)

    tril = jnp.where(_iota((ch, ch), 0) >= _iota((ch, ch), 1), 1.0, 0.0).astype(F32)
    t_row = _iota((rows, rows), 0) % ch
    t_col = _iota((rows, rows), 1) % ch
    strict = t_row > t_col
    incl = t_row >= t_col
    eye = jnp.where(_iota((rows, rows), 0) == _iota((rows, rows), 1), 1.0, 0.0).astype(F32)
    head_of_lane = _iota((ch, gw), 1) // HEAD_DIM

    def wide(x):
        return jnp.concatenate([jnp.where(head_of_lane == h, x, 0.0) for h in range(4)], axis=0).astype(BF16)

    units = [(i, q) for i in range(nseq) for q in range(RWKV_WIDTH // gw)]
    each = lambda f, *cols: [f(*xs) for xs in zip(*cols)]
    part = lambda f: [f[i * ch:(i + 1) * ch, gw * q:gw * (q + 1)] for i, q in units]
    r, lw, k, v, a, b, gate = (part(f) for f in feats)
    s0 = [s_s[i, q] for i, q in units]
    cl = each(lambda x: _dot(tril, x, HI), lw)
    g_in = each(jnp.exp, cl)
    g_inv = each(lambda x: jnp.exp(-x), cl)
    w_ar = each(lambda a_, r_, cl_, lw_, gi: jnp.concatenate([wide(a_ * jnp.exp(cl_ - lw_)), wide(r_ * gi)], axis=0),
                a, r, cl, lw, g_in)
    w_bk = each(lambda b_, k_, gv: jnp.concatenate([wide(b_ * gv), wide(k_ * gv)], axis=0), b, k, g_inv)
    w_v = each(wide, v)
    p_all = each(_dot_nt, w_ar, w_bk)
    xy0 = each(lambda x, s: _dot_nt(x, s.astype(BF16)), w_ar, s0)
    n = each(lambda p: jnp.where(strict, p[:rows, :rows], 0.0), p_all)
    l_ak = each(lambda p: jnp.where(strict, p[:rows, rows:], 0.0).astype(BF16), p_all)
    m_rb = each(lambda p: jnp.where(incl, p[rows:, :rows], 0.0).astype(BF16), p_all)
    m_rk = each(lambda p: jnp.where(incl, p[rows:, rows:], 0.0).astype(BF16), p_all)
    t_inv = each(lambda x: eye + x, n)
    pw = each(lambda x: x.astype(BF16), n)
    for _ in range(max(ch.bit_length() - 2, 0)):
        pw = each(lambda x: _dot(x, x).astype(BF16), pw)
        t_inv = each(lambda t, x: t + _dot(t.astype(BF16), x), t_inv, pw)
    x_w = each(lambda xy, l, vv: (xy[:rows] + _dot(l, vv)).astype(BF16), xy0, l_ak, w_v)
    u = each(lambda t, x: _dot(t.astype(BF16), x).astype(BF16), t_inv, x_w)
    y_w = each(lambda xy, mb, uu, mk, vv: xy[rows:] + _dot(mb, uu) + _dot(mk, vv), xy0, m_rb, u, m_rk, w_v)
    upd = each(lambda uu, vv, bk: _dot_tn(jnp.concatenate([uu, vv], axis=0), bk), u, w_v, w_bk)
    for j, (i, q) in enumerate(units):
        s_s[i, q] = (s0[j] + upd[j]) * g_in[j][ch - 1:ch, :]
        y = y_w[j][0:ch] + y_w[j][ch:2 * ch] + y_w[j][2 * ch:3 * ch] + y_w[j][3 * ch:4 * ch]
        for p in range(gw // LANES):
            ps = slice(LANES * p, LANES * (p + 1))
            po = slice(gw * q + LANES * p, gw * q + LANES * (p + 1))
            o_ref[i, :, po] = _rwkv_finish(y[:, ps], r[j][:, ps], k[j][:, ps], v[j][:, ps], gate[j][:, ps],
                                           rk_ref[:, po], lnw_ref[:, po], lnb_ref[:, po])

    @pl.when(c == pl.num_programs(1) - 1)
    def _():
        s_out_ref[...] = s_s[...]


def _rwkv_scan(xr, shift0, rp, rk, lnw, lnb):
    b, t, _ = xr.shape
    ch = min(t, RWKV_CHUNK)
    assert t % ch == 0 and ch & (ch - 1) == 0
    bb = RWKV_SEQS_PER_STEP if b % RWKV_SEQS_PER_STEP == 0 else 1
    blk = lambda n: pl.BlockSpec((bb, ch, n), lambda i, c: (i, c, 0))
    one = _const_spec((1, RWKV_WIDTH))
    gw = 4 * HEAD_DIM
    n_groups = RWKV_WIDTH // gw
    return pl.pallas_call(
        _rwkv_scan_kernel,
        grid=(b // bb, t // ch),
        in_specs=[blk(XR_PAD), pl.BlockSpec((bb, 1, XR_PAD), lambda i, c: (i, 0, 0))] + _rwkv_param_specs()
        + [one] * 3,
        out_specs=[blk(RWKV_WIDTH), pl.BlockSpec((bb, n_groups, gw, gw), lambda i, c: (i, 0, 0, 0))],
        out_shape=[jax.ShapeDtypeStruct((b, t, RWKV_WIDTH), F32),
                   jax.ShapeDtypeStruct((b, n_groups, gw, gw), F32)],
        scratch_shapes=[pltpu.VMEM((bb, n_groups, gw, gw), F32), pltpu.VMEM((bb, 1, XR_PAD), F32)],
        compiler_params=_params(("parallel", "arbitrary")),
        name="rwkv_scan",
    )(xr, shift0, *rp, rk, lnw, lnb)


def _unblock_states(s_bd):
    hd = HEAD_DIM
    blocks = [s_bd[:, :, hd * h:hd * (h + 1), hd * h:hd * (h + 1)] for h in range(s_bd.shape[2] // hd)]
    return jnp.stack(blocks, axis=2).reshape(s_bd.shape[0], -1, hd, hd)


def _rwkv_step_kernel(s_ref, r_ref, lw_ref, k_ref, v_ref, a_ref, b_ref, g_ref, rk_ref, lnw_ref, lnb_ref,
                      o_ref, s_out_ref):
    hd = HEAD_DIM
    eye = _iota((hd, hd), 0) == _iota((hd, hd), 1)
    for h in range(RWKV_HEADS):
        for i in range(s_ref.shape[0]):
            s0 = s_ref[i, h]
            r, lw, k, v, a, b = (ref[i, h] for ref in (r_ref, lw_ref, k_ref, v_ref, a_ref, b_ref))
            v_col = jnp.sum(jnp.where(eye, v, 0.0), axis=-1, keepdims=True)
            sa = jnp.sum(s0 * a, axis=-1, keepdims=True)
            s1 = s0 * jnp.exp(lw) + sa * b + v_col * k
            s_out_ref[i, h] = s1
            y_col = jnp.sum(s1 * r, axis=-1, keepdims=True)
            y = jnp.sum(jnp.where(eye, y_col, 0.0), axis=0, keepdims=True)
            mu = jnp.mean(y, axis=-1, keepdims=True)
            d = y - mu
            var = jnp.mean(d * d, axis=-1, keepdims=True)
            yn = d * lax.rsqrt(var + GN_EPS) * lnw_ref[h] + lnb_ref[h]
            bonus = jnp.sum(r * k * rk_ref[h], axis=-1, keepdims=True) * v
            o_ref[i, h] = (yn + bonus) * g_ref[i, h]


def _rwkv_step(state, feats, rk, lnw, lnb):
    m = state.shape[0]
    hd, nh = HEAD_DIM, RWKV_HEADS
    heads = lambda x: x.reshape(x.shape[:-1] + (nh, 1, hd))
    vec = pl.BlockSpec((1, nh, 1, hd), lambda i: (i, 0, 0, 0))
    st = pl.BlockSpec((1, nh, hd, hd), lambda i: (i, 0, 0, 0))
    par = _const_spec((nh, 1, hd))
    o, s1 = pl.pallas_call(
        _rwkv_step_kernel,
        grid=(m,),
        in_specs=[st] + [vec] * 7 + [par] * 3,
        out_specs=[vec, st],
        out_shape=[jax.ShapeDtypeStruct((m, nh, 1, hd), F32), jax.ShapeDtypeStruct((m, nh, hd, hd), F32)],
        compiler_params=_params(("parallel",)),
        name="rwkv_step",
    )(state, *[heads(f) for f in feats], heads(rk)[0], heads(lnw)[0], heads(lnb)[0])
    return o.reshape(m, nh * hd), s1


def _layer_params(i, norm_mix, w_in, q_norm, k_norm, cmp_pe, cmp_w1, cmp_b1, cmp_w2, cmp_b2, shift_mu, w0, w2,
                  a0, a2, g2, k_k, k_a, r_k, ln_w, ln_b, w_out, norm_ffn, w_gate, w_up, w_down, norm_ple,
                  w_ple_gate, w_ple):
    pair = lambda g: jnp.concatenate([g, g])[None]
    return dict(
        norm_mix=norm_mix[i][None], w_in=_pad_w_in(w_in[i]),
        qn=pair(q_norm[i]), kn_sel=pair(k_norm[i, 1]), kn_win=pair(k_norm[i, 2]),
        step_gains=(pair(q_norm[i]), k_norm[i, 1][:, None], k_norm[i, 2][:, None], k_norm[i, 1][None],
                    k_norm[i, 2][None]),
        cw=_compress_weights(cmp_pe[i], cmp_w1[i], cmp_b1[i], cmp_w2[i], cmp_b2[i], k_norm[i, 0]),
        rp=_rwkv_params(shift_mu[i], w0[i], w2[i], a0[i], a2[i], g2[i], k_k[i], k_a[i]),
        rk=r_k[i].reshape(1, RWKV_WIDTH), lnw=ln_w[i][None], lnb=ln_b[i][None],
        w_out=w_out[i].astype(BF16), norm_ffn=norm_ffn[i][None], w_gate=w_gate[i].astype(BF16),
        w_up=w_up[i].astype(BF16), w_down=w_down[i].astype(BF16), norm_ple=norm_ple[i][None],
        w_ple_gate=w_ple_gate[i].astype(BF16), w_ple=w_ple[i].astype(BF16))


def _prompt_layer(x, pe, lp, b, t):
    g, hd = NSA_KV_HEADS, HEAD_DIM
    q, kv, kw, gl, xr = _in_proj(x, lp["norm_mix"], lp["w_in"])
    kv = kv.reshape(b, t, KV_COLS)
    kw = kw.reshape(b, t, WIN_COLS)
    xr = xr.reshape(b, t, XR_PAD)
    kc = _compress_prompt(kv, lp["cw"])
    ksv = _kv_prep(kv, kw, lp["kn_sel"], lp["kn_win"])
    o_nsa = _nsa_prompt(q.reshape(b, t, Q_COLS), gl.reshape(b, t, LANES), lp["qn"], kc, ksv, t)
    o_rwkv, s_bd = _rwkv_scan(xr, jnp.zeros((b, 1, XR_PAD), F32), lp["rp"], lp["rk"], lp["lnw"], lp["lnb"])
    x = _out_ffn(x, o_nsa.reshape(b * t, Q_COLS), o_rwkv.reshape(b * t, RWKV_WIDTH), pe, lp)
    keep = min(WINDOW, t)
    return (x, kv.reshape(b, t, 4, g, hd), kw[:, t - keep:].reshape(b, keep, 2, g, hd), _unblock_states(s_bd),
            _unpad_xr(xr[:, -1]))


def _sample_layer(x, pe, lp, cache_t, pt_flat, n_pages, cwin_t, win_off, state_wkv, state_shift):
    db = x.shape[0]
    g, hd = NSA_KV_HEADS, HEAD_DIM
    q, kv, kw, gl, xr = _in_proj(x, lp["norm_mix"], lp["w_in"])
    kcn, _, vcn, _ = _compress_paged(cache_t, pt_flat, db, n_pages, lp["cw"])
    q3 = q.reshape(db, NSA_HEADS, hd)
    z = jnp.zeros((db, NSA_HEADS // 2, hd), F32)
    qm = jnp.concatenate([jnp.concatenate([q3[:, :4], z], -1), jnp.concatenate([z, q3[:, 4:]], -1)], axis=1)
    glm = jnp.pad(gl[:, :GATE_COLS].reshape(db, NSA_HEADS, 3), ((0, 0), (0, 0), (0, LANES - 3)))
    o8, win = _nsa_step(cache_t, pt_flat, qm, glm, kcn, vcn, kv.reshape(db, 4 * g, hd), cwin_t, win_off,
                        kw.reshape(db, 2 * g, hd), lp["step_gains"], n_pages)
    o_nsa = o8.reshape(db, Q_COLS)
    feats = _rwkv_prep_step(xr, _pad_xr(state_shift), lp["rp"])
    o_rwkv, s1 = _rwkv_step(state_wkv, feats, lp["rk"], lp["lnw"], lp["lnb"])
    x = _out_ffn(x, o_nsa, o_rwkv, pe, lp)
    return (x, kv.reshape(db, 1, 4, g, hd), win, s1, _unpad_xr(xr))


def kernel(x_prompt, x_sample, p_prompt, p_sample, cache_kv, cache_win, state_wkv, state_shift, page_table, norm_mix, w_in, q_norm, k_norm, cmp_pe, cmp_w1, cmp_b1, cmp_w2, cmp_b2, shift_mu, w0, w2, a0, a2, g2, k_k, k_a, r_k, ln_w, ln_b, w_out, norm_ffn, w_gate, w_up, w_down, norm_ple, w_ple_gate, w_ple):
    b, t, d = x_prompt.shape
    db = x_sample.shape[0]
    depth, n_phys, page = cache_kv.shape[:3]
    n_pages = page_table.shape[1]
    assert x_sample.shape[1] == 1 and d == D_MODEL
    rows_last = (0, 1, 3, 4, 5, 2)
    cache_t = cache_kv.transpose(rows_last).reshape(depth * n_phys, -1, HEAD_DIM, page)
    wbuf = cache_win.shape[2]
    cwin_t = cache_win.transpose(rows_last).reshape(depth * db, -1, HEAD_DIM, wbuf)
    pt_flat = page_table.reshape(-1).astype(jnp.int32)
    xp = x_prompt.reshape(b * t, d)
    xs = x_sample.reshape(db, d)
    outs_p, outs_s = [], []
    for i in range(depth):
        lp = _layer_params(i, norm_mix, w_in, q_norm, k_norm, cmp_pe, cmp_w1, cmp_b1, cmp_w2, cmp_b2, shift_mu,
                           w0, w2, a0, a2, g2, k_k, k_a, r_k, ln_w, ln_b, w_out, norm_ffn, w_gate, w_up, w_down,
                           norm_ple, w_ple_gate, w_ple)
        xp, *rest_p = _prompt_layer(xp, p_prompt[i].reshape(b * t, -1), lp, b, t)
        xs, *rest_s = _sample_layer(xs, p_sample[i].reshape(db, -1), lp, cache_t, pt_flat + i * n_phys, n_pages,
                                    cwin_t, i * db, state_wkv[i], state_shift[i])
        outs_p.append(rest_p)
        outs_s.append(rest_s)
    stack = lambda outs, k: jnp.stack([o[k] for o in outs])
    g = NSA_KV_HEADS
    win_s = stack(outs_s, 1).reshape(depth, db, 2, g, HEAD_DIM, wbuf).transpose(0, 1, 5, 2, 3, 4)
    return (xp.reshape(b, t, d), xs.reshape(db, 1, d),
            stack(outs_p, 0), stack(outs_s, 0), stack(outs_p, 1), win_s,
            stack(outs_p, 2), stack(outs_s, 2), stack(outs_p, 3), stack(outs_s, 3))
```

```python
import functools
import itertools

import jax
import jax.numpy as jnp
from jax import lax
from jax.experimental import pallas as pl
from jax.experimental.pallas import tpu as pltpu

F32 = jnp.float32
BF16 = jnp.bfloat16
HI = lax.Precision.HIGHEST

D_MODEL = 1024
HEAD_DIM = 64
LANES = 128
NSA_HEADS = 8
NSA_KV_HEADS = 2
RWKV_HEADS = 8
RWKV_WIDTH = RWKV_HEADS * HEAD_DIM
CMP_BLOCK = 32
CMP_STRIDE = 16
CMP_HIDDEN = 256
SEL_BLOCK = 64
SEL_TOPN = 16
WINDOW = 512
Q_BLOCK = 512
COMPRESS_SEQS = 2
NSA_STEP_SEQS = 4
WINDOW_Q = 512
DECAY_RANK = 64
A_RANK = 64
GATE_RANK = 128
Q_COLS = NSA_HEADS * HEAD_DIM
KV_COLS = 4 * NSA_KV_HEADS * HEAD_DIM
WIN_COLS = 2 * NSA_KV_HEADS * HEAD_DIM
GATE_COLS = 3 * NSA_HEADS
RWKV_PROJ = 3 * RWKV_WIDTH + DECAY_RANK + A_RANK + GATE_RANK
XR_PAD = 3 * RWKV_WIDTH + 3 * LANES
IN_PAD = Q_COLS + KV_COLS + WIN_COLS + LANES + XR_PAD
NORM_EPS = 1e-6
GN_EPS = 64e-5
BIG = 1e9
NEG = -1e30
RWKV_CHUNK = 32
RWKV_SEQS_PER_STEP = 4
VMEM_LIMIT =56 * 1024 * 1024


def _dot(a, b, prec=None):
    return lax.dot_general(a, b, (((1,), (0,)), ((), ())), precision=prec, preferred_element_type=F32)


def _dot_nt(a, b, prec=None):
    return lax.dot_general(a, b, (((1,), (1,)), ((), ())), precision=prec, preferred_element_type=F32)


def _dot_tn(a, b, prec=None):
    return lax.dot_general(a, b, (((0,), (0,)), ((), ())), precision=prec, preferred_element_type=F32)


def _iota(shape, dim):
    return lax.broadcasted_iota(jnp.int32, shape, dim)


def _lo_mask(shape):
    return (_iota(shape, len(shape) - 1) % LANES) < HEAD_DIM


def _seg_sum(x):
    lo = _lo_mask(x.shape)
    s_lo = jnp.sum(jnp.where(lo, x, 0.0), axis=-1, keepdims=True)
    s_hi = jnp.sum(jnp.where(lo, 0.0, x), axis=-1, keepdims=True)
    return jnp.where(lo, s_lo, s_hi)


def _seg_rmsnorm(x, gain):
    ms = _seg_sum(x * x) * (1.0 / HEAD_DIM)
    return x * lax.rsqrt(ms + NORM_EPS) * gain


def _rmsnorm_rows(x, gain):
    return x * lax.rsqrt(jnp.mean(x * x, axis=-1, keepdims=True) + NORM_EPS) * gain


def _sigmoid(x):
    return 1.0 / (1.0 + jnp.exp(-x))


def _masked_softmax(s, mask):
    sm = jnp.where(mask, s, NEG)
    m = jnp.max(sm, axis=-1, keepdims=True)
    m = jnp.where(m > 0.5 * NEG, m, 0.0)
    e = jnp.where(mask, jnp.exp(sm - m), 0.0)
    return e * (1.0 / jnp.maximum(jnp.sum(e, axis=-1, keepdims=True), 1e-30))


def _softmax_av(s, bias, v):
    s = s + bias
    e = jnp.exp(s - jnp.max(s, axis=-1, keepdims=True))
    return _dot(e.astype(BF16), v) * (1.0 / jnp.sum(e, axis=-1, keepdims=True))


def _params(sem):
    return pltpu.CompilerParams(dimension_semantics=sem, vmem_limit_bytes=VMEM_LIMIT)


def _const_spec(shape):
    nd = len(shape)
    return pl.BlockSpec(shape, lambda *_: (0,) * nd)


_IN_SPLITS = (("q", 0, Q_COLS), ("kv", Q_COLS, KV_COLS), ("kw", Q_COLS + KV_COLS, WIN_COLS),
              ("gl", Q_COLS + KV_COLS + WIN_COLS, LANES),
              ("xr", Q_COLS + KV_COLS + WIN_COLS + LANES, XR_PAD))


def _in_proj_kernel(x_ref, g_ref, w_ref, *out_refs):
    h = _rmsnorm_rows(x_ref[...], g_ref[...]).astype(BF16)
    for o_ref, (_, lo, n) in zip(out_refs, _IN_SPLITS):
        o_ref[...] = _dot(h, w_ref[:, lo:lo + n])


def _in_proj(x, gain, w_pad):
    m = x.shape[0]
    tm = min(m, 512)
    assert m % tm == 0
    return pl.pallas_call(
        _in_proj_kernel,
        grid=(m // tm,),
        in_specs=[pl.BlockSpec((tm, D_MODEL), lambda i: (i, 0)),
                  _const_spec((1, D_MODEL)),
                  _const_spec((D_MODEL, IN_PAD))],
        out_specs=[pl.BlockSpec((tm, n), lambda i: (i, 0)) for _, _, n in _IN_SPLITS],
        out_shape=[jax.ShapeDtypeStruct((m, n), F32) for _, _, n in _IN_SPLITS],
        compiler_params=_params(("parallel",)),
        name="in_proj",
    )(x, gain, w_pad)


def _pad_xr(v):
    z = jnp.zeros(v.shape[:-1] + (LANES - DECAY_RANK,), v.dtype)
    c = 3 * RWKV_WIDTH
    return jnp.concatenate([v[..., :c + DECAY_RANK], z, v[..., c + DECAY_RANK:c + DECAY_RANK + A_RANK], z,
                            v[..., c + DECAY_RANK + A_RANK:]], axis=-1)


def _unpad_xr(v):
    c = 3 * RWKV_WIDTH
    return jnp.concatenate([v[..., :c + DECAY_RANK], v[..., c + LANES:c + LANES + A_RANK], v[..., c + 2 * LANES:]],
                           axis=-1)


def _pad_w_in(w_in):
    c = Q_COLS + KV_COLS + WIN_COLS
    gl = jnp.pad(w_in[:, c:c + GATE_COLS], ((0, 0), (0, LANES - GATE_COLS)))
    return jnp.concatenate([w_in[:, :c], gl, _pad_xr(w_in[:, c + GATE_COLS:])], axis=-1).astype(BF16)


def _ffn_kernel(x_ref, on_ref, or_ref, pe_ref, wout_ref, nf_ref, wg_ref, wu_ref, wd_ref, np_ref, wpg_ref,
                wple_ref, y_ref, x1_s, hf_s, acc_s):
    j = pl.program_id(1)

    @pl.when(j == 0)
    def _():
        half = D_MODEL // 2
        o = _dot(on_ref[...].astype(BF16), wout_ref[0:half, :]) + _dot(or_ref[...].astype(BF16), wout_ref[half:, :])
        x1 = x_ref[...] + o
        x1_s[...] = x1
        hf_s[...] = _rmsnorm_rows(x1, nf_ref[...]).astype(BF16)
        acc_s[...] = jnp.zeros_like(acc_s)

    hf = hf_s[...]
    g = _dot(hf, wg_ref[...])
    u = _dot(hf, wu_ref[...])
    act = (g * _sigmoid(g)) * u
    acc_s[...] += _dot(act.astype(BF16), wd_ref[...])

    @pl.when(j == pl.num_programs(1) - 1)
    def _():
        x2 = x1_s[...] + acc_s[...]
        hp = _rmsnorm_rows(x2, np_ref[...]).astype(BF16)
        pg = _sigmoid(_dot(hp, wpg_ref[...]))
        y_ref[...] = x2 + pg * _dot(pe_ref[...].astype(BF16), wple_ref[...])


def _out_ffn(x, o_nsa, o_rwkv, pe, lw):
    m = x.shape[0]
    tm = min(m, 512)
    hid = lw["w_gate"].shape[1]
    th = 1408
    assert m % tm == 0 and hid % th == 0
    ple = pe.shape[1]
    row = lambda i, j: (i, 0)
    return pl.pallas_call(
        _ffn_kernel,
        grid=(m // tm, hid // th),
        in_specs=[pl.BlockSpec((tm, D_MODEL), row),
                  pl.BlockSpec((tm, D_MODEL // 2), row),
                  pl.BlockSpec((tm, D_MODEL // 2), row),
                  pl.BlockSpec((tm, ple), row),
                  _const_spec((D_MODEL, D_MODEL)),
                  _const_spec((1, D_MODEL)),
                  pl.BlockSpec((D_MODEL, th), lambda i, j: (0, j)),
                  pl.BlockSpec((D_MODEL, th), lambda i, j: (0, j)),
                  pl.BlockSpec((th, D_MODEL), lambda i, j: (j, 0)),
                  _const_spec((1, D_MODEL)),
                  _const_spec((D_MODEL, D_MODEL)),
                  _const_spec((ple, D_MODEL))],
        out_specs=pl.BlockSpec((tm, D_MODEL), row),
        out_shape=jax.ShapeDtypeStruct((m, D_MODEL), F32),
        scratch_shapes=[pltpu.VMEM((tm, D_MODEL), F32), pltpu.VMEM((tm, D_MODEL), BF16),
                        pltpu.VMEM((tm, D_MODEL), F32)],
        compiler_params=_params(("parallel", "arbitrary")),
        name="out_ffn_ple",
    )(x, o_nsa, o_rwkv, pe, lw["w_out"], lw["norm_ffn"], lw["w_gate"], lw["w_up"], lw["w_down"],
      lw["norm_ple"], lw["w_ple_gate"], lw["w_ple"])


def _gelu_tanh(x):
    return 0.5 * x * (1.0 + jnp.tanh(0.7978845608028654 * (x + 0.044715 * x * x * x)))


def _compress_body(load_rows, nbp, first_step, pe_ref, wbd_ref, b1_ref, w2lo_ref, w2hi_ref, b2_ref, kn_ref,
                   kcn_ref, kcr_ref, vcn_ref, vcr_ref, bias_s):
    hid = CMP_HIDDEN

    @pl.when(first_step)
    def _():
        for ty in range(2):
            acc = jnp.zeros((1, hid), F32)
            for r2 in range(CMP_STRIDE // 2):
                t = _dot(pe_ref[ty, r2:r2 + 1, :].astype(BF16), wbd_ref[ty, r2])
                acc = acc + t[:, 0:hid] + t[:, 3 * hid:4 * hid]
            bias_s[ty:ty + 1, :] = acc + b1_ref[ty:ty + 1, :]

    acc = [jnp.zeros((nbp, 4 * hid), F32) for _ in range(2)]
    for r2 in range(CMP_STRIDE // 2):
        for ty in range(2):
            lhs = jnp.concatenate([load_rows(2 * r2, ty), load_rows(2 * r2 + 1, ty)], axis=-1).astype(BF16)
            acc[ty] = acc[ty] + _dot(lhs, wbd_ref[ty, r2])
    outs = ((kcn_ref, kcr_ref), (vcn_ref, vcr_ref))
    for ty in range(2):
        hs = []
        for g in range(2):
            a = acc[ty][:, 2 * hid * g:2 * hid * g + hid]
            b = acc[ty][:, 2 * hid * g + hid:2 * hid * (g + 1)]
            b_next = pltpu.roll(b, nbp - 1, 0)
            hs.append(_gelu_tanh(a + b_next + bias_s[ty:ty + 1, :]).astype(BF16))
        o_n = _dot(hs[0], w2lo_ref[ty]) + _dot(hs[1], w2hi_ref[ty]) + b2_ref[ty:ty + 1, :]
        o_r = _dot(hs[0], w2hi_ref[ty]) + _dot(hs[1], w2lo_ref[ty]) + b2_ref[ty:ty + 1, :]
        if ty == 0:
            o_n = _seg_rmsnorm(o_n, kn_ref[...])
            o_r = _seg_rmsnorm(o_r, kn_ref[...])
        for o_ref, o in zip(outs[ty], (o_n, o_r)):
            o_ref[...] = o.astype(o_ref.dtype).reshape(o_ref.shape)


def _compress_prompt_kernel(k_ref, v_ref, *refs, nbp):
    load = lambda r, ty: (k_ref, v_ref)[ty][0, pl.ds(r, nbp, stride=CMP_STRIDE), :]
    _compress_body(load, nbp, pl.program_id(0) == 0, *refs)


def _compress_paged_kernel(pt_ref, *refs, n_pages, page):
    del pt_ref
    page_refs, rest, rows_s = refs[:n_pages], refs[n_pages:-2], refs[-2:]
    for j in range(n_pages):
        for ty in range(2):
            tiles = page_refs[j][0, 2 * ty:2 * ty + 2].reshape(2 * HEAD_DIM, page)
            rows_s[ty][j * page:(j + 1) * page, :] = tiles.T
    nbp = n_pages * page // CMP_STRIDE
    load = lambda r, ty: rows_s[ty][pl.ds(r, nbp, stride=CMP_STRIDE), :]
    _compress_body(load, nbp, pl.program_id(0) == 0, *rest)


def _compress_weight_specs():
    hid = CMP_HIDDEN
    return [_const_spec((2, CMP_STRIDE // 2, 2 * LANES)), _const_spec((2, CMP_STRIDE // 2, 2 * LANES, 4 * hid)),
            _const_spec((2, hid)), _const_spec((2, hid, LANES)), _const_spec((2, hid, LANES)),
            _const_spec((2, LANES)), _const_spec((1, LANES))]


def _compress_outs(b, nbp, per_step=1):
    spec = pl.BlockSpec((per_step, nbp, LANES), lambda i, *_: (i, 0, 0))
    shapes = [jax.ShapeDtypeStruct((b, nbp, LANES), dt) for dt in (F32, F32, BF16, BF16)]
    return [spec] * 4, shapes


def _compress_prompt(kv, cw):
    b, t, _ = kv.shape
    nbp = t // CMP_STRIDE
    out_specs, out_shape = _compress_outs(b, nbp)
    return pl.pallas_call(
        functools.partial(_compress_prompt_kernel, nbp=nbp),
        grid=(b,),
        in_specs=[pl.BlockSpec((1, t, LANES), lambda i: (i, 0, 0)),
                  pl.BlockSpec((1, t, LANES), lambda i: (i, 0, 1))] + _compress_weight_specs(),
        out_specs=out_specs, out_shape=out_shape,
        scratch_shapes=[pltpu.VMEM((2, CMP_HIDDEN), F32)],
        compiler_params=_params(("arbitrary",)),
        name="compress_prompt",
    )(kv, kv, *cw)


def _compress_paged(cache, pt_flat, n_seq, n_pages, cw):
    page = cache.shape[3]
    assert page == LANES
    nbp = n_pages * page // CMP_STRIDE
    sq = COMPRESS_SEQS if n_seq % COMPRESS_SEQS == 0 else 1
    out_specs, out_shape = _compress_outs(n_seq, nbp, sq)
    page_specs = [pl.BlockSpec((1, 4, HEAD_DIM, page), functools.partial(
        lambda i, pt, j: (pt[i * sq * n_pages + j], 0, 0, 0), j=j)) for j in range(sq * n_pages)]
    return pl.pallas_call(
        functools.partial(_compress_paged_kernel, n_pages=sq * n_pages, page=page),
        grid_spec=pltpu.PrefetchScalarGridSpec(
            num_scalar_prefetch=1, grid=(n_seq // sq,),
            in_specs=page_specs + _compress_weight_specs(),
            out_specs=out_specs,
            scratch_shapes=[pltpu.VMEM((2, CMP_HIDDEN), F32), pltpu.VMEM((sq * n_pages * page, LANES), F32),
                            pltpu.VMEM((sq * n_pages * page, LANES), F32)]),
        out_shape=out_shape,
        compiler_params=_params(("arbitrary",)),
        name="compress_paged",
    )(pt_flat, *([cache] * (sq * n_pages)), *cw)


def _compress_weights(cmp_pe, cmp_w1, cmp_b1, cmp_w2, cmp_b2, k_norm0):
    hid = CMP_HIDDEN
    w1 = cmp_w1.reshape(2, CMP_BLOCK, HEAD_DIM, hid)
    wcat = jnp.concatenate([w1[:, :CMP_STRIDE], w1[:, CMP_STRIDE:]], axis=-1)
    z = jnp.zeros_like(wcat)
    wbd = jnp.concatenate([jnp.concatenate([wcat, z], -1), jnp.concatenate([z, wcat], -1)], axis=2).astype(BF16)
    wbd = wbd.reshape(2, CMP_STRIDE // 2, 2 * LANES, 4 * hid)
    pe_pair = jnp.concatenate([cmp_pe[:, :CMP_STRIDE], cmp_pe[:, CMP_STRIDE:]], axis=-1)
    pe_pair = pe_pair.reshape(2, CMP_STRIDE // 2, 2 * LANES)
    z2 = jnp.zeros_like(cmp_w2)
    w2lo = jnp.concatenate([cmp_w2, z2], -1).astype(BF16)
    w2hi = jnp.concatenate([z2, cmp_w2], -1).astype(BF16)
    b2 = jnp.concatenate([cmp_b2, cmp_b2], -1)
    kn = jnp.concatenate([k_norm0, k_norm0])[None]
    return (pe_pair, wbd, cmp_b1, w2lo, w2hi, b2, kn)


def _kv_prep_kernel(kv_ref, kw_ref, kns_ref, knw_ref, ksn_ref, ksr_ref, vsn_ref, vsr_ref, kwn_ref, kwr_ref,
                    vwn_ref, vwr_ref):
    def put(x, n_ref, r_ref):
        n_ref[0] = x.astype(BF16)
        r_ref[0] = pltpu.roll(x, HEAD_DIM, 1).astype(BF16)

    put(_seg_rmsnorm(kv_ref[0, :, 2 * LANES:3 * LANES], kns_ref[...]), ksn_ref, ksr_ref)
    put(kv_ref[0, :, 3 * LANES:4 * LANES], vsn_ref, vsr_ref)
    put(_seg_rmsnorm(kw_ref[0, :, 0:LANES], knw_ref[...]), kwn_ref, kwr_ref)
    put(kw_ref[0, :, LANES:2 * LANES], vwn_ref, vwr_ref)


def _kv_prep(kv, kw, kn_sel, kn_win):
    b, t, _ = kv.shape
    tm = min(t, 512)
    spec = lambda n: pl.BlockSpec((1, tm, n), lambda i, j: (i, j, 0))
    return pl.pallas_call(
        _kv_prep_kernel,
        grid=(b, t // tm),
        in_specs=[spec(4 * LANES), spec(2 * LANES), _const_spec((1, LANES)), _const_spec((1, LANES))],
        out_specs=[spec(LANES)] * 8,
        out_shape=[jax.ShapeDtypeStruct((b, t, LANES), BF16)] * 8,
        compiler_params=_params(("parallel", "parallel")),
        name="kv_prep",
    )(kv, kw, kn_sel, kn_win)


def _overlap_t(ns, nbp):
    j = _iota((ns, nbp), 0) * SEL_BLOCK
    n = _iota((ns, nbp), 1) * CMP_STRIDE
    return jnp.where((n < j + SEL_BLOCK) & (n + CMP_BLOCK > j), 1.0, 0.0).astype(F32)


def _nsa_prompt_kernel(q_ref, gl_ref, qn_ref, kcn_ref, kcr_ref, vcn_ref, vcr_ref, ksn_ref, ksr_ref, vsn_ref,
                       vsr_ref, kwn_ref, kwr_ref, vwn_ref, vwr_ref, o_ref, *, seq, tk, wq, wk):
    i = pl.program_id(1)
    qb = Q_BLOCK
    nbp = kcn_ref.shape[1]
    nb = nbp - 1
    ns = seq // SEL_BLOCK
    n_sel = min(SEL_TOPN, ns)
    lo = _lo_mask((qb, LANES))
    pos = i * qb + _iota((qb, 1), 0)
    sig = _sigmoid(gl_ref[0])

    qp = [_seg_rmsnorm(q_ref[0, :, LANES * p:LANES * (p + 1)], qn_ref[...]) * (HEAD_DIM ** -0.5) for p in range(4)]
    q_lo = [jnp.where(lo, x, 0.0) for x in qp]
    q_hi = [jnp.where(lo, 0.0, x) for x in qp]

    n_c = _iota((qb, nbp), 1)
    mask_c = (n_c * CMP_STRIDE + (CMP_BLOCK - 1) <= pos) & (n_c < nb)
    ov_t = _overlap_t(ns, nbp)
    j_t = _iota((ns, qb), 0)
    pos_t = i * qb + _iota((ns, qb), 1)
    cur_t = pos_t >> 6
    forced_t = (j_t == 0) | (j_t == cur_t) | (j_t == cur_t - 1)
    causal_t = j_t * SEL_BLOCK <= pos_t
    eye = jnp.where(_iota((qb, qb), 0) == _iota((qb, qb), 1), 1.0, 0.0).astype(BF16)

    n_tiles = (i * qb + qb - 1) // tk + 1

    order = lambda g, n, r: (n, r) if g == 0 else (r, n)
    heads = []
    for p in range(4):
        heads += [(p, p // 2, q_lo[p], 0), (p, p // 2, q_hi[p], 1)]

    split = lambda x: (x.astype(BF16), (x - x.astype(BF16).astype(F32)).astype(BF16))
    kc_parts = [split(kcn_ref[0]), split(kcr_ref[0])]

    def score3(qt, kc):
        q_h, q_l = split(qt)
        return _dot_nt(q_h, kc[0]) + (_dot_nt(q_h, kc[1]) + _dot_nt(q_l, kc[0]))

    p_cmp = [_masked_softmax(score3(qt, order(g, *kc_parts)[w]), mask_c) for _, g, qt, w in heads]
    o_cmp = [_dot(pc.astype(BF16), order(g, vcn_ref, vcr_ref)[w][0]) for pc, (_, g, _, w) in zip(p_cmp, heads)]
    unsel = []
    for g in range(2):
        psum = p_cmp[4 * g] + p_cmp[4 * g + 1] + p_cmp[4 * g + 2] + p_cmp[4 * g + 3]
        imp_t = _dot_nt(ov_t, psum, HI)
        score = jnp.where(forced_t, BIG, jnp.where(causal_t, imp_t, -BIG))
        rank = jnp.zeros((ns, qb), jnp.int32)
        for jp in range(ns):
            row = score[jp:jp + 1, :]
            rank = rank + jnp.where((row > score) | ((row == score) & (j_t > jp)), 1, 0)
        unsel.append(_dot_nt(eye, jnp.where(rank < n_sel, 0.0, NEG).astype(BF16)).astype(BF16))

    q16 = [qt.astype(BF16) for _, _, qt, _ in heads]

    def tile(kt, carry, diagonal):
        k0 = pl.multiple_of(kt * tk, tk)
        blk = (k0 + _iota((ns, tk), 1)) >> 6
        expand = jnp.where(_iota((ns, tk), 0) == blk, 1.0, 0.0).astype(BF16)
        bias = [_dot(u, expand) for u in unsel]
        if diagonal:
            causal = k0 + _iota((qb, tk), 1) <= pos
            bias = [jnp.where(causal, x, NEG) for x in bias]
        out = []
        for (m, l, acc), q_, (_, g, _, w) in zip(carry, q16, heads):
            k_ref = order(g, ksn_ref, ksr_ref)[w]
            v_ref = order(g, vsn_ref, vsr_ref)[w]
            s = _dot_nt(q_, k_ref[0, pl.ds(k0, tk), :]) + bias[g]
            m_new = jnp.maximum(m, jnp.max(s, axis=-1, keepdims=True))
            alpha = jnp.exp(m - m_new)
            e = jnp.exp(s - m_new)
            l_new = alpha * l + jnp.sum(e, axis=-1, keepdims=True)
            acc_new = alpha * acc + _dot(e.astype(BF16), v_ref[0, pl.ds(k0, tk), :])
            out.append((m_new, l_new, acc_new))
        return tuple(out)

    init = (jnp.full((qb, 1), NEG, F32), jnp.zeros((qb, 1), F32), jnp.zeros((qb, LANES), F32))
    carry = lax.fori_loop(0, n_tiles - 1, functools.partial(tile, diagonal=False), (init,) * len(heads))
    o_sel = [acc / l for _, l, acc in tile(n_tiles - 1, carry, True)]

    o_win = [[] for _ in heads]
    for sub in range(qb // wq):
        w_start = pl.multiple_of(jnp.maximum(i * qb + sub * wq - (wk - wq), 0), wq)
        w_dist = pos[sub * wq:(sub + 1) * wq] - (w_start + _iota((wq, wk), 1))
        bias_w = jnp.where((w_dist >= 0) & (w_dist <= WINDOW), 0.0, NEG)
        for n, (q_, (_, g, _, w)) in enumerate(zip(q16, heads)):
            o_win[n].append(_softmax_av(
                _dot_nt(q_[sub * wq:(sub + 1) * wq], order(g, kwn_ref, kwr_ref)[w][0, pl.ds(w_start, wk), :]), bias_w,
                order(g, vwn_ref, vwr_ref)[w][0, pl.ds(w_start, wk), :]))
    o_win = [jnp.concatenate(parts, axis=0) for parts in o_win]

    for p in range(4):
        gate = lambda c: jnp.where(lo, sig[:, 6 * p + c:6 * p + c + 1], sig[:, 6 * p + 3 + c:6 * p + 4 + c])
        both = lambda o: jnp.where(lo, o[2 * p], o[2 * p + 1])
        o_ref[0, :, LANES * p:LANES * (p + 1)] = gate(0) * both(o_cmp) + gate(1) * both(o_sel) + gate(2) * both(o_win)


def _nsa_prompt(q, gl, qn, kc, ksv, seq):
    b = q.shape[0]
    nbp = kc[0].shape[1]
    tk = min(seq, 512)
    wq = min(Q_BLOCK, WINDOW_Q)
    wk = min(seq, WINDOW + wq)
    assert seq % Q_BLOCK == 0 and seq % tk == 0
    blk = lambda n: pl.BlockSpec((1, Q_BLOCK, n), lambda bi, i: (bi, i, 0))
    full = lambda rows: pl.BlockSpec((1, rows, LANES), lambda bi, i: (bi, 0, 0))
    return pl.pallas_call(
        functools.partial(_nsa_prompt_kernel, seq=seq, tk=tk, wq=wq, wk=wk),
        grid=(b, seq // Q_BLOCK),
        in_specs=[blk(4 * LANES), blk(LANES), _const_spec((1, LANES))] + [full(nbp)] * 4 + [full(seq)] * 8,
        out_specs=blk(4 * LANES),
        out_shape=jax.ShapeDtypeStruct((b, seq, 4 * LANES), F32),
        compiler_params=_params(("parallel", "arbitrary")),
        name="nsa_prompt",
    )(q, gl, qn, *kc, *ksv)


def _nsa_step_kernel(pt_ref, *refs, n_pages, page, past):
    del pt_ref
    n_page_refs = len(refs) - 14
    page_refs = refs[:n_page_refs]
    (qm_ref, gl_ref, kcn_ref, vcn_ref, kvn_ref, cw_ref, kwn_ref, qn_ref, kns_ref, knw_ref, kns_row_ref, knw_row_ref,
     o_ref, win_ref) = refs[n_page_refs:]
    nh = NSA_HEADS
    hd = HEAD_DIM
    nbp = kcn_ref.shape[1]
    nb = nbp - 1
    ns = past // SEL_BLOCK + 1
    n_sel = min(SEL_TOPN, ns)
    wbuf = cw_ref.shape[3]
    rep = NSA_HEADS // NSA_KV_HEADS
    first_group = _iota((nh, 1), 0) < rep

    def unpair(x):
        return jnp.where(first_group, x, pltpu.roll(x, hd, 1))[:, 0:hd]

    def by_group(f):
        return jnp.where(first_group, f(0), f(1))

    def norm_t(t, gain_ref):
        return (t * lax.rsqrt(jnp.mean(t * t, axis=0, keepdims=True) + NORM_EPS) * gain_ref[...]).astype(BF16)

    def one(s):
        qn = _seg_rmsnorm(qm_ref[s], qn_ref[...]) * (HEAD_DIM ** -0.5)
        q8 = unpair(qn)
        q8b = q8.astype(BF16)
        sg = _sigmoid(gl_ref[s])
        yield

        n_c = _iota((nh, nbp), 1)
        mask_c = (n_c * CMP_STRIDE + (CMP_BLOCK - 1) <= past) & (n_c < nb)
        p_c = _masked_softmax(_dot_nt(qn, kcn_ref[s], HI), mask_c)
        yield
        o_cmp = unpair(_dot(p_c.astype(BF16), vcn_ref[s]))
        same_group = jnp.where(_iota((nh, nh), 0) // rep == _iota((nh, nh), 1) // rep, 1.0, 0.0).astype(F32)
        psum = _dot(same_group, p_c, HI)
        yield
        imp = _dot_nt(psum, _overlap_t(LANES, nbp), HI)
        yield
        j = _iota((nh, LANES), 1)
        cur = past // SEL_BLOCK
        forced = (j == 0) | (j == cur) | (j == cur - 1)
        score = jnp.where(forced, BIG, jnp.where(j * SEL_BLOCK <= past, imp, -BIG))
        score = jnp.where(j < ns, score, -4.0 * BIG)
        rank = jnp.zeros((nh, LANES), jnp.int32)
        for jp in range(ns):
            col = score[:, jp:jp + 1]
            rank = rank + jnp.where((col > score) | ((col == score) & (j > jp)), 1, 0)
        sel = jnp.where((rank < n_sel) & (j < ns), 1.0, 0.0).astype(BF16)
        yield

        kv_new = kvn_ref[s]
        k_new = by_group(lambda g: _rmsnorm_rows(kv_new[4 + g:5 + g, :], kns_row_ref[...]))
        s_new = jnp.sum(q8 * k_new, axis=-1, keepdims=True)
        s_parts, m_parts = [], []
        for pj in range(n_pages):
            blk = (pj * page + _iota((LANES, page), 1)) >> 6
            expand = jnp.where(_iota((LANES, page), 0) == blk, 1.0, 0.0).astype(BF16)
            m_parts.append(_dot(sel, expand) > 0.5)
            s_parts.append(by_group(lambda g: _dot(q8b, norm_t(page_refs[s * n_pages + pj][0, g], kns_ref))))
        s_all = jnp.concatenate(s_parts, axis=-1)
        valid = jnp.concatenate(m_parts, axis=-1)
        s_all = jnp.where(valid, s_all, NEG)
        yield
        m = jnp.maximum(jnp.max(s_all, axis=-1, keepdims=True), s_new)
        e_all = jnp.where(valid, jnp.exp(s_all - m), 0.0)
        e_new = jnp.exp(s_new - m)
        yield
        acc = e_new * by_group(lambda g: kv_new[6 + g:7 + g, :])
        for pj in range(n_pages):
            e_pj = e_all[:, pj * page:(pj + 1) * page].astype(BF16)
            acc = acc + by_group(lambda g: _dot_nt(e_pj, page_refs[s * n_pages + pj][0, 2 + g].astype(BF16)))
        o_sel = acc / (jnp.sum(e_all, axis=-1, keepdims=True) + e_new)
        yield

        kw_new = kwn_ref[s]
        s_w = by_group(lambda g: _dot(q8b, norm_t(cw_ref[s, g], knw_ref)))
        s_wn = jnp.sum(q8 * by_group(lambda g: _rmsnorm_rows(kw_new[g:g + 1, :], knw_row_ref[...])), axis=-1,
                       keepdims=True)
        m_w = jnp.maximum(jnp.max(s_w, axis=-1, keepdims=True), s_wn)
        e_w = jnp.exp(s_w - m_w)
        e_wn = jnp.exp(s_wn - m_w)
        e_wb = e_w.astype(BF16)
        yield
        o_win = (by_group(lambda g: _dot_nt(e_wb, cw_ref[s, 2 + g].astype(BF16))) + e_wn * by_group(
            lambda g: kw_new[2 + g:3 + g, :])) / (jnp.sum(e_w, axis=-1, keepdims=True) + e_wn)

        o_ref[s] = sg[:, 0:1] * o_cmp + sg[:, 1:2] * o_sel + sg[:, 2:3] * o_win

        eye = _iota((hd, hd), 0) == _iota((hd, hd), 1)
        last = _iota((hd, wbuf), 1) == wbuf - 1
        for c in range(4):
            col = jnp.sum(jnp.where(eye, kw_new[c:c + 1, :], 0.0), axis=1, keepdims=True)
            win_ref[s, c] = jnp.where(last, col, pltpu.roll(cw_ref[s, c], wbuf - 1, 1))

    for _ in itertools.zip_longest(*[one(s) for s in range(qm_ref.shape[0])]):
        pass


def _nsa_step(cache, pt_flat, qm, glm, kcn, vcn, kv_new, cache_win, win_off, kw_new, gains, n_pages):
    db = qm.shape[0]
    page = cache.shape[3]
    past = n_pages * page
    nbp = kcn.shape[1]
    wbuf = cache_win.shape[3]
    hd = HEAD_DIM
    assert wbuf == min(WINDOW, past) and page == LANES and past // SEL_BLOCK + 1 <= LANES
    sq = NSA_STEP_SEQS if db % NSA_STEP_SEQS == 0 and win_off % NSA_STEP_SEQS == 0 else 1
    per = lambda shape: pl.BlockSpec((sq,) + shape, lambda i, pt: (i,) + (0,) * len(shape))
    page_specs = [pl.BlockSpec((1, 4, hd, page), functools.partial(
        lambda i, pt, s, j: (pt[(i * sq + s) * n_pages + j], 1, 0, 0), s=s, j=j))
        for s in range(sq) for j in range(n_pages)]
    return pl.pallas_call(
        functools.partial(_nsa_step_kernel, n_pages=n_pages, page=page, past=past),
        grid_spec=pltpu.PrefetchScalarGridSpec(
            num_scalar_prefetch=1, grid=(db // sq,),
            in_specs=page_specs + [per((NSA_HEADS, LANES)), per((NSA_HEADS, LANES)), per((nbp, LANES)),
                                   per((nbp, LANES)), per((8, hd)),
                                   pl.BlockSpec((sq, 4, hd, wbuf), lambda i, pt: (i + win_off // sq, 0, 0, 0)),
                                   per((4, hd))]
            + [_const_spec((1, LANES)), _const_spec((hd, 1)), _const_spec((hd, 1)), _const_spec((1, hd)),
               _const_spec((1, hd))],
            out_specs=[per((NSA_HEADS, hd)), per((4, hd, wbuf))]),
        out_shape=[jax.ShapeDtypeStruct((db, NSA_HEADS, hd), F32),
                   jax.ShapeDtypeStruct((db, 4, hd, wbuf), F32)],
        compiler_params=_params(("parallel",)),
        name="nsa_step",
    )(pt_flat, *([cache] * (sq * n_pages)), qm, glm, kcn, vcn, kv_new, cache_win, kw_new, *gains)


def _softplus(z):
    return jnp.maximum(z, 0.0) + jnp.log(1.0 + jnp.exp(-jnp.abs(z)))


def _rwkv_features(x, prev, mu_ref, w0_ref, a0_ref, w2_ref, a2_ref, g2_ref, kk_ref, ka_ref):
    rw = RWKV_WIDTH
    xs = x + (prev - x) * mu_ref[...]
    r, k, v = xs[:, 0:rw], xs[:, rw:2 * rw], xs[:, 2 * rw:3 * rw]
    wd = xs[:, 3 * rw:3 * rw + LANES]
    ad = xs[:, 3 * rw + LANES:3 * rw + 2 * LANES]
    gd = xs[:, 3 * rw + 2 * LANES:3 * rw + 3 * LANES]
    w_log = -_softplus(-(w0_ref[...] + _dot(jnp.tanh(wd).astype(BF16), w2_ref[...]))) - 0.5
    a_sig = _sigmoid(a0_ref[...] + _dot(ad.astype(BF16), a2_ref[...]))
    kk = k * kk_ref[...]
    kk2 = kk * kk
    norm2 = jnp.concatenate([_seg_sum(kk2[:, LANES * p:LANES * (p + 1)]) for p in range(rw // LANES)], axis=-1)
    kk = kk / jnp.maximum(jnp.sqrt(norm2), 1e-12)
    return (r, -jnp.exp(w_log), k * (1.0 + (a_sig - 1.0) * ka_ref[...]), v, -kk, kk * a_sig,
            _dot(_sigmoid(gd).astype(BF16), g2_ref[...]))


def _rwkv_prep_step_kernel(xr_ref, shift_ref, *refs):
    for o_ref, val in zip(refs[8:15], _rwkv_features(xr_ref[...], shift_ref[...], *refs[:8])):
        o_ref[...] = val


def _rwkv_param_specs():
    one = _const_spec((1, RWKV_WIDTH))
    low = _const_spec((LANES, RWKV_WIDTH))
    return [_const_spec((1, XR_PAD)), one, one, low, low, low, one, one]


def _rwkv_prep_step(xr, shift0, rp):
    m = xr.shape[0]
    return pl.pallas_call(
        _rwkv_prep_step_kernel,
        grid=(1,),
        in_specs=[_const_spec((m, XR_PAD)), _const_spec((m, XR_PAD))] + _rwkv_param_specs(),
        out_specs=[_const_spec((m, RWKV_WIDTH))] * 7,
        out_shape=[jax.ShapeDtypeStruct((m, RWKV_WIDTH), F32)] * 7,
        compiler_params=_params(("arbitrary",)),
        name="rwkv_prep_step",
    )(xr, shift0, *rp)


def _rwkv_params(shift_mu, w0, w2, a0, a2, g2, k_k, k_a):
    pad = lambda w: jnp.pad(w, ((0, LANES - w.shape[0]), (0, 0))).astype(BF16)
    return (_pad_xr(shift_mu)[None], w0[None], a0[None], pad(w2), pad(a2), g2.astype(BF16), k_k[None], k_a[None])


def _rwkv_finish(y, r, k, v, gate, rk, lnw, lnb):
    mu = _seg_sum(y) * (1.0 / HEAD_DIM)
    d = y - mu
    var = _seg_sum(d * d) * (1.0 / HEAD_DIM)
    yn = d * lax.rsqrt(var + GN_EPS) * lnw + lnb
    return (yn + _seg_sum(r * k * rk) * v) * gate


def _rwkv_scan_kernel(xr_ref, shift_ref, *refs):
    feat_params, (rk_ref, lnw_ref, lnb_ref, o_ref, s_out_ref, s_s, last_s) = refs[:8], refs[8:]
    c = pl.program_id(1)
    nseq, ch = xr_ref.shape[:2]
    gw = 4 * HEAD_DIM
    rows = 4 * ch

    @pl.when(c == 0)
    def _():
        s_s[...] = jnp.zeros_like(s_s)
        last_s[...] = shift_ref[...]

    x = xr_ref[...].reshape(nseq * ch, XR_PAD)
    prev = pltpu.roll(x, 1, 0)
    row = _iota(x.shape, 0)
    for i in range(nseq):
        prev = jnp.where(row == i * ch, last_s[i], prev)
        last_s[i] = x[(i + 1) * ch - 1:(i + 1) * ch, :]
    feats = _rwkv_features(x, prev, *feat_params)

    tril = jnp.where(_iota((ch, ch), 0) >= _iota((ch, ch), 1), 1.0, 0.0).astype(F32)
    t_row = _iota((rows, rows), 0) % ch
    t_col = _iota((rows, rows), 1) % ch
    strict = t_row > t_col
    incl = t_row >= t_col
    eye = jnp.where(_iota((rows, rows), 0) == _iota((rows, rows), 1), 1.0, 0.0).astype(F32)
    head_of_lane = _iota((ch, gw), 1) // HEAD_DIM

    def wide(x):
        return jnp.concatenate([jnp.where(head_of_lane == h, x, 0.0) for h in range(4)], axis=0).astype(BF16)

    units = [(i, q) for i in range(nseq) for q in range(RWKV_WIDTH // gw)]
    each = lambda f, *cols: [f(*xs) for xs in zip(*cols)]
    part = lambda f: [f[i * ch:(i + 1) * ch, gw * q:gw * (q + 1)] for i, q in units]
    r, lw, k, v, a, b, gate = (part(f) for f in feats)
    s0 = [s_s[i, q] for i, q in units]
    cl = each(lambda x: _dot(tril, x, HI), lw)
    g_in = each(jnp.exp, cl)
    g_inv = each(lambda x: jnp.exp(-x), cl)
    w_ar = each(lambda a_, r_, cl_, lw_, gi: jnp.concatenate([wide(a_ * jnp.exp(cl_ - lw_)), wide(r_ * gi)], axis=0),
                a, r, cl, lw, g_in)
    w_bk = each(lambda b_, k_, gv: jnp.concatenate([wide(b_ * gv), wide(k_ * gv)], axis=0), b, k, g_inv)
    w_v = each(wide, v)
    p_all = each(_dot_nt, w_ar, w_bk)
    xy0 = each(lambda x, s: _dot_nt(x, s.astype(BF16)), w_ar, s0)
    n = each(lambda p: jnp.where(strict, p[:rows, :rows], 0.0), p_all)
    l_ak = each(lambda p: jnp.where(strict, p[:rows, rows:], 0.0).astype(BF16), p_all)
    m_rb = each(lambda p: jnp.where(incl, p[rows:, :rows], 0.0).astype(BF16), p_all)
    m_rk = each(lambda p: jnp.where(incl, p[rows:, rows:], 0.0).astype(BF16), p_all)
    t_inv = each(lambda x: eye + x, n)
    pw = each(lambda x: x.astype(BF16), n)
    for _ in range(max(ch.bit_length() - 2, 0)):
        pw = each(lambda x: _dot(x, x).astype(BF16), pw)
        t_inv = each(lambda t, x: t + _dot(t.astype(BF16), x), t_inv, pw)
    x_w = each(lambda xy, l, vv: (xy[:rows] + _dot(l, vv)).astype(BF16), xy0, l_ak, w_v)
    u = each(lambda t, x: _dot(t.astype(BF16), x).astype(BF16), t_inv, x_w)
    y_w = each(lambda xy, mb, uu, mk, vv: xy[rows:] + _dot(mb, uu) + _dot(mk, vv), xy0, m_rb, u, m_rk, w_v)
    upd = each(lambda uu, vv, bk: _dot_tn(jnp.concatenate([uu, vv], axis=0), bk), u, w_v, w_bk)
    for j, (i, q) in enumerate(units):
        s_s[i, q] = (s0[j] + upd[j]) * g_in[j][ch - 1:ch, :]
        y = y_w[j][0:ch] + y_w[j][ch:2 * ch] + y_w[j][2 * ch:3 * ch] + y_w[j][3 * ch:4 * ch]
        for p in range(gw // LANES):
            ps = slice(LANES * p, LANES * (p + 1))
            po = slice(gw * q + LANES * p, gw * q + LANES * (p + 1))
            o_ref[i, :, po] = _rwkv_finish(y[:, ps], r[j][:, ps], k[j][:, ps], v[j][:, ps], gate[j][:, ps],
                                           rk_ref[:, po], lnw_ref[:, po], lnb_ref[:, po])

    @pl.when(c == pl.num_programs(1) - 1)
    def _():
        s_out_ref[...] = s_s[...]


def _rwkv_scan(xr, shift0, rp, rk, lnw, lnb):
    b, t, _ = xr.shape
    ch = min(t, RWKV_CHUNK)
    assert t % ch == 0 and ch & (ch - 1) == 0
    bb = RWKV_SEQS_PER_STEP if b % RWKV_SEQS_PER_STEP == 0 else 1
    blk = lambda n: pl.BlockSpec((bb, ch, n), lambda i, c: (i, c, 0))
    one = _const_spec((1, RWKV_WIDTH))
    gw = 4 * HEAD_DIM
    n_groups = RWKV_WIDTH // gw
    return pl.pallas_call(
        _rwkv_scan_kernel,
        grid=(b // bb, t // ch),
        in_specs=[blk(XR_PAD), pl.BlockSpec((bb, 1, XR_PAD), lambda i, c: (i, 0, 0))] + _rwkv_param_specs()
        + [one] * 3,
        out_specs=[blk(RWKV_WIDTH), pl.BlockSpec((bb, n_groups, gw, gw), lambda i, c: (i, 0, 0, 0))],
        out_shape=[jax.ShapeDtypeStruct((b, t, RWKV_WIDTH), F32),
                   jax.ShapeDtypeStruct((b, n_groups, gw, gw), F32)],
        scratch_shapes=[pltpu.VMEM((bb, n_groups, gw, gw), F32), pltpu.VMEM((bb, 1, XR_PAD), F32)],
        compiler_params=_params(("parallel", "arbitrary")),
        name="rwkv_scan",
    )(xr, shift0, *rp, rk, lnw, lnb)


def _unblock_states(s_bd):
    hd = HEAD_DIM
    blocks = [s_bd[:, :, hd * h:hd * (h + 1), hd * h:hd * (h + 1)] for h in range(s_bd.shape[2] // hd)]
    return jnp.stack(blocks, axis=2).reshape(s_bd.shape[0], -1, hd, hd)


def _rwkv_step_kernel(s_ref, r_ref, lw_ref, k_ref, v_ref, a_ref, b_ref, g_ref, rk_ref, lnw_ref, lnb_ref,
                      o_ref, s_out_ref):
    hd = HEAD_DIM
    eye = _iota((hd, hd), 0) == _iota((hd, hd), 1)
    for h in range(RWKV_HEADS):
        s0 = s_ref[0, h]
        r, lw, k, v, a, b = (ref[0, h] for ref in (r_ref, lw_ref, k_ref, v_ref, a_ref, b_ref))
        v_col = jnp.sum(jnp.where(eye, v, 0.0), axis=-1, keepdims=True)
        sa = jnp.sum(s0 * a, axis=-1, keepdims=True)
        s1 = s0 * jnp.exp(lw) + sa * b + v_col * k
        s_out_ref[0, h] = s1
        y_col = jnp.sum(s1 * r, axis=-1, keepdims=True)
        y = jnp.sum(jnp.where(eye, y_col, 0.0), axis=0, keepdims=True)
        mu = jnp.mean(y, axis=-1, keepdims=True)
        d = y - mu
        var = jnp.mean(d * d, axis=-1, keepdims=True)
        yn = d * lax.rsqrt(var + GN_EPS) * lnw_ref[h] + lnb_ref[h]
        bonus = jnp.sum(r * k * rk_ref[h], axis=-1, keepdims=True) * v
        o_ref[0, h] = (yn + bonus) * g_ref[0, h]


def _rwkv_step(state, feats, rk, lnw, lnb):
    m = state.shape[0]
    hd, nh = HEAD_DIM, RWKV_HEADS
    heads = lambda x: x.reshape(x.shape[:-1] + (nh, 1, hd))
    vec = pl.BlockSpec((1, nh, 1, hd), lambda i: (i, 0, 0, 0))
    st = pl.BlockSpec((1, nh, hd, hd), lambda i: (i, 0, 0, 0))
    par = _const_spec((nh, 1, hd))
    o, s1 = pl.pallas_call(
        _rwkv_step_kernel,
        grid=(m,),
        in_specs=[st] + [vec] * 7 + [par] * 3,
        out_specs=[vec, st],
        out_shape=[jax.ShapeDtypeStruct((m, nh, 1, hd), F32), jax.ShapeDtypeStruct((m, nh, hd, hd), F32)],
        compiler_params=_params(("parallel",)),
        name="rwkv_step",
    )(state, *[heads(f) for f in feats], heads(rk)[0], heads(lnw)[0], heads(lnb)[0])
    return o.reshape(m, nh * hd), s1


def _layer_params(i, norm_mix, w_in, q_norm, k_norm, cmp_pe, cmp_w1, cmp_b1, cmp_w2, cmp_b2, shift_mu, w0, w2,
                  a0, a2, g2, k_k, k_a, r_k, ln_w, ln_b, w_out, norm_ffn, w_gate, w_up, w_down, norm_ple,
                  w_ple_gate, w_ple):
    pair = lambda g: jnp.concatenate([g, g])[None]
    return dict(
        norm_mix=norm_mix[i][None], w_in=_pad_w_in(w_in[i]),
        qn=pair(q_norm[i]), kn_sel=pair(k_norm[i, 1]), kn_win=pair(k_norm[i, 2]),
        step_gains=(pair(q_norm[i]), k_norm[i, 1][:, None], k_norm[i, 2][:, None], k_norm[i, 1][None],
                    k_norm[i, 2][None]),
        cw=_compress_weights(cmp_pe[i], cmp_w1[i], cmp_b1[i], cmp_w2[i], cmp_b2[i], k_norm[i, 0]),
        rp=_rwkv_params(shift_mu[i], w0[i], w2[i], a0[i], a2[i], g2[i], k_k[i], k_a[i]),
        rk=r_k[i].reshape(1, RWKV_WIDTH), lnw=ln_w[i][None], lnb=ln_b[i][None],
        w_out=w_out[i].astype(BF16), norm_ffn=norm_ffn[i][None], w_gate=w_gate[i].astype(BF16),
        w_up=w_up[i].astype(BF16), w_down=w_down[i].astype(BF16), norm_ple=norm_ple[i][None],
        w_ple_gate=w_ple_gate[i].astype(BF16), w_ple=w_ple[i].astype(BF16))


def _prompt_layer(x, pe, lp, b, t):
    g, hd = NSA_KV_HEADS, HEAD_DIM
    q, kv, kw, gl, xr = _in_proj(x, lp["norm_mix"], lp["w_in"])
    kv = kv.reshape(b, t, KV_COLS)
    kw = kw.reshape(b, t, WIN_COLS)
    xr = xr.reshape(b, t, XR_PAD)
    kc = _compress_prompt(kv, lp["cw"])
    ksv = _kv_prep(kv, kw, lp["kn_sel"], lp["kn_win"])
    o_nsa = _nsa_prompt(q.reshape(b, t, Q_COLS), gl.reshape(b, t, LANES), lp["qn"], kc, ksv, t)
    o_rwkv, s_bd = _rwkv_scan(xr, jnp.zeros((b, 1, XR_PAD), F32), lp["rp"], lp["rk"], lp["lnw"], lp["lnb"])
    x = _out_ffn(x, o_nsa.reshape(b * t, Q_COLS), o_rwkv.reshape(b * t, RWKV_WIDTH), pe, lp)
    keep = min(WINDOW, t)
    return (x, kv.reshape(b, t, 4, g, hd), kw[:, t - keep:].reshape(b, keep, 2, g, hd), _unblock_states(s_bd),
            _unpad_xr(xr[:, -1]))


def _sample_layer(x, pe, lp, cache_t, pt_flat, n_pages, cwin_t, win_off, state_wkv, state_shift):
    db = x.shape[0]
    g, hd = NSA_KV_HEADS, HEAD_DIM
    q, kv, kw, gl, xr = _in_proj(x, lp["norm_mix"], lp["w_in"])
    kcn, _, vcn, _ = _compress_paged(cache_t, pt_flat, db, n_pages, lp["cw"])
    q3 = q.reshape(db, NSA_HEADS, hd)
    z = jnp.zeros((db, NSA_HEADS // 2, hd), F32)
    qm = jnp.concatenate([jnp.concatenate([q3[:, :4], z], -1), jnp.concatenate([z, q3[:, 4:]], -1)], axis=1)
    glm = jnp.pad(gl[:, :GATE_COLS].reshape(db, NSA_HEADS, 3), ((0, 0), (0, 0), (0, LANES - 3)))
    o8, win = _nsa_step(cache_t, pt_flat, qm, glm, kcn, vcn, kv.reshape(db, 4 * g, hd), cwin_t, win_off,
                        kw.reshape(db, 2 * g, hd), lp["step_gains"], n_pages)
    o_nsa = o8.reshape(db, Q_COLS)
    feats = _rwkv_prep_step(xr, _pad_xr(state_shift), lp["rp"])
    o_rwkv, s1 = _rwkv_step(state_wkv, feats, lp["rk"], lp["lnw"], lp["lnb"])
    x = _out_ffn(x, o_nsa, o_rwkv, pe, lp)
    return (x, kv.reshape(db, 1, 4, g, hd), win, s1, _unpad_xr(xr))


def kernel(x_prompt, x_sample, p_prompt, p_sample, cache_kv, cache_win, state_wkv, state_shift, page_table, norm_mix, w_in, q_norm, k_norm, cmp_pe, cmp_w1, cmp_b1, cmp_w2, cmp_b2, shift_mu, w0, w2, a0, a2, g2, k_k, k_a, r_k, ln_w, ln_b, w_out, norm_ffn, w_gate, w_up, w_down, norm_ple, w_ple_gate, w_ple):
    b, t, d = x_prompt.shape
    db = x_sample.shape[0]
    depth, n_phys, page = cache_kv.shape[:3]
    n_pages = page_table.shape[1]
    assert x_sample.shape[1] == 1 and d == D_MODEL
    rows_last = (0, 1, 3, 4, 5, 2)
    cache_t = cache_kv.transpose(rows_last).reshape(depth * n_phys, -1, HEAD_DIM, page)
    wbuf = cache_win.shape[2]
    cwin_t = cache_win.transpose(rows_last).reshape(depth * db, -1, HEAD_DIM, wbuf)
    pt_flat = page_table.reshape(-1).astype(jnp.int32)
    xp = x_prompt.reshape(b * t, d)
    xs = x_sample.reshape(db, d)
    outs_p, outs_s = [], []
    for i in range(depth):
        lp = _layer_params(i, norm_mix, w_in, q_norm, k_norm, cmp_pe, cmp_w1, cmp_b1, cmp_w2, cmp_b2, shift_mu,
                           w0, w2, a0, a2, g2, k_k, k_a, r_k, ln_w, ln_b, w_out, norm_ffn, w_gate, w_up, w_down,
                           norm_ple, w_ple_gate, w_ple)
        xp, *rest_p = _prompt_layer(xp, p_prompt[i].reshape(b * t, -1), lp, b, t)
        xs, *rest_s = _sample_layer(xs, p_sample[i].reshape(db, -1), lp, cache_t, pt_flat + i * n_phys, n_pages,
                                    cwin_t, i * db, state_wkv[i], state_shift[i])
        outs_p.append(rest_p)
        outs_s.append(rest_s)
    stack = lambda outs, k: jnp.stack([o[k] for o in outs])
    g = NSA_KV_HEADS
    win_s = stack(outs_s, 1).reshape(depth, db, 2, g, HEAD_DIM, wbuf).transpose(0, 1, 5, 2, 3, 4)
    return (xp.reshape(b, t, d), xs.reshape(db, 1, d),
            stack(outs_p, 0), stack(outs_s, 0), stack(outs_p, 1), win_s,
            stack(outs_p, 2), stack(outs_s, 2), stack(outs_p, 3), stack(outs_s, 3))
```

```python
import functools
import itertools

import jax
import jax.numpy as jnp
from jax import lax
from jax.experimental import pallas as pl
from jax.experimental.pallas import tpu as pltpu

F32 = jnp.float32
BF16 = jnp.bfloat16
HI = lax.Precision.HIGHEST

D_MODEL = 1024
HEAD_DIM = 64
LANES = 128
NSA_HEADS = 8
NSA_KV_HEADS = 2
RWKV_HEADS = 8
RWKV_WIDTH = RWKV_HEADS * HEAD_DIM
CMP_BLOCK = 32
CMP_STRIDE = 16
CMP_HIDDEN = 256
SEL_BLOCK = 64
SEL_TOPN = 16
WINDOW = 512
Q_BLOCK = 512
COMPRESS_SEQS = 2
NSA_STEP_SEQS = 4
WINDOW_Q = 512
DECAY_RANK = 64
A_RANK = 64
GATE_RANK = 128
Q_COLS = NSA_HEADS * HEAD_DIM
KV_COLS = 4 * NSA_KV_HEADS * HEAD_DIM
WIN_COLS = 2 * NSA_KV_HEADS * HEAD_DIM
GATE_COLS = 3 * NSA_HEADS
RWKV_PROJ = 3 * RWKV_WIDTH + DECAY_RANK + A_RANK + GATE_RANK
XR_PAD = 3 * RWKV_WIDTH + 3 * LANES
IN_PAD = Q_COLS + KV_COLS + WIN_COLS + LANES + XR_PAD
NORM_EPS = 1e-6
GN_EPS = 64e-5
BIG = 1e9
NEG = -1e30
RWKV_CHUNK = 32
RWKV_SEQS_PER_STEP = 4
VMEM_LIMIT =56 * 1024 * 1024


def _dot(a, b, prec=None):
    return lax.dot_general(a, b, (((1,), (0,)), ((), ())), precision=prec, preferred_element_type=F32)


def _dot_nt(a, b, prec=None):
    return lax.dot_general(a, b, (((1,), (1,)), ((), ())), precision=prec, preferred_element_type=F32)


def _dot_tn(a, b, prec=None):
    return lax.dot_general(a, b, (((0,), (0,)), ((), ())), precision=prec, preferred_element_type=F32)


def _iota(shape, dim):
    return lax.broadcasted_iota(jnp.int32, shape, dim)


def _lo_mask(shape):
    return (_iota(shape, len(shape) - 1) % LANES) < HEAD_DIM


def _seg_sum(x):
    lo = _lo_mask(x.shape)
    s_lo = jnp.sum(jnp.where(lo, x, 0.0), axis=-1, keepdims=True)
    s_hi = jnp.sum(jnp.where(lo, 0.0, x), axis=-1, keepdims=True)
    return jnp.where(lo, s_lo, s_hi)


def _seg_rmsnorm(x, gain):
    ms = _seg_sum(x * x) * (1.0 / HEAD_DIM)
    return x * lax.rsqrt(ms + NORM_EPS) * gain


def _rmsnorm_rows(x, gain):
    return x * lax.rsqrt(jnp.mean(x * x, axis=-1, keepdims=True) + NORM_EPS) * gain


def _sigmoid(x):
    return 1.0 / (1.0 + jnp.exp(-x))


def _masked_softmax(s, mask):
    sm = jnp.where(mask, s, NEG)
    m = jnp.max(sm, axis=-1, keepdims=True)
    m = jnp.where(m > 0.5 * NEG, m, 0.0)
    e = jnp.where(mask, jnp.exp(sm - m), 0.0)
    return e * (1.0 / jnp.maximum(jnp.sum(e, axis=-1, keepdims=True), 1e-30))


def _softmax_av(s, bias, v):
    s = s + bias
    e = jnp.exp(s - jnp.max(s, axis=-1, keepdims=True))
    return _dot(e.astype(BF16), v) * (1.0 / jnp.sum(e, axis=-1, keepdims=True))


def _params(sem):
    return pltpu.CompilerParams(dimension_semantics=sem, vmem_limit_bytes=VMEM_LIMIT)


def _const_spec(shape):
    nd = len(shape)
    return pl.BlockSpec(shape, lambda *_: (0,) * nd)


_IN_SPLITS = (("q", 0, Q_COLS), ("kv", Q_COLS, KV_COLS), ("kw", Q_COLS + KV_COLS, WIN_COLS),
              ("gl", Q_COLS + KV_COLS + WIN_COLS, LANES),
              ("xr", Q_COLS + KV_COLS + WIN_COLS + LANES, XR_PAD))


def _in_proj_kernel(x_ref, g_ref, w_ref, *out_refs):
    h = _rmsnorm_rows(x_ref[...], g_ref[...]).astype(BF16)
    for o_ref, (_, lo, n) in zip(out_refs, _IN_SPLITS):
        o_ref[...] = _dot(h, w_ref[:, lo:lo + n])


def _in_proj(x, gain, w_pad):
    m = x.shape[0]
    tm = min(m, 512)
    assert m % tm == 0
    return pl.pallas_call(
        _in_proj_kernel,
        grid=(m // tm,),
        in_specs=[pl.BlockSpec((tm, D_MODEL), lambda i: (i, 0)),
                  _const_spec((1, D_MODEL)),
                  _const_spec((D_MODEL, IN_PAD))],
        out_specs=[pl.BlockSpec((tm, n), lambda i: (i, 0)) for _, _, n in _IN_SPLITS],
        out_shape=[jax.ShapeDtypeStruct((m, n), F32) for _, _, n in _IN_SPLITS],
        compiler_params=_params(("parallel",)),
        name="in_proj",
    )(x, gain, w_pad)


def _pad_xr(v):
    z = jnp.zeros(v.shape[:-1] + (LANES - DECAY_RANK,), v.dtype)
    c = 3 * RWKV_WIDTH
    return jnp.concatenate([v[..., :c + DECAY_RANK], z, v[..., c + DECAY_RANK:c + DECAY_RANK + A_RANK], z,
                            v[..., c + DECAY_RANK + A_RANK:]], axis=-1)


def _unpad_xr(v):
    c = 3 * RWKV_WIDTH
    return jnp.concatenate([v[..., :c + DECAY_RANK], v[..., c + LANES:c + LANES + A_RANK], v[..., c + 2 * LANES:]],
                           axis=-1)


def _pad_w_in(w_in):
    c = Q_COLS + KV_COLS + WIN_COLS
    gl = jnp.pad(w_in[:, c:c + GATE_COLS], ((0, 0), (0, LANES - GATE_COLS)))
    return jnp.concatenate([w_in[:, :c], gl, _pad_xr(w_in[:, c + GATE_COLS:])], axis=-1).astype(BF16)


def _ffn_kernel(x_ref, on_ref, or_ref, pe_ref, wout_ref, nf_ref, wg_ref, wu_ref, wd_ref, np_ref, wpg_ref,
                wple_ref, y_ref, x1_s, hf_s, acc_s):
    j = pl.program_id(1)

    @pl.when(j == 0)
    def _():
        half = D_MODEL // 2
        o = _dot(on_ref[...].astype(BF16), wout_ref[0:half, :]) + _dot(or_ref[...].astype(BF16), wout_ref[half:, :])
        x1 = x_ref[...] + o
        x1_s[...] = x1
        hf_s[...] = _rmsnorm_rows(x1, nf_ref[...]).astype(BF16)
        acc_s[...] = jnp.zeros_like(acc_s)

    hf = hf_s[...]
    g = _dot(hf, wg_ref[...])
    u = _dot(hf, wu_ref[...])
    act = (g * _sigmoid(g)) * u
    acc_s[...] += _dot(act.astype(BF16), wd_ref[...])

    @pl.when(j == pl.num_programs(1) - 1)
    def _():
        x2 = x1_s[...] + acc_s[...]
        hp = _rmsnorm_rows(x2, np_ref[...]).astype(BF16)
        pg = _sigmoid(_dot(hp, wpg_ref[...]))
        y_ref[...] = x2 + pg * _dot(pe_ref[...].astype(BF16), wple_ref[...])


def _out_ffn(x, o_nsa, o_rwkv, pe, lw):
    m = x.shape[0]
    tm = min(m, 512)
    hid = lw["w_gate"].shape[1]
    th = 1408
    assert m % tm == 0 and hid % th == 0
    ple = pe.shape[1]
    row = lambda i, j: (i, 0)
    return pl.pallas_call(
        _ffn_kernel,
        grid=(m // tm, hid // th),
        in_specs=[pl.BlockSpec((tm, D_MODEL), row),
                  pl.BlockSpec((tm, D_MODEL // 2), row),
                  pl.BlockSpec((tm, D_MODEL // 2), row),
                  pl.BlockSpec((tm, ple), row),
                  _const_spec((D_MODEL, D_MODEL)),
                  _const_spec((1, D_MODEL)),
                  pl.BlockSpec((D_MODEL, th), lambda i, j: (0, j)),
                  pl.BlockSpec((D_MODEL, th), lambda i, j: (0, j)),
                  pl.BlockSpec((th, D_MODEL), lambda i, j: (j, 0)),
                  _const_spec((1, D_MODEL)),
                  _const_spec((D_MODEL, D_MODEL)),
                  _const_spec((ple, D_MODEL))],
        out_specs=pl.BlockSpec((tm, D_MODEL), row),
        out_shape=jax.ShapeDtypeStruct((m, D_MODEL), F32),
        scratch_shapes=[pltpu.VMEM((tm, D_MODEL), F32), pltpu.VMEM((tm, D_MODEL), BF16),
                        pltpu.VMEM((tm, D_MODEL), F32)],
        compiler_params=_params(("parallel", "arbitrary")),
        name="out_ffn_ple",
    )(x, o_nsa, o_rwkv, pe, lw["w_out"], lw["norm_ffn"], lw["w_gate"], lw["w_up"], lw["w_down"],
      lw["norm_ple"], lw["w_ple_gate"], lw["w_ple"])


def _gelu_tanh(x):
    return 0.5 * x * (1.0 + jnp.tanh(0.7978845608028654 * (x + 0.044715 * x * x * x)))


def _compress_body(load_rows, nbp, first_step, pe_ref, wbd_ref, b1_ref, w2lo_ref, w2hi_ref, b2_ref, kn_ref,
                   kcn_ref, kcr_ref, vcn_ref, vcr_ref, bias_s):
    hid = CMP_HIDDEN

    @pl.when(first_step)
    def _():
        for ty in range(2):
            acc = jnp.zeros((1, hid), F32)
            for r2 in range(CMP_STRIDE // 2):
                t = _dot(pe_ref[ty, r2:r2 + 1, :].astype(BF16), wbd_ref[ty, r2])
                acc = acc + t[:, 0:hid] + t[:, 3 * hid:4 * hid]
            bias_s[ty:ty + 1, :] = acc + b1_ref[ty:ty + 1, :]

    acc = [jnp.zeros((nbp, 4 * hid), F32) for _ in range(2)]
    for r2 in range(CMP_STRIDE // 2):
        for ty in range(2):
            lhs = jnp.concatenate([load_rows(2 * r2, ty), load_rows(2 * r2 + 1, ty)], axis=-1).astype(BF16)
            acc[ty] = acc[ty] + _dot(lhs, wbd_ref[ty, r2])
    outs = ((kcn_ref, kcr_ref), (vcn_ref, vcr_ref))
    for ty in range(2):
        hs = []
        for g in range(2):
            a = acc[ty][:, 2 * hid * g:2 * hid * g + hid]
            b = acc[ty][:, 2 * hid * g + hid:2 * hid * (g + 1)]
            b_next = pltpu.roll(b, nbp - 1, 0)
            hs.append(_gelu_tanh(a + b_next + bias_s[ty:ty + 1, :]).astype(BF16))
        o_n = _dot(hs[0], w2lo_ref[ty]) + _dot(hs[1], w2hi_ref[ty]) + b2_ref[ty:ty + 1, :]
        o_r = _dot(hs[0], w2hi_ref[ty]) + _dot(hs[1], w2lo_ref[ty]) + b2_ref[ty:ty + 1, :]
        if ty == 0:
            o_n = _seg_rmsnorm(o_n, kn_ref[...])
            o_r = _seg_rmsnorm(o_r, kn_ref[...])
        for o_ref, o in zip(outs[ty], (o_n, o_r)):
            o_ref[...] = o.astype(o_ref.dtype).reshape(o_ref.shape)


def _compress_prompt_kernel(k_ref, v_ref, *refs, nbp):
    load = lambda r, ty: (k_ref, v_ref)[ty][0, pl.ds(r, nbp, stride=CMP_STRIDE), :]
    _compress_body(load, nbp, pl.program_id(0) == 0, *refs)


def _compress_paged_kernel(pt_ref, *refs, n_pages, page):
    del pt_ref
    page_refs, rest, rows_s = refs[:n_pages], refs[n_pages:-2], refs[-2:]
    for j in range(n_pages):
        for ty in range(2):
            tiles = page_refs[j][0, 2 * ty:2 * ty + 2].reshape(2 * HEAD_DIM, page)
            rows_s[ty][j * page:(j + 1) * page, :] = tiles.T
    nbp = n_pages * page // CMP_STRIDE
    load = lambda r, ty: rows_s[ty][pl.ds(r, nbp, stride=CMP_STRIDE), :]
    _compress_body(load, nbp, pl.program_id(0) == 0, *rest)


def _compress_weight_specs():
    hid = CMP_HIDDEN
    return [_const_spec((2, CMP_STRIDE // 2, 2 * LANES)), _const_spec((2, CMP_STRIDE // 2, 2 * LANES, 4 * hid)),
            _const_spec((2, hid)), _const_spec((2, hid, LANES)), _const_spec((2, hid, LANES)),
            _const_spec((2, LANES)), _const_spec((1, LANES))]


def _compress_outs(b, nbp, per_step=1):
    spec = pl.BlockSpec((per_step, nbp, LANES), lambda i, *_: (i, 0, 0))
    shapes = [jax.ShapeDtypeStruct((b, nbp, LANES), dt) for dt in (F32, F32, BF16, BF16)]
    return [spec] * 4, shapes


def _compress_prompt(kv, cw):
    b, t, _ = kv.shape
    nbp = t // CMP_STRIDE
    out_specs, out_shape = _compress_outs(b, nbp)
    return pl.pallas_call(
        functools.partial(_compress_prompt_kernel, nbp=nbp),
        grid=(b,),
        in_specs=[pl.BlockSpec((1, t, LANES), lambda i: (i, 0, 0)),
                  pl.BlockSpec((1, t, LANES), lambda i: (i, 0, 1))] + _compress_weight_specs(),
        out_specs=out_specs, out_shape=out_shape,
        scratch_shapes=[pltpu.VMEM((2, CMP_HIDDEN), F32)],
        compiler_params=_params(("arbitrary",)),
        name="compress_prompt",
    )(kv, kv, *cw)


def _compress_paged(cache, pt_flat, n_seq, n_pages, cw):
    page = cache.shape[3]
    assert page == LANES
    nbp = n_pages * page // CMP_STRIDE
    sq = COMPRESS_SEQS if n_seq % COMPRESS_SEQS == 0 else 1
    out_specs, out_shape = _compress_outs(n_seq, nbp, sq)
    page_specs = [pl.BlockSpec((1, 4, HEAD_DIM, page), functools.partial(
        lambda i, pt, j: (pt[i * sq * n_pages + j], 0, 0, 0), j=j)) for j in range(sq * n_pages)]
    return pl.pallas_call(
        functools.partial(_compress_paged_kernel, n_pages=sq * n_pages, page=page),
        grid_spec=pltpu.PrefetchScalarGridSpec(
            num_scalar_prefetch=1, grid=(n_seq // sq,),
            in_specs=page_specs + _compress_weight_specs(),
            out_specs=out_specs,
            scratch_shapes=[pltpu.VMEM((2, CMP_HIDDEN), F32), pltpu.VMEM((sq * n_pages * page, LANES), F32),
                            pltpu.VMEM((sq * n_pages * page, LANES), F32)]),
        out_shape=out_shape,
        compiler_params=_params(("arbitrary",)),
        name="compress_paged",
    )(pt_flat, *([cache] * (sq * n_pages)), *cw)


def _compress_weights(cmp_pe, cmp_w1, cmp_b1, cmp_w2, cmp_b2, k_norm0):
    hid = CMP_HIDDEN
    w1 = cmp_w1.reshape(2, CMP_BLOCK, HEAD_DIM, hid)
    wcat = jnp.concatenate([w1[:, :CMP_STRIDE], w1[:, CMP_STRIDE:]], axis=-1)
    z = jnp.zeros_like(wcat)
    wbd = jnp.concatenate([jnp.concatenate([wcat, z], -1), jnp.concatenate([z, wcat], -1)], axis=2).astype(BF16)
    wbd = wbd.reshape(2, CMP_STRIDE // 2, 2 * LANES, 4 * hid)
    pe_pair = jnp.concatenate([cmp_pe[:, :CMP_STRIDE], cmp_pe[:, CMP_STRIDE:]], axis=-1)
    pe_pair = pe_pair.reshape(2, CMP_STRIDE // 2, 2 * LANES)
    z2 = jnp.zeros_like(cmp_w2)
    w2lo = jnp.concatenate([cmp_w2, z2], -1).astype(BF16)
    w2hi = jnp.concatenate([z2, cmp_w2], -1).astype(BF16)
    b2 = jnp.concatenate([cmp_b2, cmp_b2], -1)
    kn = jnp.concatenate([k_norm0, k_norm0])[None]
    return (pe_pair, wbd, cmp_b1, w2lo, w2hi, b2, kn)


def _kv_prep_kernel(kv_ref, kw_ref, kns_ref, knw_ref, ksn_ref, ksr_ref, vsn_ref, vsr_ref, kwn_ref, kwr_ref,
                    vwn_ref, vwr_ref):
    def put(x, n_ref, r_ref):
        n_ref[0] = x.astype(BF16)
        r_ref[0] = pltpu.roll(x, HEAD_DIM, 1).astype(BF16)

    put(_seg_rmsnorm(kv_ref[0, :, 2 * LANES:3 * LANES], kns_ref[...]), ksn_ref, ksr_ref)
    put(kv_ref[0, :, 3 * LANES:4 * LANES], vsn_ref, vsr_ref)
    put(_seg_rmsnorm(kw_ref[0, :, 0:LANES], knw_ref[...]), kwn_ref, kwr_ref)
    put(kw_ref[0, :, LANES:2 * LANES], vwn_ref, vwr_ref)


def _kv_prep(kv, kw, kn_sel, kn_win):
    b, t, _ = kv.shape
    tm = min(t, 512)
    spec = lambda n: pl.BlockSpec((1, tm, n), lambda i, j: (i, j, 0))
    return pl.pallas_call(
        _kv_prep_kernel,
        grid=(b, t // tm),
        in_specs=[spec(4 * LANES), spec(2 * LANES), _const_spec((1, LANES)), _const_spec((1, LANES))],
        out_specs=[spec(LANES)] * 8,
        out_shape=[jax.ShapeDtypeStruct((b, t, LANES), BF16)] * 8,
        compiler_params=_params(("parallel", "parallel")),
        name="kv_prep",
    )(kv, kw, kn_sel, kn_win)


def _overlap_t(ns, nbp):
    j = _iota((ns, nbp), 0) * SEL_BLOCK
    n = _iota((ns, nbp), 1) * CMP_STRIDE
    return jnp.where((n < j + SEL_BLOCK) & (n + CMP_BLOCK > j), 1.0, 0.0).astype(F32)


def _nsa_prompt_kernel(q_ref, gl_ref, qn_ref, kcn_ref, kcr_ref, vcn_ref, vcr_ref, ksn_ref, ksr_ref, vsn_ref,
                       vsr_ref, kwn_ref, kwr_ref, vwn_ref, vwr_ref, o_ref, *, seq, tk, wq, wk):
    i = pl.program_id(1)
    qb = Q_BLOCK
    nbp = kcn_ref.shape[1]
    nb = nbp - 1
    ns = seq // SEL_BLOCK
    n_sel = min(SEL_TOPN, ns)
    lo = _lo_mask((qb, LANES))
    pos = i * qb + _iota((qb, 1), 0)
    sig = _sigmoid(gl_ref[0])

    qp = [_seg_rmsnorm(q_ref[0, :, LANES * p:LANES * (p + 1)], qn_ref[...]) * (HEAD_DIM ** -0.5) for p in range(4)]
    q_lo = [jnp.where(lo, x, 0.0) for x in qp]
    q_hi = [jnp.where(lo, 0.0, x) for x in qp]

    n_c = _iota((qb, nbp), 1)
    mask_c = (n_c * CMP_STRIDE + (CMP_BLOCK - 1) <= pos) & (n_c < nb)
    ov_t = _overlap_t(ns, nbp)
    j_t = _iota((ns, qb), 0)
    pos_t = i * qb + _iota((ns, qb), 1)
    cur_t = pos_t >> 6
    forced_t = (j_t == 0) | (j_t == cur_t) | (j_t == cur_t - 1)
    causal_t = j_t * SEL_BLOCK <= pos_t
    eye = jnp.where(_iota((qb, qb), 0) == _iota((qb, qb), 1), 1.0, 0.0).astype(BF16)

    n_tiles = (i * qb + qb - 1) // tk + 1

    order = lambda g, n, r: (n, r) if g == 0 else (r, n)
    heads = []
    for p in range(4):
        heads += [(p, p // 2, q_lo[p], 0), (p, p // 2, q_hi[p], 1)]

    split = lambda x: (x.astype(BF16), (x - x.astype(BF16).astype(F32)).astype(BF16))
    kc_parts = [split(kcn_ref[0]), split(kcr_ref[0])]

    def score3(qt, kc):
        q_h, q_l = split(qt)
        return _dot_nt(q_h, kc[0]) + (_dot_nt(q_h, kc[1]) + _dot_nt(q_l, kc[0]))

    p_cmp = [_masked_softmax(score3(qt, order(g, *kc_parts)[w]), mask_c) for _, g, qt, w in heads]
    o_cmp = [_dot(pc.astype(BF16), order(g, vcn_ref, vcr_ref)[w][0]) for pc, (_, g, _, w) in zip(p_cmp, heads)]
    unsel = []
    for g in range(2):
        psum = p_cmp[4 * g] + p_cmp[4 * g + 1] + p_cmp[4 * g + 2] + p_cmp[4 * g + 3]
        imp_t = _dot_nt(ov_t, psum, HI)
        score = jnp.where(forced_t, BIG, jnp.where(causal_t, imp_t, -BIG))
        rank = jnp.zeros((ns, qb), jnp.int32)
        for jp in range(ns):
            row = score[jp:jp + 1, :]
            rank = rank + jnp.where((row > score) | ((row == score) & (j_t > jp)), 1, 0)
        unsel.append(_dot_nt(eye, jnp.where(rank < n_sel, 0.0, NEG).astype(BF16)).astype(BF16))

    q16 = [qt.astype(BF16) for _, _, qt, _ in heads]

    def tile(kt, carry, diagonal):
        k0 = pl.multiple_of(kt * tk, tk)
        blk = (k0 + _iota((ns, tk), 1)) >> 6
        expand = jnp.where(_iota((ns, tk), 0) == blk, 1.0, 0.0).astype(BF16)
        bias = [_dot(u, expand) for u in unsel]
        if diagonal:
            causal = k0 + _iota((qb, tk), 1) <= pos
            bias = [jnp.where(causal, x, NEG) for x in bias]
        out = []
        for (m, l, acc), q_, (_, g, _, w) in zip(carry, q16, heads):
            k_ref = order(g, ksn_ref, ksr_ref)[w]
            v_ref = order(g, vsn_ref, vsr_ref)[w]
            s = _dot_nt(q_, k_ref[0, pl.ds(k0, tk), :]) + bias[g]
            m_new = jnp.maximum(m, jnp.max(s, axis=-1, keepdims=True))
            alpha = jnp.exp(m - m_new)
            e = jnp.exp((s - m_new).astype(BF16))
            l_new = alpha * l + jnp.sum(e.astype(F32), axis=-1, keepdims=True)
            acc_new = alpha * acc + _dot(e, v_ref[0, pl.ds(k0, tk), :])
            out.append((m_new, l_new, acc_new))
        return tuple(out)

    init = (jnp.full((qb, 1), NEG, F32), jnp.zeros((qb, 1), F32), jnp.zeros((qb, LANES), F32))
    carry = lax.fori_loop(0, n_tiles - 1, functools.partial(tile, diagonal=False), (init,) * len(heads))
    o_sel = [acc / l for _, l, acc in tile(n_tiles - 1, carry, True)]

    o_win = [[] for _ in heads]
    for sub in range(qb // wq):
        w_start = pl.multiple_of(jnp.maximum(i * qb + sub * wq - (wk - wq), 0), wq)
        w_dist = pos[sub * wq:(sub + 1) * wq] - (w_start + _iota((wq, wk), 1))
        bias_w = jnp.where((w_dist >= 0) & (w_dist <= WINDOW), 0.0, NEG)
        for n, (q_, (_, g, _, w)) in enumerate(zip(q16, heads)):
            o_win[n].append(_softmax_av(
                _dot_nt(q_[sub * wq:(sub + 1) * wq], order(g, kwn_ref, kwr_ref)[w][0, pl.ds(w_start, wk), :]), bias_w,
                order(g, vwn_ref, vwr_ref)[w][0, pl.ds(w_start, wk), :]))
    o_win = [jnp.concatenate(parts, axis=0) for parts in o_win]

    for p in range(4):
        gate = lambda c: jnp.where(lo, sig[:, 6 * p + c:6 * p + c + 1], sig[:, 6 * p + 3 + c:6 * p + 4 + c])
        both = lambda o: jnp.where(lo, o[2 * p], o[2 * p + 1])
        o_ref[0, :, LANES * p:LANES * (p + 1)] = gate(0) * both(o_cmp) + gate(1) * both(o_sel) + gate(2) * both(o_win)


def _nsa_prompt(q, gl, qn, kc, ksv, seq):
    b = q.shape[0]
    nbp = kc[0].shape[1]
    tk = min(seq, 512)
    wq = min(Q_BLOCK, WINDOW_Q)
    wk = min(seq, WINDOW + wq)
    assert seq % Q_BLOCK == 0 and seq % tk == 0
    blk = lambda n: pl.BlockSpec((1, Q_BLOCK, n), lambda bi, i: (bi, i, 0))
    full = lambda rows: pl.BlockSpec((1, rows, LANES), lambda bi, i: (bi, 0, 0))
    return pl.pallas_call(
        functools.partial(_nsa_prompt_kernel, seq=seq, tk=tk, wq=wq, wk=wk),
        grid=(b, seq // Q_BLOCK),
        in_specs=[blk(4 * LANES), blk(LANES), _const_spec((1, LANES))] + [full(nbp)] * 4 + [full(seq)] * 8,
        out_specs=blk(4 * LANES),
        out_shape=jax.ShapeDtypeStruct((b, seq, 4 * LANES), F32),
        compiler_params=_params(("parallel", "arbitrary")),
        name="nsa_prompt",
    )(q, gl, qn, *kc, *ksv)


def _nsa_step_kernel(pt_ref, *refs, n_pages, page, past):
    del pt_ref
    n_page_refs = len(refs) - 14
    page_refs = refs[:n_page_refs]
    (qm_ref, gl_ref, kcn_ref, vcn_ref, kvn_ref, cw_ref, kwn_ref, qn_ref, kns_ref, knw_ref, kns_row_ref, knw_row_ref,
     o_ref, win_ref) = refs[n_page_refs:]
    nh = NSA_HEADS
    hd = HEAD_DIM
    nbp = kcn_ref.shape[1]
    nb = nbp - 1
    ns = past // SEL_BLOCK + 1
    n_sel = min(SEL_TOPN, ns)
    wbuf = cw_ref.shape[3]
    rep = NSA_HEADS // NSA_KV_HEADS
    first_group = _iota((nh, 1), 0) < rep

    def unpair(x):
        return jnp.where(first_group, x, pltpu.roll(x, hd, 1))[:, 0:hd]

    def by_group(f):
        return jnp.where(first_group, f(0), f(1))

    def norm_t(t, gain_ref):
        return (t * lax.rsqrt(jnp.mean(t * t, axis=0, keepdims=True) + NORM_EPS) * gain_ref[...]).astype(BF16)

    def one(s):
        qn = _seg_rmsnorm(qm_ref[s], qn_ref[...]) * (HEAD_DIM ** -0.5)
        q8 = unpair(qn)
        q8b = q8.astype(BF16)
        sg = _sigmoid(gl_ref[s])
        yield

        n_c = _iota((nh, nbp), 1)
        mask_c = (n_c * CMP_STRIDE + (CMP_BLOCK - 1) <= past) & (n_c < nb)
        p_c = _masked_softmax(_dot_nt(qn, kcn_ref[s], HI), mask_c)
        yield
        o_cmp = unpair(_dot(p_c.astype(BF16), vcn_ref[s]))
        same_group = jnp.where(_iota((nh, nh), 0) // rep == _iota((nh, nh), 1) // rep, 1.0, 0.0).astype(F32)
        psum = _dot(same_group, p_c, HI)
        yield
        imp = _dot_nt(psum, _overlap_t(LANES, nbp), HI)
        yield
        j = _iota((nh, LANES), 1)
        cur = past // SEL_BLOCK
        forced = (j == 0) | (j == cur) | (j == cur - 1)
        score = jnp.where(forced, BIG, jnp.where(j * SEL_BLOCK <= past, imp, -BIG))
        score = jnp.where(j < ns, score, -4.0 * BIG)
        rank = jnp.zeros((nh, LANES), jnp.int32)
        for jp in range(ns):
            col = score[:, jp:jp + 1]
            rank = rank + jnp.where((col > score) | ((col == score) & (j > jp)), 1, 0)
        sel = jnp.where((rank < n_sel) & (j < ns), 1.0, 0.0).astype(BF16)
        yield

        kv_new = kvn_ref[s]
        k_new = by_group(lambda g: _rmsnorm_rows(kv_new[4 + g:5 + g, :], kns_row_ref[...]))
        s_new = jnp.sum(q8 * k_new, axis=-1, keepdims=True)
        s_parts, m_parts = [], []
        for pj in range(n_pages):
            blk = (pj * page + _iota((LANES, page), 1)) >> 6
            expand = jnp.where(_iota((LANES, page), 0) == blk, 1.0, 0.0).astype(BF16)
            m_parts.append(_dot(sel, expand) > 0.5)
            s_parts.append(by_group(lambda g: _dot(q8b, norm_t(page_refs[s * n_pages + pj][0, g], kns_ref))))
        s_all = jnp.concatenate(s_parts, axis=-1)
        valid = jnp.concatenate(m_parts, axis=-1)
        s_all = jnp.where(valid, s_all, NEG)
        yield
        m = jnp.maximum(jnp.max(s_all, axis=-1, keepdims=True), s_new)
        e_all = jnp.where(valid, jnp.exp(s_all - m), 0.0)
        e_new = jnp.exp(s_new - m)
        yield
        acc = e_new * by_group(lambda g: kv_new[6 + g:7 + g, :])
        for pj in range(n_pages):
            e_pj = e_all[:, pj * page:(pj + 1) * page].astype(BF16)
            acc = acc + by_group(lambda g: _dot_nt(e_pj, page_refs[s * n_pages + pj][0, 2 + g].astype(BF16)))
        o_sel = acc / (jnp.sum(e_all, axis=-1, keepdims=True) + e_new)
        yield

        kw_new = kwn_ref[s]
        s_w = by_group(lambda g: _dot(q8b, norm_t(cw_ref[s, g], knw_ref)))
        s_wn = jnp.sum(q8 * by_group(lambda g: _rmsnorm_rows(kw_new[g:g + 1, :], knw_row_ref[...])), axis=-1,
                       keepdims=True)
        m_w = jnp.maximum(jnp.max(s_w, axis=-1, keepdims=True), s_wn)
        e_w = jnp.exp(s_w - m_w)
        e_wn = jnp.exp(s_wn - m_w)
        e_wb = e_w.astype(BF16)
        yield
        o_win = (by_group(lambda g: _dot_nt(e_wb, cw_ref[s, 2 + g].astype(BF16))) + e_wn * by_group(
            lambda g: kw_new[2 + g:3 + g, :])) / (jnp.sum(e_w, axis=-1, keepdims=True) + e_wn)

        o_ref[s] = sg[:, 0:1] * o_cmp + sg[:, 1:2] * o_sel + sg[:, 2:3] * o_win

        eye = _iota((hd, hd), 0) == _iota((hd, hd), 1)
        last = _iota((hd, wbuf), 1) == wbuf - 1
        for c in range(4):
            col = jnp.sum(jnp.where(eye, kw_new[c:c + 1, :], 0.0), axis=1, keepdims=True)
            win_ref[s, c] = jnp.where(last, col, pltpu.roll(cw_ref[s, c], wbuf - 1, 1))

    for _ in itertools.zip_longest(*[one(s) for s in range(qm_ref.shape[0])]):
        pass


def _nsa_step(cache, pt_flat, qm, glm, kcn, vcn, kv_new, cache_win, win_off, kw_new, gains, n_pages):
    db = qm.shape[0]
    page = cache.shape[3]
    past = n_pages * page
    nbp = kcn.shape[1]
    wbuf = cache_win.shape[3]
    hd = HEAD_DIM
    assert wbuf == min(WINDOW, past) and page == LANES and past // SEL_BLOCK + 1 <= LANES
    sq = NSA_STEP_SEQS if db % NSA_STEP_SEQS == 0 and win_off % NSA_STEP_SEQS == 0 else 1
    per = lambda shape: pl.BlockSpec((sq,) + shape, lambda i, pt: (i,) + (0,) * len(shape))
    page_specs = [pl.BlockSpec((1, 4, hd, page), functools.partial(
        lambda i, pt, s, j: (pt[(i * sq + s) * n_pages + j], 1, 0, 0), s=s, j=j))
        for s in range(sq) for j in range(n_pages)]
    return pl.pallas_call(
        functools.partial(_nsa_step_kernel, n_pages=n_pages, page=page, past=past),
        grid_spec=pltpu.PrefetchScalarGridSpec(
            num_scalar_prefetch=1, grid=(db // sq,),
            in_specs=page_specs + [per((NSA_HEADS, LANES)), per((NSA_HEADS, LANES)), per((nbp, LANES)),
                                   per((nbp, LANES)), per((8, hd)),
                                   pl.BlockSpec((sq, 4, hd, wbuf), lambda i, pt: (i + win_off // sq, 0, 0, 0)),
                                   per((4, hd))]
            + [_const_spec((1, LANES)), _const_spec((hd, 1)), _const_spec((hd, 1)), _const_spec((1, hd)),
               _const_spec((1, hd))],
            out_specs=[per((NSA_HEADS, hd)), per((4, hd, wbuf))]),
        out_shape=[jax.ShapeDtypeStruct((db, NSA_HEADS, hd), F32),
                   jax.ShapeDtypeStruct((db, 4, hd, wbuf), F32)],
        compiler_params=_params(("parallel",)),
        name="nsa_step",
    )(pt_flat, *([cache] * (sq * n_pages)), qm, glm, kcn, vcn, kv_new, cache_win, kw_new, *gains)


def _softplus(z):
    return jnp.maximum(z, 0.0) + jnp.log(1.0 + jnp.exp(-jnp.abs(z)))


def _rwkv_features(x, prev, mu_ref, w0_ref, a0_ref, w2_ref, a2_ref, g2_ref, kk_ref, ka_ref):
    rw = RWKV_WIDTH
    xs = x + (prev - x) * mu_ref[...]
    r, k, v = xs[:, 0:rw], xs[:, rw:2 * rw], xs[:, 2 * rw:3 * rw]
    wd = xs[:, 3 * rw:3 * rw + LANES]
    ad = xs[:, 3 * rw + LANES:3 * rw + 2 * LANES]
    gd = xs[:, 3 * rw + 2 * LANES:3 * rw + 3 * LANES]
    w_log = -_softplus(-(w0_ref[...] + _dot(jnp.tanh(wd).astype(BF16), w2_ref[...]))) - 0.5
    a_sig = _sigmoid(a0_ref[...] + _dot(ad.astype(BF16), a2_ref[...]))
    kk = k * kk_ref[...]
    kk2 = kk * kk
    norm2 = jnp.concatenate([_seg_sum(kk2[:, LANES * p:LANES * (p + 1)]) for p in range(rw // LANES)], axis=-1)
    kk = kk / jnp.maximum(jnp.sqrt(norm2), 1e-12)
    return (r, -jnp.exp(w_log), k * (1.0 + (a_sig - 1.0) * ka_ref[...]), v, -kk, kk * a_sig,
            _dot(_sigmoid(gd).astype(BF16), g2_ref[...]))


def _rwkv_prep_step_kernel(xr_ref, shift_ref, *refs):
    for o_ref, val in zip(refs[8:15], _rwkv_features(xr_ref[...], shift_ref[...], *refs[:8])):
        o_ref[...] = val


def _rwkv_param_specs():
    one = _const_spec((1, RWKV_WIDTH))
    low = _const_spec((LANES, RWKV_WIDTH))
    return [_const_spec((1, XR_PAD)), one, one, low, low, low, one, one]


def _rwkv_prep_step(xr, shift0, rp):
    m = xr.shape[0]
    return pl.pallas_call(
        _rwkv_prep_step_kernel,
        grid=(1,),
        in_specs=[_const_spec((m, XR_PAD)), _const_spec((m, XR_PAD))] + _rwkv_param_specs(),
        out_specs=[_const_spec((m, RWKV_WIDTH))] * 7,
        out_shape=[jax.ShapeDtypeStruct((m, RWKV_WIDTH), F32)] * 7,
        compiler_params=_params(("arbitrary",)),
        name="rwkv_prep_step",
    )(xr, shift0, *rp)


def _rwkv_params(shift_mu, w0, w2, a0, a2, g2, k_k, k_a):
    pad = lambda w: jnp.pad(w, ((0, LANES - w.shape[0]), (0, 0))).astype(BF16)
    return (_pad_xr(shift_mu)[None], w0[None], a0[None], pad(w2), pad(a2), g2.astype(BF16), k_k[None], k_a[None])


def _rwkv_finish(y, r, k, v, gate, rk, lnw, lnb):
    mu = _seg_sum(y) * (1.0 / HEAD_DIM)
    d = y - mu
    var = _seg_sum(d * d) * (1.0 / HEAD_DIM)
    yn = d * lax.rsqrt(var + GN_EPS) * lnw + lnb
    return (yn + _seg_sum(r * k * rk) * v) * gate


def _rwkv_scan_kernel(xr_ref, shift_ref, *refs):
    feat_params, (rk_ref, lnw_ref, lnb_ref, o_ref, s_out_ref, s_s, last_s) = refs[:8], refs[8:]
    c = pl.program_id(1)
    nseq, ch = xr_ref.shape[:2]
    gw = 4 * HEAD_DIM
    rows = 4 * ch

    @pl.when(c == 0)
    def _():
        s_s[...] = jnp.zeros_like(s_s)
        last_s[...] = shift_ref[...]

    x = xr_ref[...].reshape(nseq * ch, XR_PAD)
    prev = pltpu.roll(x, 1, 0)
    row = _iota(x.shape, 0)
    for i in range(nseq):
        prev = jnp.where(row == i * ch, last_s[i], prev)
        last_s[i] = x[(i + 1) * ch - 1:(i + 1) * ch, :]
    feats = _rwkv_features(x, prev, *feat_params)

    tril = jnp.where(_iota((ch, ch), 0) >= _iota((ch, ch), 1), 1.0, 0.0).astype(F32)
    t_row = _iota((rows, rows), 0) % ch
    t_col = _iota((rows, rows), 1) % ch
    strict = t_row > t_col
    incl = t_row >= t_col
    eye = jnp.where(_iota((rows, rows), 0) == _iota((rows, rows), 1), 1.0, 0.0).astype(F32)
    head_of_lane = _iota((ch, gw), 1) // HEAD_DIM

    def wide(x):
        return jnp.concatenate([jnp.where(head_of_lane == h, x, 0.0) for h in range(4)], axis=0).astype(BF16)

    units = [(i, q) for i in range(nseq) for q in range(RWKV_WIDTH // gw)]
    each = lambda f, *cols: [f(*xs) for xs in zip(*cols)]
    part = lambda f: [f[i * ch:(i + 1) * ch, gw * q:gw * (q + 1)] for i, q in units]
    r, lw, k, v, a, b, gate = (part(f) for f in feats)
    s0 = [s_s[i, q] for i, q in units]
    cl = each(lambda x: _dot(tril, x, HI), lw)
    g_in = each(jnp.exp, cl)
    g_inv = each(lambda x: jnp.exp(-x), cl)
    w_ar = each(lambda a_, r_, cl_, lw_, gi: jnp.concatenate([wide(a_ * jnp.exp(cl_ - lw_)), wide(r_ * gi)], axis=0),
                a, r, cl, lw, g_in)
    w_bk = each(lambda b_, k_, gv: jnp.concatenate([wide(b_ * gv), wide(k_ * gv)], axis=0), b, k, g_inv)
    w_v = each(wide, v)
    p_all = each(_dot_nt, w_ar, w_bk)
    xy0 = each(lambda x, s: _dot_nt(x, s.astype(BF16)), w_ar, s0)
    n = each(lambda p: jnp.where(strict, p[:rows, :rows], 0.0), p_all)
    l_ak = each(lambda p: jnp.where(strict, p[:rows, rows:], 0.0).astype(BF16), p_all)
    m_rb = each(lambda p: jnp.where(incl, p[rows:, :rows], 0.0).astype(BF16), p_all)
    m_rk = each(lambda p: jnp.where(incl, p[rows:, rows:], 0.0).astype(BF16), p_all)
    t_inv = each(lambda x: eye + x, n)
    pw = each(lambda x: x.astype(BF16), n)
    for _ in range(max(ch.bit_length() - 2, 0)):
        pw = each(lambda x: _dot(x, x).astype(BF16), pw)
        t_inv = each(lambda t, x: t + _dot(t.astype(BF16), x), t_inv, pw)
    x_w = each(lambda xy, l, vv: (xy[:rows] + _dot(l, vv)).astype(BF16), xy0, l_ak, w_v)
    u = each(lambda t, x: _dot(t.astype(BF16), x).astype(BF16), t_inv, x_w)
    y_w = each(lambda xy, mb, uu, mk, vv: xy[rows:] + _dot(mb, uu) + _dot(mk, vv), xy0, m_rb, u, m_rk, w_v)
    upd = each(lambda uu, vv, bk: _dot_tn(jnp.concatenate([uu, vv], axis=0), bk), u, w_v, w_bk)
    for j, (i, q) in enumerate(units):
        s_s[i, q] = (s0[j] + upd[j]) * g_in[j][ch - 1:ch, :]
        y = y_w[j][0:ch] + y_w[j][ch:2 * ch] + y_w[j][2 * ch:3 * ch] + y_w[j][3 * ch:4 * ch]
        for p in range(gw // LANES):
            ps = slice(LANES * p, LANES * (p + 1))
            po = slice(gw * q + LANES * p, gw * q + LANES * (p + 1))
            o_ref[i, :, po] = _rwkv_finish(y[:, ps], r[j][:, ps], k[j][:, ps], v[j][:, ps], gate[j][:, ps],
                                           rk_ref[:, po], lnw_ref[:, po], lnb_ref[:, po])

    @pl.when(c == pl.num_programs(1) - 1)
    def _():
        s_out_ref[...] = s_s[...]


def _rwkv_scan(xr, shift0, rp, rk, lnw, lnb):
    b, t, _ = xr.shape
    ch = min(t, RWKV_CHUNK)
    assert t % ch == 0 and ch & (ch - 1) == 0
    bb = RWKV_SEQS_PER_STEP if b % RWKV_SEQS_PER_STEP == 0 else 1
    blk = lambda n: pl.BlockSpec((bb, ch, n), lambda i, c: (i, c, 0))
    one = _const_spec((1, RWKV_WIDTH))
    gw = 4 * HEAD_DIM
    n_groups = RWKV_WIDTH // gw
    return pl.pallas_call(
        _rwkv_scan_kernel,
        grid=(b // bb, t // ch),
        in_specs=[blk(XR_PAD), pl.BlockSpec((bb, 1, XR_PAD), lambda i, c: (i, 0, 0))] + _rwkv_param_specs()
        + [one] * 3,
        out_specs=[blk(RWKV_WIDTH), pl.BlockSpec((bb, n_groups, gw, gw), lambda i, c: (i, 0, 0, 0))],
        out_shape=[jax.ShapeDtypeStruct((b, t, RWKV_WIDTH), F32),
                   jax.ShapeDtypeStruct((b, n_groups, gw, gw), F32)],
        scratch_shapes=[pltpu.VMEM((bb, n_groups, gw, gw), F32), pltpu.VMEM((bb, 1, XR_PAD), F32)],
        compiler_params=_params(("parallel", "arbitrary")),
        name="rwkv_scan",
    )(xr, shift0, *rp, rk, lnw, lnb)


def _unblock_states(s_bd):
    hd = HEAD_DIM
    blocks = [s_bd[:, :, hd * h:hd * (h + 1), hd * h:hd * (h + 1)] for h in range(s_bd.shape[2] // hd)]
    return jnp.stack(blocks, axis=2).reshape(s_bd.shape[0], -1, hd, hd)


def _rwkv_step_kernel(s_ref, r_ref, lw_ref, k_ref, v_ref, a_ref, b_ref, g_ref, rk_ref, lnw_ref, lnb_ref,
                      o_ref, s_out_ref):
    hd = HEAD_DIM
    eye = _iota((hd, hd), 0) == _iota((hd, hd), 1)
    for h in range(RWKV_HEADS):
        s0 = s_ref[0, h]
        r, lw, k, v, a, b = (ref[0, h] for ref in (r_ref, lw_ref, k_ref, v_ref, a_ref, b_ref))
        v_col = jnp.sum(jnp.where(eye, v, 0.0), axis=-1, keepdims=True)
        sa = jnp.sum(s0 * a, axis=-1, keepdims=True)
        s1 = s0 * jnp.exp(lw) + sa * b + v_col * k
        s_out_ref[0, h] = s1
        y_col = jnp.sum(s1 * r, axis=-1, keepdims=True)
        y = jnp.sum(jnp.where(eye, y_col, 0.0), axis=0, keepdims=True)
        mu = jnp.mean(y, axis=-1, keepdims=True)
        d = y - mu
        var = jnp.mean(d * d, axis=-1, keepdims=True)
        yn = d * lax.rsqrt(var + GN_EPS) * lnw_ref[h] + lnb_ref[h]
        bonus = jnp.sum(r * k * rk_ref[h], axis=-1, keepdims=True) * v
        o_ref[0, h] = (yn + bonus) * g_ref[0, h]


def _rwkv_step(state, feats, rk, lnw, lnb):
    m = state.shape[0]
    hd, nh = HEAD_DIM, RWKV_HEADS
    heads = lambda x: x.reshape(x.shape[:-1] + (nh, 1, hd))
    vec = pl.BlockSpec((1, nh, 1, hd), lambda i: (i, 0, 0, 0))
    st = pl.BlockSpec((1, nh, hd, hd), lambda i: (i, 0, 0, 0))
    par = _const_spec((nh, 1, hd))
    o, s1 = pl.pallas_call(
        _rwkv_step_kernel,
        grid=(m,),
        in_specs=[st] + [vec] * 7 + [par] * 3,
        out_specs=[vec, st],
        out_shape=[jax.ShapeDtypeStruct((m, nh, 1, hd), F32), jax.ShapeDtypeStruct((m, nh, hd, hd), F32)],
        compiler_params=_params(("parallel",)),
        name="rwkv_step",
    )(state, *[heads(f) for f in feats], heads(rk)[0], heads(lnw)[0], heads(lnb)[0])
    return o.reshape(m, nh * hd), s1


def _layer_params(i, norm_mix, w_in, q_norm, k_norm, cmp_pe, cmp_w1, cmp_b1, cmp_w2, cmp_b2, shift_mu, w0, w2,
                  a0, a2, g2, k_k, k_a, r_k, ln_w, ln_b, w_out, norm_ffn, w_gate, w_up, w_down, norm_ple,
                  w_ple_gate, w_ple):
    pair = lambda g: jnp.concatenate([g, g])[None]
    return dict(
        norm_mix=norm_mix[i][None], w_in=_pad_w_in(w_in[i]),
        qn=pair(q_norm[i]), kn_sel=pair(k_norm[i, 1]), kn_win=pair(k_norm[i, 2]),
        step_gains=(pair(q_norm[i]), k_norm[i, 1][:, None], k_norm[i, 2][:, None], k_norm[i, 1][None],
                    k_norm[i, 2][None]),
        cw=_compress_weights(cmp_pe[i], cmp_w1[i], cmp_b1[i], cmp_w2[i], cmp_b2[i], k_norm[i, 0]),
        rp=_rwkv_params(shift_mu[i], w0[i], w2[i], a0[i], a2[i], g2[i], k_k[i], k_a[i]),
        rk=r_k[i].reshape(1, RWKV_WIDTH), lnw=ln_w[i][None], lnb=ln_b[i][None],
        w_out=w_out[i].astype(BF16), norm_ffn=norm_ffn[i][None], w_gate=w_gate[i].astype(BF16),
        w_up=w_up[i].astype(BF16), w_down=w_down[i].astype(BF16), norm_ple=norm_ple[i][None],
        w_ple_gate=w_ple_gate[i].astype(BF16), w_ple=w_ple[i].astype(BF16))


def _prompt_layer(x, pe, lp, b, t):
    g, hd = NSA_KV_HEADS, HEAD_DIM
    q, kv, kw, gl, xr = _in_proj(x, lp["norm_mix"], lp["w_in"])
    kv = kv.reshape(b, t, KV_COLS)
    kw = kw.reshape(b, t, WIN_COLS)
    xr = xr.reshape(b, t, XR_PAD)
    kc = _compress_prompt(kv, lp["cw"])
    ksv = _kv_prep(kv, kw, lp["kn_sel"], lp["kn_win"])
    o_nsa = _nsa_prompt(q.reshape(b, t, Q_COLS), gl.reshape(b, t, LANES), lp["qn"], kc, ksv, t)
    o_rwkv, s_bd = _rwkv_scan(xr, jnp.zeros((b, 1, XR_PAD), F32), lp["rp"], lp["rk"], lp["lnw"], lp["lnb"])
    x = _out_ffn(x, o_nsa.reshape(b * t, Q_COLS), o_rwkv.reshape(b * t, RWKV_WIDTH), pe, lp)
    keep = min(WINDOW, t)
    return (x, kv.reshape(b, t, 4, g, hd), kw[:, t - keep:].reshape(b, keep, 2, g, hd), _unblock_states(s_bd),
            _unpad_xr(xr[:, -1]))


def _sample_layer(x, pe, lp, cache_t, pt_flat, n_pages, cwin_t, win_off, state_wkv, state_shift):
    db = x.shape[0]
    g, hd = NSA_KV_HEADS, HEAD_DIM
    q, kv, kw, gl, xr = _in_proj(x, lp["norm_mix"], lp["w_in"])
    kcn, _, vcn, _ = _compress_paged(cache_t, pt_flat, db, n_pages, lp["cw"])
    q3 = q.reshape(db, NSA_HEADS, hd)
    z = jnp.zeros((db, NSA_HEADS // 2, hd), F32)
    qm = jnp.concatenate([jnp.concatenate([q3[:, :4], z], -1), jnp.concatenate([z, q3[:, 4:]], -1)], axis=1)
    glm = jnp.pad(gl[:, :GATE_COLS].reshape(db, NSA_HEADS, 3), ((0, 0), (0, 0), (0, LANES - 3)))
    o8, win = _nsa_step(cache_t, pt_flat, qm, glm, kcn, vcn, kv.reshape(db, 4 * g, hd), cwin_t, win_off,
                        kw.reshape(db, 2 * g, hd), lp["step_gains"], n_pages)
    o_nsa = o8.reshape(db, Q_COLS)
    feats = _rwkv_prep_step(xr, _pad_xr(state_shift), lp["rp"])
    o_rwkv, s1 = _rwkv_step(state_wkv, feats, lp["rk"], lp["lnw"], lp["lnb"])
    x = _out_ffn(x, o_nsa, o_rwkv, pe, lp)
    return (x, kv.reshape(db, 1, 4, g, hd), win, s1, _unpad_xr(xr))


def kernel(x_prompt, x_sample, p_prompt, p_sample, cache_kv, cache_win, state_wkv, state_shift, page_table, norm_mix, w_in, q_norm, k_norm, cmp_pe, cmp_w1, cmp_b1, cmp_w2, cmp_b2, shift_mu, w0, w2, a0, a2, g2, k_k, k_a, r_k, ln_w, ln_b, w_out, norm_ffn, w_gate, w_up, w_down, norm_ple, w_ple_gate, w_ple):
    b, t, d = x_prompt.shape
    db = x_sample.shape[0]
    depth, n_phys, page = cache_kv.shape[:3]
    n_pages = page_table.shape[1]
    assert x_sample.shape[1] == 1 and d == D_MODEL
    rows_last = (0, 1, 3, 4, 5, 2)
    cache_t = cache_kv.transpose(rows_last).reshape(depth * n_phys, -1, HEAD_DIM, page)
    wbuf = cache_win.shape[2]
    cwin_t = cache_win.transpose(rows_last).reshape(depth * db, -1, HEAD_DIM, wbuf)
    pt_flat = page_table.reshape(-1).astype(jnp.int32)
    xp = x_prompt.reshape(b * t, d)
    xs = x_sample.reshape(db, d)
    outs_p, outs_s = [], []
    for i in range(depth):
        lp = _layer_params(i, norm_mix, w_in, q_norm, k_norm, cmp_pe, cmp_w1, cmp_b1, cmp_w2, cmp_b2, shift_mu,
                           w0, w2, a0, a2, g2, k_k, k_a, r_k, ln_w, ln_b, w_out, norm_ffn, w_gate, w_up, w_down,
                           norm_ple, w_ple_gate, w_ple)
        xp, *rest_p = _prompt_layer(xp, p_prompt[i].reshape(b * t, -1), lp, b, t)
        xs, *rest_s = _sample_layer(xs, p_sample[i].reshape(db, -1), lp, cache_t, pt_flat + i * n_phys, n_pages,
                                    cwin_t, i * db, state_wkv[i], state_shift[i])
        outs_p.append(rest_p)
        outs_s.append(rest_s)
    stack = lambda outs, k: jnp.stack([o[k] for o in outs])
    g = NSA_KV_HEADS
    win_s = stack(outs_s, 1).reshape(depth, db, 2, g, HEAD_DIM, wbuf).transpose(0, 1, 5, 2, 3, 4)
    return (xp.reshape(b, t, d), xs.reshape(db, 1, d),
            stack(outs_p, 0), stack(outs_s, 0), stack(outs_p, 1), win_s,
            stack(outs_p, 2), stack(outs_s, 2), stack(outs_p, 3), stack(outs_s, 3))
```
